```python
import jax, jax.numpy as jnp
from jax import lax
import numpy as np

D_MODEL = 1024
BATCH = 16
SEQ = 2048
DEPTH = 2
DEC_BATCH = 16
DEC_SEQ = 32
PAST_LEN = 1024

CHUNK = 64
N_PREV_CHUNKS = 8
BAND_PAST = N_PREV_CHUNKS * CHUNK
BAND = BAND_PAST + CHUNK
HEAD_DIM = 64
D_MIX = D_MODEL
D_A = D_MIX // 2
D_B = D_MIX - D_A
N_HEADS_A = D_A // HEAD_DIM
N_HEADS_B = D_B // HEAD_DIM
REL_CLIP = 128
SB_BLOCK = 128
D_PLE = 256
RMS_EPS = 1e-6
D_IN = 4 * D_A + 4 * D_B
ATTN_SCALE = HEAD_DIM ** -0.5
NEG_INF = -1e30
SPLIT_IDX = [int(v) for v in np.cumsum([D_A] * 4 + [D_B] * 4)[:-1]]

kernel_name = "hybrid_band_stickbreaking_stream_step"


def rmsnorm(x, g):
    x32 = x.astype(jnp.float32)
    y = x32 * lax.rsqrt(jnp.mean(x32 * x32, axis=-1, keepdims=True) + RMS_EPS)
    return (y * g.astype(jnp.float32)).astype(x.dtype)


def heads(t, n):
    return t.reshape(t.shape[:-1] + (n, HEAD_DIM))


def in_proj(h, g_pre, w_in):
    u = rmsnorm(h, g_pre) @ w_in
    qa, ka, va, ga, qb, kb, vb, gb = jnp.split(u, SPLIT_IDX, axis=-1)
    return (heads(qa, N_HEADS_A), heads(ka, N_HEADS_A), heads(va, N_HEADS_A), ga,
            heads(qb, N_HEADS_B), heads(kb, N_HEADS_B), heads(vb, N_HEADS_B), gb)


def band_attention(q, k, v, q_pos, k_pos, bias_table):
    s = jnp.einsum('bqhd,bkhd->bhqk', q, k).astype(jnp.float32) * ATTN_SCALE
    rel = jnp.clip(k_pos[None, :] - q_pos[:, None], -REL_CLIP, REL_CLIP) + REL_CLIP
    s = s + bias_table[:, rel].astype(jnp.float32)[None]
    qc = q_pos // CHUNK
    kc = k_pos // CHUNK
    ok = ((k_pos[None, :] >= 0) & (kc[None, :] <= qc[:, None])
          & (kc[None, :] >= qc[:, None] - N_PREV_CHUNKS))
    s = jnp.where(ok[None, None], s, NEG_INF)
    w = jax.nn.softmax(s, axis=-1).astype(v.dtype)
    return jnp.einsum('bhqk,bkhd->bqhd', w, v)


def stick_breaking(q, k, v, q_pos, k_pos):
    z = jnp.einsum('bqhd,bkhd->bhqk', q, k).astype(jnp.float32) * ATTN_SCALE
    causal = (k_pos[None, :] < q_pos[:, None])[None, None]
    log_beta = jax.nn.log_sigmoid(z)
    log_1m = jnp.where(causal, jax.nn.log_sigmoid(-z), 0.0)
    after = lax.cumsum(log_1m, axis=3, reverse=True) - log_1m
    w = jnp.where(causal, jnp.exp(log_beta + after), 0.0).astype(v.dtype)
    return jnp.einsum('bhqk,bkhd->bqhd', w, v)


def mixer_a_prompt(q, k, v, bias_table):
    b, s = q.shape[0], q.shape[1]
    nc = s // CHUNK
    pad = ((0, 0), (BAND_PAST, 0), (0, 0), (0, 0))
    kp = jnp.pad(k, pad)
    vp = jnp.pad(v, pad)

    def one_chunk(c):
        start = c * CHUNK
        qi = lax.dynamic_slice_in_dim(q, start, CHUNK, axis=1)
        ki = lax.dynamic_slice_in_dim(kp, start, BAND, axis=1)
        vi = lax.dynamic_slice_in_dim(vp, start, BAND, axis=1)
        q_pos = start + jnp.arange(CHUNK, dtype=jnp.int32)
        k_pos = start - BAND_PAST + jnp.arange(BAND, dtype=jnp.int32)
        return band_attention(qi, ki, vi, q_pos, k_pos, bias_table)

    o = lax.map(one_chunk, jnp.arange(nc, dtype=jnp.int32))
    return jnp.moveaxis(o, 0, 1).reshape(b, s, N_HEADS_A * HEAD_DIM)


def mixer_b_prompt(q, k, v):
    b, s = q.shape[0], q.shape[1]
    nb = s // SB_BLOCK
    k_pos = jnp.arange(s, dtype=jnp.int32)

    def one_block(i):
        start = i * SB_BLOCK
        qi = lax.dynamic_slice_in_dim(q, start, SB_BLOCK, axis=1)
        q_pos = start + jnp.arange(SB_BLOCK, dtype=jnp.int32)
        return stick_breaking(qi, k, v, q_pos, k_pos)

    o = lax.map(one_block, jnp.arange(nb, dtype=jnp.int32))
    return jnp.moveaxis(o, 0, 1).reshape(b, s, N_HEADS_B * HEAD_DIM)


def mixer_a_sample(q, k, v, ca_k, ca_v, past, bias_table):
    b, t = q.shape[0], q.shape[1]
    la = ca_k.shape[1]
    k_all = jnp.concatenate([ca_k, k], axis=1)
    v_all = jnp.concatenate([ca_v, v], axis=1)
    k_pos = past - la + jnp.arange(la + t, dtype=jnp.int32)
    q_pos = past + jnp.arange(t, dtype=jnp.int32)
    o = band_attention(q, k_all, v_all, q_pos, k_pos, bias_table)
    return o.reshape(b, t, N_HEADS_A * HEAD_DIM)


def mixer_b_sample(q, k, v, cb_k, cb_v):
    b, t = q.shape[0], q.shape[1]
    past = cb_k.shape[1]
    k_all = jnp.concatenate([cb_k, k], axis=1)
    v_all = jnp.concatenate([cb_v, v], axis=1)
    k_pos = jnp.arange(past + t, dtype=jnp.int32)
    q_pos = past + jnp.arange(t, dtype=jnp.int32)
    o = stick_breaking(q, k_all, v_all, q_pos, k_pos)
    return o.reshape(b, t, N_HEADS_B * HEAD_DIM)


def finish(h, oa, ga, ob, gb, w_out, g_post, p, w_ple, w_ple_gate):
    y = jnp.concatenate([oa * jax.nn.silu(ga), ob * jax.nn.silu(gb)], axis=-1) @ w_out
    h = h + rmsnorm(y, g_post)
    return h + jax.nn.sigmoid(h @ w_ple_gate) * (p @ w_ple)


def setup_inputs(seed: int = 0) -> dict:
    key = jax.random.key(seed)
    ks = jax.random.split(key, 16)
    f32 = jnp.float32
    la = min(BAND_PAST, PAST_LEN)
    nrm = lambda k, shape: jax.random.normal(k, shape, f32)
    return {
        "x_prompt": nrm(ks[0], (BATCH, SEQ, D_MODEL)),
        "x_sample": nrm(ks[1], (DEC_BATCH, DEC_SEQ, D_MODEL)),
        "p_prompt": nrm(ks[2], (DEPTH, BATCH, SEQ, D_PLE)),
        "p_sample": nrm(ks[3], (DEPTH, DEC_BATCH, DEC_SEQ, D_PLE)),
        "cache_a_k": nrm(ks[4], (DEPTH, DEC_BATCH, la, N_HEADS_A, HEAD_DIM)),
        "cache_a_v": nrm(ks[5], (DEPTH, DEC_BATCH, la, N_HEADS_A, HEAD_DIM)),
        "cache_b_k": nrm(ks[6], (DEPTH, DEC_BATCH, PAST_LEN, N_HEADS_B, HEAD_DIM)),
        "cache_b_v": nrm(ks[7], (DEPTH, DEC_BATCH, PAST_LEN, N_HEADS_B, HEAD_DIM)),
        "g_pre": 1.0 + 0.05 * nrm(ks[8], (DEPTH, D_MODEL)),
        "w_in": nrm(ks[9], (DEPTH, D_MODEL, D_IN)) * D_MODEL ** -0.5,
        "rel_bias": 0.1 * nrm(ks[10], (DEPTH, N_HEADS_A, 2 * REL_CLIP + 1)),
        "w_out": nrm(ks[11], (DEPTH, D_MIX, D_MODEL)) * D_MIX ** -0.5,
        "g_post": 1.0 + 0.05 * nrm(ks[12], (DEPTH, D_MODEL)),
        "w_ple": nrm(ks[13], (DEPTH, D_PLE, D_MODEL)) * D_PLE ** -0.5,
        "w_ple_gate": nrm(ks[14], (DEPTH, D_MODEL, D_MODEL)) * D_MODEL ** -0.5,
    }


def reference(x_prompt, x_sample, p_prompt, p_sample, cache_a_k, cache_a_v,
              cache_b_k, cache_b_v, g_pre, w_in, rel_bias, w_out, g_post,
              w_ple, w_ple_gate):
    hp, hs = x_prompt, x_sample
    seq = x_prompt.shape[1]
    past = cache_b_k.shape[2]
    keep = min(BAND_PAST, seq)
    pak, pav, pbk, pbv = [], [], [], []
    sak, sav, sbk, sbv = [], [], [], []
    for i in range(DEPTH):
        qa, ka, va, ga, qb, kb, vb, gb = in_proj(hp, g_pre[i], w_in[i])
        oa = mixer_a_prompt(qa, ka, va, rel_bias[i])
        ob = mixer_b_prompt(qb, kb, vb)
        hp = finish(hp, oa, ga, ob, gb, w_out[i], g_post[i], p_prompt[i], w_ple[i], w_ple_gate[i])
        pak.append(ka[:, seq - keep:])
        pav.append(va[:, seq - keep:])
        pbk.append(kb)
        pbv.append(vb)
        qa, ka, va, ga, qb, kb, vb, gb = in_proj(hs, g_pre[i], w_in[i])
        oa = mixer_a_sample(qa, ka, va, cache_a_k[i], cache_a_v[i], past, rel_bias[i])
        ob = mixer_b_sample(qb, kb, vb, cache_b_k[i], cache_b_v[i])
        hs = finish(hs, oa, ga, ob, gb, w_out[i], g_post[i], p_sample[i], w_ple[i], w_ple_gate[i])
        sak.append(ka)
        sav.append(va)
        sbk.append(kb)
        sbv.append(vb)
    return (hp, hs,
            jnp.stack(pak), jnp.stack(pav), jnp.stack(pbk), jnp.stack(pbv),
            jnp.stack(sak), jnp.stack(sav), jnp.stack(sbk), jnp.stack(sbv))
```

```python
import functools

import numpy as np
import jax
import jax.numpy as jnp
from jax import lax
from jax.experimental import pallas as pl
from jax.experimental.pallas import tpu as pltpu

HEAD_DIM = 64
CHUNK = 64
N_PREV_CHUNKS = 8
BAND_PAST = N_PREV_CHUNKS * CHUNK
REL_CLIP = 128
RMS_EPS = 1e-6
NEG_INF = -1e30
ATTN_SCALE = HEAD_DIM ** -0.5

LANES = 128
HEADS_PER_TILE = LANES // HEAD_DIM
VMEM_LIMIT_BYTES = 56 * 1024 * 1024

BAND_Q_ROWS = 2 * CHUNK
BAND_WINDOW = BAND_PAST + BAND_Q_ROWS
BIAS_TABLE_PAD = 384
BIAS_EXT = 768

SB_TILE = 256

BF16 = jnp.bfloat16
F32 = jnp.float32


def _params(n_axes):
    return pltpu.CompilerParams(
        dimension_semantics=("arbitrary",) * n_axes,
        vmem_limit_bytes=VMEM_LIMIT_BYTES)


def _nt_dot(a, b):
    return lax.dot_general(a, b, (((1,), (1,)), ((), ())), preferred_element_type=F32)


def _silu(g):
    return g * (1.0 / (1.0 + jnp.exp(-g)))


def _inproj_kernel(x_ref, g_ref, w_ref,
                   qa_ref, ka_ref, va_ref, ga_ref, qb_ref, kb_ref, vb_ref, gb_ref,
                   ka16_ref, va16_ref, kb16_ref, vb16_ref):
    x = x_ref[...]
    ms = jnp.mean(x * x, axis=-1, keepdims=True)
    hn = (x * lax.rsqrt(ms + RMS_EPS) * g_ref[...]).astype(BF16)
    gw = qa_ref.shape[-1]

    def proj(n):
        return jnp.dot(hn, w_ref[:, n * gw:(n + 1) * gw], preferred_element_type=F32)

    qa_ref[...] = (proj(0) * ATTN_SCALE).astype(BF16)
    u = proj(1)
    ka_ref[...] = u
    ka16_ref[...] = u.astype(BF16)
    u = proj(2)
    va_ref[...] = u
    va16_ref[...] = u.astype(BF16)
    ga_ref[...] = proj(3)
    qb_ref[...] = (proj(4) * ATTN_SCALE).astype(BF16)
    u = proj(5)
    kb_ref[...] = u
    kb16_ref[...] = u.astype(BF16)
    u = proj(6)
    vb_ref[...] = u
    vb16_ref[...] = u.astype(BF16)
    gb_ref[...] = proj(7)


def _inproj(x2d, g_pre, w16, tm):
    t, d = x2d.shape
    d_in = w16.shape[1]
    gw = d_in // 8
    row = lambda i: (i, 0)
    const = lambda i: (0, 0)
    f32_out = jax.ShapeDtypeStruct((t, gw), F32)
    b16_out = jax.ShapeDtypeStruct((t, gw), BF16)
    out_shape = (b16_out, f32_out, f32_out, f32_out, b16_out, f32_out, f32_out, f32_out,
                 b16_out, b16_out, b16_out, b16_out)
    return pl.pallas_call(
        _inproj_kernel,
        out_shape=out_shape,
        grid=(t // tm,),
        in_specs=[pl.BlockSpec((tm, d), row),
                  pl.BlockSpec((1, d), const),
                  pl.BlockSpec((d, d_in), const)],
        out_specs=tuple(pl.BlockSpec((tm, gw), row) for _ in out_shape),
        compiler_params=_params(1),
        name="inproj",
    )(x2d, g_pre.reshape(1, d), w16)


def _finish_kernel(oa_ref, ob_ref, h_ref, p_ref, woa_ref, wob_ref, gpost_ref, wg_ref, wp_ref,
                   out_ref):
    y = jnp.dot(oa_ref[...], woa_ref[...], preferred_element_type=F32)
    y = y + jnp.dot(ob_ref[...], wob_ref[...], preferred_element_type=F32)
    ms = jnp.mean(y * y, axis=-1, keepdims=True)
    h1 = h_ref[...] + y * lax.rsqrt(ms + RMS_EPS) * gpost_ref[...]
    gate_logit = jnp.dot(h1.astype(BF16), wg_ref[...], preferred_element_type=F32)
    gate = 1.0 / (1.0 + jnp.exp(-gate_logit))
    emb = jnp.dot(p_ref[...].astype(BF16), wp_ref[...], preferred_element_type=F32)
    out_ref[...] = h1 + gate * emb


def _finish(oa, ob, h2d, p2d, wo16, g_post, wg16, wp16, tm):
    t, d = h2d.shape
    da = oa.shape[1]
    dp = p2d.shape[1]
    row = lambda i: (i, 0)
    const = lambda i: (0, 0)
    return pl.pallas_call(
        _finish_kernel,
        out_shape=jax.ShapeDtypeStruct((t, d), F32),
        grid=(t // tm,),
        in_specs=[pl.BlockSpec((tm, da), row),
                  pl.BlockSpec((tm, da), row),
                  pl.BlockSpec((tm, d), row),
                  pl.BlockSpec((tm, dp), row),
                  pl.BlockSpec((da, d), const),
                  pl.BlockSpec((da, d), lambda i: (1, 0)),
                  pl.BlockSpec((1, d), const),
                  pl.BlockSpec((d, d), const),
                  pl.BlockSpec((dp, d), const)],
        out_specs=pl.BlockSpec((tm, d), row),
        compiler_params=_params(1),
        name="finish",
    )(oa, ob, h2d, p2d, wo16, wo16, g_post.reshape(1, d), wg16, wp16)


def _build_band_bias(tbl_ref, maskadd_ref, bias_ref):
    n_heads, m_rows, _ = bias_ref.shape
    tbl = tbl_ref[...]
    p1 = tbl.astype(BF16)
    r1 = tbl - p1.astype(F32)
    p2 = r1.astype(BF16)
    p3 = (r1 - p2.astype(F32)).astype(BF16)
    r_idx = lax.broadcasted_iota(jnp.int32, (BIAS_TABLE_PAD, BIAS_EXT), 0)
    m_idx = lax.broadcasted_iota(jnp.int32, (BIAS_TABLE_PAD, BIAS_EXT), 1)
    rel = jnp.clip(m_idx - (m_rows - 1) - BAND_PAST, -REL_CLIP, REL_CLIP) + REL_CLIP
    onehot = (r_idx == rel).astype(BF16)
    t_ext = (jnp.dot(p1, onehot, preferred_element_type=F32)
             + jnp.dot(p2, onehot, preferred_element_type=F32)
             + jnp.dot(p3, onehot, preferred_element_type=F32))
    ql = lax.broadcasted_iota(jnp.int32, (m_rows, BIAS_EXT), 0)
    n_bits = int(m_rows - 1).bit_length()
    for h in range(n_heads):
        x = jnp.broadcast_to(t_ext[h:h + 1, :], (m_rows, BIAS_EXT))
        x = pltpu.roll(x, BIAS_EXT - (m_rows - 1), 1)
        for b in range(n_bits):
            x = jnp.where(((ql >> b) & 1) == 1, pltpu.roll(x, 1 << b, 1), x)
        bias_ref[h] = x[:, :BAND_WINDOW] + maskadd_ref[...]


def _band_step(q_ref, g_ref, kwin, vwin, bias_ref, key_lo, out_ref):
    m_rows = q_ref.shape[0]
    n_tiles = q_ref.shape[1] // LANES
    lane = lax.broadcasted_iota(jnp.int32, (m_rows, LANES), 1)
    if key_lo is not None:
        kl = lax.broadcasted_iota(jnp.int32, (1, BAND_WINDOW), 1)
        key_ok = kl >= key_lo
    for p in range(n_tiles):
        sl = slice(p * LANES, (p + 1) * LANES)
        q_pair = q_ref[:, sl]
        k_pair = kwin[:, sl]
        v_pair = vwin[:, sl]
        o_pair = None
        for hh in range(HEADS_PER_TILE):
            in_head = (lane >= hh * HEAD_DIM) & (lane < (hh + 1) * HEAD_DIM)
            qm = jnp.where(in_head, q_pair, jnp.zeros_like(q_pair))
            s = _nt_dot(qm, k_pair) + bias_ref[p * HEADS_PER_TILE + hh]
            if key_lo is not None:
                s = jnp.where(key_ok, s, NEG_INF)
            mx = jnp.max(s, axis=-1, keepdims=True)
            e = jnp.exp(s - mx)
            w = e * (1.0 / jnp.sum(e, axis=-1, keepdims=True))
            o = jnp.dot(w.astype(BF16), v_pair, preferred_element_type=F32)
            o_pair = o if o_pair is None else jnp.where(in_head, o, o_pair)
        out_ref[:, sl] = (o_pair * _silu(g_ref[:, sl])).astype(BF16)


def _band_prompt_kernel(q_ref, g_ref, k_ref, v_ref, tbl_ref, maskadd_ref, out_ref,
                        kpad_ref, vpad_ref, bias_ref):
    b = pl.program_id(0)
    c = pl.program_id(1)
    seq = k_ref.shape[0]

    @pl.when((b == 0) & (c == 0))
    def _():
        _build_band_bias(tbl_ref, maskadd_ref, bias_ref)
        kpad_ref[0:BAND_PAST, :] = jnp.zeros((BAND_PAST, kpad_ref.shape[1]), BF16)
        vpad_ref[0:BAND_PAST, :] = jnp.zeros((BAND_PAST, vpad_ref.shape[1]), BF16)

    @pl.when(c == 0)
    def _():
        kpad_ref[BAND_PAST:BAND_PAST + seq, :] = k_ref[...]
        vpad_ref[BAND_PAST:BAND_PAST + seq, :] = v_ref[...]

    start = pl.multiple_of(c * BAND_Q_ROWS, BAND_Q_ROWS)
    kwin = kpad_ref[pl.ds(start, BAND_WINDOW), :]
    vwin = vpad_ref[pl.ds(start, BAND_WINDOW), :]
    _band_step(q_ref, g_ref, kwin, vwin, bias_ref, BAND_PAST - start, out_ref)


def _band_static_mask(q_pos, k_pos):
    qc = q_pos // CHUNK
    kc = k_pos // CHUNK
    ok = (kc[None, :] <= qc[:, None]) & (kc[None, :] >= qc[:, None] - N_PREV_CHUNKS)
    return np.where(ok, 0.0, NEG_INF).astype(np.float32)


def _pad_table(rel_bias):
    return jnp.pad(rel_bias, ((0, 0), (0, BIAS_TABLE_PAD - rel_bias.shape[1])))


def _band_prompt(q16, g, k16, v16, rel_bias):
    bsz, seq, da = q16.shape
    n_heads = da // HEAD_DIM
    q_pos = BAND_PAST + np.arange(BAND_Q_ROWS)
    k_pos = np.arange(BAND_WINDOW)
    maskadd = jnp.asarray(_band_static_mask(q_pos, k_pos))
    blk = lambda b, c: (b, c, 0)
    whole = lambda b, c: (b, 0, 0)
    const = lambda b, c: (0, 0)
    return pl.pallas_call(
        _band_prompt_kernel,
        out_shape=jax.ShapeDtypeStruct((bsz, seq, da), BF16),
        grid=(bsz, seq // BAND_Q_ROWS),
        in_specs=[pl.BlockSpec((None, BAND_Q_ROWS, da), blk),
                  pl.BlockSpec((None, BAND_Q_ROWS, da), blk),
                  pl.BlockSpec((None, seq, da), whole),
                  pl.BlockSpec((None, seq, da), whole),
                  pl.BlockSpec((n_heads, BIAS_TABLE_PAD), const),
                  pl.BlockSpec((BAND_Q_ROWS, BAND_WINDOW), const)],
        out_specs=pl.BlockSpec((None, BAND_Q_ROWS, da), blk),
        scratch_shapes=[pltpu.VMEM((BAND_PAST + seq, da), BF16),
                        pltpu.VMEM((BAND_PAST + seq, da), BF16),
                        pltpu.VMEM((n_heads, BAND_Q_ROWS, BAND_WINDOW), F32)],
        compiler_params=_params(2),
        name="band_prompt",
    )(q16, g, k16, v16, _pad_table(rel_bias), maskadd)


def _band_sample_kernel(q_ref, g_ref, ck_ref, cv_ref, nk_ref, nv_ref, tbl_ref, maskadd_ref,
                        out_ref, kwin_ref, vwin_ref, bias_ref):
    b = pl.program_id(0)
    la = ck_ref.shape[0]
    t = nk_ref.shape[0]

    @pl.when(b == 0)
    def _():
        _build_band_bias(tbl_ref, maskadd_ref, bias_ref)
        tail = BAND_WINDOW - (la + t)
        kwin_ref[la + t:, :] = jnp.zeros((tail, kwin_ref.shape[1]), BF16)
        vwin_ref[la + t:, :] = jnp.zeros((tail, vwin_ref.shape[1]), BF16)

    kwin_ref[0:la, :] = ck_ref[...].astype(BF16)
    vwin_ref[0:la, :] = cv_ref[...].astype(BF16)
    kwin_ref[la:la + t, :] = nk_ref[...]
    vwin_ref[la:la + t, :] = nv_ref[...]
    _band_step(q_ref, g_ref, kwin_ref[...], vwin_ref[...], bias_ref, None, out_ref)


def _band_sample(q16, g, cache_k, cache_v, nk16, nv16, rel_bias, past):
    bsz, t, da = q16.shape
    la = cache_k.shape[1]
    n_heads = da // HEAD_DIM
    assert la == BAND_PAST and la + t <= BAND_WINDOW and t % 16 == 0
    q_pos = past + np.arange(t)
    k_pos = past - la + np.arange(BAND_WINDOW)
    maskadd = _band_static_mask(q_pos, k_pos)
    maskadd[:, la + t:] = NEG_INF
    maskadd = jnp.asarray(maskadd)
    blk = lambda b: (b, 0, 0)
    const = lambda b: (0, 0)
    return pl.pallas_call(
        _band_sample_kernel,
        out_shape=jax.ShapeDtypeStruct((bsz, t, da), BF16),
        grid=(bsz,),
        in_specs=[pl.BlockSpec((None, t, da), blk),
                  pl.BlockSpec((None, t, da), blk),
                  pl.BlockSpec((None, la, da), blk),
                  pl.BlockSpec((None, la, da), blk),
                  pl.BlockSpec((None, t, da), blk),
                  pl.BlockSpec((None, t, da), blk),
                  pl.BlockSpec((n_heads, BIAS_TABLE_PAD), const),
                  pl.BlockSpec((t, BAND_WINDOW), const)],
        out_specs=pl.BlockSpec((None, t, da), blk),
        scratch_shapes=[pltpu.VMEM((BAND_WINDOW, da), BF16),
                        pltpu.VMEM((BAND_WINDOW, da), BF16),
                        pltpu.VMEM((n_heads, t, BAND_WINDOW), F32)],
        compiler_params=_params(1),
        name="band_sample",
    )(q16, g, cache_k, cache_v, nk16, nv16, _pad_table(rel_bias), maskadd)


def _suffix_matrix(width):
    j = np.arange(width)[:, None]
    s = np.arange(width)[None, :]
    tri = (j > s).astype(np.float32)
    return jnp.asarray(np.concatenate([tri, tri], axis=0), dtype=BF16)


def _sb_tile(qm, k_t, v_t, tri, carry, acc, causal):
    z = _nt_dot(qm, k_t)
    sp = jnp.maximum(z, 0.0) + jnp.log(1.0 + jnp.exp(-jnp.abs(z)))
    log_beta = z - sp
    if causal is not None:
        sp = jnp.where(causal, sp, 0.0)
    hi = sp.astype(BF16)
    lo = (sp - hi.astype(F32)).astype(BF16)
    after = jnp.dot(jnp.concatenate([hi, lo], axis=1), tri, preferred_element_type=F32)
    w = jnp.exp(log_beta - after - carry)
    if causal is not None:
        w = jnp.where(causal, w, 0.0)
    acc = acc + jnp.dot(w.astype(BF16), v_t, preferred_element_type=F32)
    carry = carry + (after[:, 0:1] + sp[:, 0:1])
    return carry, acc


def _sb_kernel(q_ref, g_ref, kd_ref, vd_ref, kf_ref, vf_ref, trid_ref, trif_ref, out_ref,
               *, n_full_static):
    tq = q_ref.shape[0]
    td = kd_ref.shape[0]
    tk = trif_ref.shape[1]
    n_full = pl.program_id(2) if n_full_static is None else n_full_static
    lane = lax.broadcasted_iota(jnp.int32, (tq, LANES), 1)
    row = lax.broadcasted_iota(jnp.int32, (tq, td), 0)
    col = lax.broadcasted_iota(jnp.int32, (tq, td), 1)
    causal = col < row
    q_pair = q_ref[...]
    kd = kd_ref[...].astype(BF16)
    vd = vd_ref[...].astype(BF16)
    trid = trid_ref[...]
    trif = trif_ref[...]
    o_pair = None
    for hh in range(HEADS_PER_TILE):
        in_head = (lane >= hh * HEAD_DIM) & (lane < (hh + 1) * HEAD_DIM)
        qm = jnp.where(in_head, q_pair, jnp.zeros_like(q_pair))
        carry = jnp.zeros((tq, 1), F32)
        acc = jnp.zeros((tq, LANES), F32)
        carry, acc = _sb_tile(qm, kd, vd, trid, carry, acc, causal)

        def body(it, state):
            carry, acc = state
            start = pl.multiple_of((n_full - 1 - it) * tk, tk)
            k_t = kf_ref[pl.ds(start, tk), :].astype(BF16)
            v_t = vf_ref[pl.ds(start, tk), :].astype(BF16)
            return _sb_tile(qm, k_t, v_t, trif, carry, acc, None)

        carry, acc = lax.fori_loop(0, n_full, body, (carry, acc))
        o_pair = acc if o_pair is None else jnp.where(in_head, acc, o_pair)
    out_ref[...] = (o_pair * _silu(g_ref[...])).astype(BF16)


def _sb_attention(q16, g, kd, vd, kf, vf, *, tq, td, tk, prompt):
    bsz, sq, db = q16.shape
    sf = kf.shape[1]
    n_tiles = db // LANES
    nq = sq // tq
    assert sq % tq == 0 and sf % tk == 0
    if prompt:
        assert tq == td == tk
    qblk = lambda b, p, i: (b, i, p)
    dblk = qblk if prompt else (lambda b, p, i: (b, 0, p))
    whole = lambda b, p, i: (b, 0, p)
    const = lambda b, p, i: (0, 0)
    return pl.pallas_call(
        functools.partial(_sb_kernel, n_full_static=None if prompt else sf // tk),
        out_shape=jax.ShapeDtypeStruct((bsz, sq, db), BF16),
        grid=(bsz, n_tiles, nq),
        in_specs=[pl.BlockSpec((None, tq, LANES), qblk),
                  pl.BlockSpec((None, tq, LANES), qblk),
                  pl.BlockSpec((None, td, LANES), dblk),
                  pl.BlockSpec((None, td, LANES), dblk),
                  pl.BlockSpec((None, sf, LANES), whole),
                  pl.BlockSpec((None, sf, LANES), whole),
                  pl.BlockSpec((2 * td, td), const),
                  pl.BlockSpec((2 * tk, tk), const)],
        out_specs=pl.BlockSpec((None, tq, LANES), qblk),
        compiler_params=_params(3),
        name="sb_prompt" if prompt else "sb_sample",
    )(q16, g, kd, vd, kf, vf, _suffix_matrix(td), _suffix_matrix(tk))


def _row_tile(t):
    return 512 if t % 512 == 0 else t


def kernel(x_prompt, x_sample, p_prompt, p_sample, cache_a_k, cache_a_v, cache_b_k, cache_b_v,
           g_pre, w_in, rel_bias, w_out, g_post, w_ple, w_ple_gate):
    depth = w_in.shape[0]
    bsz, seq, d = x_prompt.shape
    dbsz, dseq, _ = x_sample.shape
    past = cache_b_k.shape[2]
    la = cache_a_k.shape[2]
    keep = min(BAND_PAST, seq)
    da = w_in.shape[2] // 8
    n_heads = da // HEAD_DIM
    tp, ts = bsz * seq, dbsz * dseq
    sample_diag = LANES
    assert dseq <= sample_diag

    hp = x_prompt.reshape(tp, d)
    hs = x_sample.reshape(ts, d)
    pak, pav, pbk, pbv, sak, sav, sbk, sbv = ([] for _ in range(8))
    for i in range(depth):
        w16 = w_in[i].astype(BF16)
        wo16 = w_out[i].astype(BF16)
        wg16 = w_ple_gate[i].astype(BF16)
        wp16 = w_ple[i].astype(BF16)

        (qa, ka, va, ga, qb, kb, vb, gb, ka16, va16, kb16, vb16) = _inproj(
            hp, g_pre[i], w16, _row_tile(tp))
        r3 = lambda a: a.reshape(bsz, seq, da)
        oa = _band_prompt(r3(qa), r3(ga), r3(ka16), r3(va16), rel_bias[i])
        ob = _sb_attention(r3(qb), r3(gb), r3(kb16), r3(vb16), r3(kb16), r3(vb16),
                           tq=SB_TILE, td=SB_TILE, tk=SB_TILE, prompt=True)
        hp = _finish(oa.reshape(tp, da), ob.reshape(tp, da), hp, p_prompt[i].reshape(tp, -1),
                     wo16, g_post[i], wg16, wp16, _row_tile(tp))
        r4 = lambda a: a.reshape(bsz, seq, n_heads, HEAD_DIM)
        pak.append(r4(ka)[:, seq - keep:])
        pav.append(r4(va)[:, seq - keep:])
        pbk.append(r4(kb))
        pbv.append(r4(vb))

        (qa, ka, va, ga, qb, kb, vb, gb, ka16, va16, kb16, vb16) = _inproj(
            hs, g_pre[i], w16, _row_tile(ts))
        s3 = lambda a: a.reshape(dbsz, dseq, da)
        oa = _band_sample(s3(qa), s3(ga), cache_a_k[i].reshape(dbsz, la, da),
                          cache_a_v[i].reshape(dbsz, la, da), s3(ka16), s3(va16),
                          rel_bias[i], past)
        pad_new = lambda a: jnp.pad(s3(a), ((0, 0), (0, sample_diag - dseq), (0, 0)))
        ob = _sb_attention(s3(qb), s3(gb), pad_new(kb16), pad_new(vb16),
                           cache_b_k[i].reshape(dbsz, past, da),
                           cache_b_v[i].reshape(dbsz, past, da),
                           tq=dseq, td=sample_diag, tk=SB_TILE, prompt=False)
        hs = _finish(oa.reshape(ts, da), ob.reshape(ts, da), hs, p_sample[i].reshape(ts, -1),
                     wo16, g_post[i], wg16, wp16, _row_tile(ts))
        s4 = lambda a: a.reshape(dbsz, dseq, n_heads, HEAD_DIM)
        sak.append(s4(ka))
        sav.append(s4(va))
        sbk.append(s4(kb))
        sbv.append(s4(vb))

    return (hp.reshape(bsz, seq, d), hs.reshape(dbsz, dseq, d),
            jnp.stack(pak), jnp.stack(pav), jnp.stack(pbk), jnp.stack(pbv),
            jnp.stack(sak), jnp.stack(sav), jnp.stack(sbk), jnp.stack(sbv))
```

```python
import functools

import numpy as np
import jax
import jax.numpy as jnp
from jax import lax
from jax.experimental import pallas as pl
from jax.experimental.pallas import tpu as pltpu

HEAD_DIM = 64
CHUNK = 64
N_PREV_CHUNKS = 8
BAND_PAST = N_PREV_CHUNKS * CHUNK
REL_CLIP = 128
RMS_EPS = 1e-6
NEG_INF = -1e30
ATTN_SCALE = HEAD_DIM ** -0.5

LANES = 128
HEADS_PER_TILE = LANES // HEAD_DIM
VMEM_LIMIT_BYTES = 56 * 1024 * 1024

BAND_Q_ROWS = 2 * CHUNK
BAND_WINDOW = BAND_PAST + BAND_Q_ROWS
BIAS_TABLE_PAD = 384
BIAS_EXT = 768

SB_TILE = 256
SB_WIDTH = 256
SB_UNDERFLOW = 105.0

BF16 = jnp.bfloat16
F32 = jnp.float32


def _params(n_axes):
    return pltpu.CompilerParams(
        dimension_semantics=("arbitrary",) * n_axes,
        vmem_limit_bytes=VMEM_LIMIT_BYTES)


def _nt_dot(a, b):
    return lax.dot_general(a, b, (((1,), (1,)), ((), ())), preferred_element_type=F32)


def _silu(g):
    return g * (1.0 / (1.0 + jnp.exp(-g)))


def _inproj_kernel(x_ref, g_ref, w_ref,
                   qa_ref, ka_ref, va_ref, ga_ref, qb_ref, kb_ref, vb_ref, gb_ref,
                   ka16_ref, va16_ref, kb16_ref, vb16_ref):
    x = x_ref[...]
    ms = jnp.mean(x * x, axis=-1, keepdims=True)
    hn = (x * lax.rsqrt(ms + RMS_EPS) * g_ref[...]).astype(BF16)
    gw = qa_ref.shape[-1]

    def proj(n):
        return jnp.dot(hn, w_ref[:, n * gw:(n + 1) * gw], preferred_element_type=F32)

    qa_ref[...] = (proj(0) * ATTN_SCALE).astype(BF16)
    u = proj(1)
    ka_ref[...] = u
    ka16_ref[...] = u.astype(BF16)
    u = proj(2)
    va_ref[...] = u
    va16_ref[...] = u.astype(BF16)
    ga_ref[...] = proj(3)
    qb_ref[...] = (proj(4) * ATTN_SCALE).astype(BF16)
    u = proj(5)
    kb_ref[...] = u
    kb16_ref[...] = u.astype(BF16)
    u = proj(6)
    vb_ref[...] = u
    vb16_ref[...] = u.astype(BF16)
    gb_ref[...] = proj(7)


def _inproj(x2d, g_pre, w16, tm):
    t, d = x2d.shape
    d_in = w16.shape[1]
    gw = d_in // 8
    row = lambda i: (i, 0)
    const = lambda i: (0, 0)
    f32_out = jax.ShapeDtypeStruct((t, gw), F32)
    b16_out = jax.ShapeDtypeStruct((t, gw), BF16)
    out_shape = (b16_out, f32_out, f32_out, f32_out, b16_out, f32_out, f32_out, f32_out,
                 b16_out, b16_out, b16_out, b16_out)
    return pl.pallas_call(
        _inproj_kernel,
        out_shape=out_shape,
        grid=(t // tm,),
        in_specs=[pl.BlockSpec((tm, d), row),
                  pl.BlockSpec((1, d), const),
                  pl.BlockSpec((d, d_in), const)],
        out_specs=tuple(pl.BlockSpec((tm, gw), row) for _ in out_shape),
        compiler_params=_params(1),
        name="inproj",
    )(x2d, g_pre.reshape(1, d), w16)


def _finish_kernel(oa_ref, ob_ref, h_ref, p_ref, woa_ref, wob_ref, gpost_ref, wg_ref, wp_ref,
                   out_ref):
    y = jnp.dot(oa_ref[...], woa_ref[...], preferred_element_type=F32)
    y = y + jnp.dot(ob_ref[...], wob_ref[...], preferred_element_type=F32)
    ms = jnp.mean(y * y, axis=-1, keepdims=True)
    h1 = h_ref[...] + y * lax.rsqrt(ms + RMS_EPS) * gpost_ref[...]
    gate_logit = jnp.dot(h1.astype(BF16), wg_ref[...], preferred_element_type=F32)
    gate = 1.0 / (1.0 + jnp.exp(-gate_logit))
    emb = jnp.dot(p_ref[...].astype(BF16), wp_ref[...], preferred_element_type=F32)
    out_ref[...] = h1 + gate * emb


def _finish(oa, ob, h2d, p2d, wo16, g_post, wg16, wp16, tm):
    t, d = h2d.shape
    da = oa.shape[1]
    dp = p2d.shape[1]
    row = lambda i: (i, 0)
    const = lambda i: (0, 0)
    return pl.pallas_call(
        _finish_kernel,
        out_shape=jax.ShapeDtypeStruct((t, d), F32),
        grid=(t // tm,),
        in_specs=[pl.BlockSpec((tm, da), row),
                  pl.BlockSpec((tm, da), row),
                  pl.BlockSpec((tm, d), row),
                  pl.BlockSpec((tm, dp), row),
                  pl.BlockSpec((da, d), const),
                  pl.BlockSpec((da, d), lambda i: (1, 0)),
                  pl.BlockSpec((1, d), const),
                  pl.BlockSpec((d, d), const),
                  pl.BlockSpec((dp, d), const)],
        out_specs=pl.BlockSpec((tm, d), row),
        compiler_params=_params(1),
        name="finish",
    )(oa, ob, h2d, p2d, wo16, wo16, g_post.reshape(1, d), wg16, wp16)


def _build_band_bias(tbl_ref, maskadd_ref, bias_ref):
    n_heads, m_rows, _ = bias_ref.shape
    tbl = tbl_ref[...]
    p1 = tbl.astype(BF16)
    r1 = tbl - p1.astype(F32)
    p2 = r1.astype(BF16)
    p3 = (r1 - p2.astype(F32)).astype(BF16)
    r_idx = lax.broadcasted_iota(jnp.int32, (BIAS_TABLE_PAD, BIAS_EXT), 0)
    m_idx = lax.broadcasted_iota(jnp.int32, (BIAS_TABLE_PAD, BIAS_EXT), 1)
    rel = jnp.clip(m_idx - (m_rows - 1) - BAND_PAST, -REL_CLIP, REL_CLIP) + REL_CLIP
    onehot = (r_idx == rel).astype(BF16)
    t_ext = (jnp.dot(p1, onehot, preferred_element_type=F32)
             + jnp.dot(p2, onehot, preferred_element_type=F32)
             + jnp.dot(p3, onehot, preferred_element_type=F32))
    ql = lax.broadcasted_iota(jnp.int32, (m_rows, BIAS_EXT), 0)
    n_bits = int(m_rows - 1).bit_length()
    for h in range(n_heads):
        x = jnp.broadcast_to(t_ext[h:h + 1, :], (m_rows, BIAS_EXT))
        x = pltpu.roll(x, BIAS_EXT - (m_rows - 1), 1)
        for b in range(n_bits):
            x = jnp.where(((ql >> b) & 1) == 1, pltpu.roll(x, 1 << b, 1), x)
        bias_ref[h] = x[:, :BAND_WINDOW] + maskadd_ref[...]


def _band_step(q_ref, g_ref, kwin, vwin, bias_ref, key_lo, out_ref):
    m_rows = q_ref.shape[0]
    n_tiles = q_ref.shape[1] // LANES
    lane = lax.broadcasted_iota(jnp.int32, (m_rows, LANES), 1)
    if key_lo is not None:
        kl = lax.broadcasted_iota(jnp.int32, (1, BAND_WINDOW), 1)
        key_ok = kl >= key_lo
    for p in range(n_tiles):
        sl = slice(p * LANES, (p + 1) * LANES)
        q_pair = q_ref[:, sl]
        k_pair = kwin[:, sl]
        v_pair = vwin[:, sl]
        o_pair = None
        for hh in range(HEADS_PER_TILE):
            in_head = (lane >= hh * HEAD_DIM) & (lane < (hh + 1) * HEAD_DIM)
            qm = jnp.where(in_head, q_pair, jnp.zeros_like(q_pair))
            s = _nt_dot(qm, k_pair) + bias_ref[p * HEADS_PER_TILE + hh]
            if key_lo is not None:
                s = jnp.where(key_ok, s, NEG_INF)
            mx = jnp.max(s, axis=-1, keepdims=True)
            e = jnp.exp(s - mx)
            w = e * (1.0 / jnp.sum(e, axis=-1, keepdims=True))
            o = jnp.dot(w.astype(BF16), v_pair, preferred_element_type=F32)
            o_pair = o if o_pair is None else jnp.where(in_head, o, o_pair)
        out_ref[:, sl] = (o_pair * _silu(g_ref[:, sl])).astype(BF16)


def _band_prompt_kernel(q_ref, g_ref, k_ref, v_ref, tbl_ref, maskadd_ref, out_ref,
                        kpad_ref, vpad_ref, bias_ref):
    b = pl.program_id(0)
    c = pl.program_id(1)
    seq = k_ref.shape[0]

    @pl.when((b == 0) & (c == 0))
    def _():
        _build_band_bias(tbl_ref, maskadd_ref, bias_ref)
        kpad_ref[0:BAND_PAST, :] = jnp.zeros((BAND_PAST, kpad_ref.shape[1]), BF16)
        vpad_ref[0:BAND_PAST, :] = jnp.zeros((BAND_PAST, vpad_ref.shape[1]), BF16)

    @pl.when(c == 0)
    def _():
        kpad_ref[BAND_PAST:BAND_PAST + seq, :] = k_ref[...]
        vpad_ref[BAND_PAST:BAND_PAST + seq, :] = v_ref[...]

    start = pl.multiple_of(c * BAND_Q_ROWS, BAND_Q_ROWS)
    kwin = kpad_ref[pl.ds(start, BAND_WINDOW), :]
    vwin = vpad_ref[pl.ds(start, BAND_WINDOW), :]
    _band_step(q_ref, g_ref, kwin, vwin, bias_ref, BAND_PAST - start, out_ref)


def _band_static_mask(q_pos, k_pos):
    qc = q_pos // CHUNK
    kc = k_pos // CHUNK
    ok = (kc[None, :] <= qc[:, None]) & (kc[None, :] >= qc[:, None] - N_PREV_CHUNKS)
    return np.where(ok, 0.0, NEG_INF).astype(np.float32)


def _pad_table(rel_bias):
    return jnp.pad(rel_bias, ((0, 0), (0, BIAS_TABLE_PAD - rel_bias.shape[1])))


def _band_prompt(q16, g, k16, v16, rel_bias):
    bsz, seq, da = q16.shape
    n_heads = da // HEAD_DIM
    q_pos = BAND_PAST + np.arange(BAND_Q_ROWS)
    k_pos = np.arange(BAND_WINDOW)
    maskadd = jnp.asarray(_band_static_mask(q_pos, k_pos))
    blk = lambda b, c: (b, c, 0)
    whole = lambda b, c: (b, 0, 0)
    const = lambda b, c: (0, 0)
    return pl.pallas_call(
        _band_prompt_kernel,
        out_shape=jax.ShapeDtypeStruct((bsz, seq, da), BF16),
        grid=(bsz, seq // BAND_Q_ROWS),
        in_specs=[pl.BlockSpec((None, BAND_Q_ROWS, da), blk),
                  pl.BlockSpec((None, BAND_Q_ROWS, da), blk),
                  pl.BlockSpec((None, seq, da), whole),
                  pl.BlockSpec((None, seq, da), whole),
                  pl.BlockSpec((n_heads, BIAS_TABLE_PAD), const),
                  pl.BlockSpec((BAND_Q_ROWS, BAND_WINDOW), const)],
        out_specs=pl.BlockSpec((None, BAND_Q_ROWS, da), blk),
        scratch_shapes=[pltpu.VMEM((BAND_PAST + seq, da), BF16),
                        pltpu.VMEM((BAND_PAST + seq, da), BF16),
                        pltpu.VMEM((n_heads, BAND_Q_ROWS, BAND_WINDOW), F32)],
        compiler_params=_params(2),
        name="band_prompt",
    )(q16, g, k16, v16, _pad_table(rel_bias), maskadd)


def _band_sample_kernel(q_ref, g_ref, ck_ref, cv_ref, nk_ref, nv_ref, tbl_ref, maskadd_ref,
                        out_ref, kwin_ref, vwin_ref, bias_ref):
    b = pl.program_id(0)
    la = ck_ref.shape[0]
    t = nk_ref.shape[0]

    @pl.when(b == 0)
    def _():
        _build_band_bias(tbl_ref, maskadd_ref, bias_ref)
        tail = BAND_WINDOW - (la + t)
        kwin_ref[la + t:, :] = jnp.zeros((tail, kwin_ref.shape[1]), BF16)
        vwin_ref[la + t:, :] = jnp.zeros((tail, vwin_ref.shape[1]), BF16)

    kwin_ref[0:la, :] = ck_ref[...].astype(BF16)
    vwin_ref[0:la, :] = cv_ref[...].astype(BF16)
    kwin_ref[la:la + t, :] = nk_ref[...]
    vwin_ref[la:la + t, :] = nv_ref[...]
    _band_step(q_ref, g_ref, kwin_ref[...], vwin_ref[...], bias_ref, None, out_ref)


def _band_sample(q16, g, cache_k, cache_v, nk16, nv16, rel_bias, past):
    bsz, t, da = q16.shape
    la = cache_k.shape[1]
    n_heads = da // HEAD_DIM
    assert la == BAND_PAST and la + t <= BAND_WINDOW and t % 16 == 0
    q_pos = past + np.arange(t)
    k_pos = past - la + np.arange(BAND_WINDOW)
    maskadd = _band_static_mask(q_pos, k_pos)
    maskadd[:, la + t:] = NEG_INF
    maskadd = jnp.asarray(maskadd)
    blk = lambda b: (b, 0, 0)
    const = lambda b: (0, 0)
    return pl.pallas_call(
        _band_sample_kernel,
        out_shape=jax.ShapeDtypeStruct((bsz, t, da), BF16),
        grid=(bsz,),
        in_specs=[pl.BlockSpec((None, t, da), blk),
                  pl.BlockSpec((None, t, da), blk),
                  pl.BlockSpec((None, la, da), blk),
                  pl.BlockSpec((None, la, da), blk),
                  pl.BlockSpec((None, t, da), blk),
                  pl.BlockSpec((None, t, da), blk),
                  pl.BlockSpec((n_heads, BIAS_TABLE_PAD), const),
                  pl.BlockSpec((t, BAND_WINDOW), const)],
        out_specs=pl.BlockSpec((None, t, da), blk),
        scratch_shapes=[pltpu.VMEM((BAND_WINDOW, da), BF16),
                        pltpu.VMEM((BAND_WINDOW, da), BF16),
                        pltpu.VMEM((n_heads, t, BAND_WINDOW), F32)],
        compiler_params=_params(1),
        name="band_sample",
    )(q16, g, cache_k, cache_v, nk16, nv16, _pad_table(rel_bias), maskadd)


def _suffix_matrix(width):
    j = np.arange(width)[:, None]
    s = np.arange(width)[None, :]
    tri = (j > s).astype(np.float32)
    return jnp.asarray(np.concatenate([tri, tri], axis=0), dtype=BF16)


def _sb_tiles(qms, k_ts, v_ts, tri, state, causal):
    n = len(qms)
    zs = [_nt_dot(qms[i], k_ts[i]) for i in range(n)]
    sps, log_betas, split = [], [], []
    for z in zs:
        sp = jnp.maximum(z, 0.0) + jnp.log(1.0 + jnp.exp(-jnp.abs(z)))
        log_betas.append(z - sp)
        if causal is not None:
            sp = jnp.where(causal, sp, 0.0)
        hi = sp.astype(BF16)
        lo = (sp - hi.astype(F32)).astype(BF16)
        sps.append(sp)
        split.append(jnp.concatenate([hi, lo], axis=1))
    afters = [jnp.dot(s, tri, preferred_element_type=F32) for s in split]
    ws = []
    for i in range(n):
        w = jnp.exp(log_betas[i] - afters[i] - state[2 * i])
        if causal is not None:
            w = jnp.where(causal, w, 0.0)
        ws.append(w.astype(BF16))
    new_state = []
    for i in range(n):
        new_state.append(state[2 * i] + (afters[i][:, 0:1] + sps[i][:, 0:1]))
        new_state.append(state[2 * i + 1]
                         + jnp.dot(ws[i], v_ts[i], preferred_element_type=F32))
    return new_state


def _sb_kernel(q_ref, g_ref, kd_ref, vd_ref, kf_ref, vf_ref, trid_ref, trif_ref, out_ref,
               *, n_full_static):
    tq, width = q_ref.shape
    td = kd_ref.shape[0]
    tk = trif_ref.shape[1]
    n_full = pl.program_id(2) if n_full_static is None else n_full_static
    lane = lax.broadcasted_iota(jnp.int32, (tq, LANES), 1)
    row = lax.broadcasted_iota(jnp.int32, (tq, td), 0)
    col = lax.broadcasted_iota(jnp.int32, (tq, td), 1)
    causal = col < row
    trid = trid_ref[...]
    trif = trif_ref[...]

    heads = [(p, hh) for p in range(width // LANES) for hh in range(HEADS_PER_TILE)]
    in_head = [(lane >= hh * HEAD_DIM) & (lane < (hh + 1) * HEAD_DIM)
               for hh in range(HEADS_PER_TILE)]
    lane_sl = [slice(p * LANES, (p + 1) * LANES) for p, _ in heads]
    qms, state = [], []
    for (p, hh), sl in zip(heads, lane_sl):
        q_pair = q_ref[:, sl]
        qms.append(jnp.where(in_head[hh], q_pair, jnp.zeros_like(q_pair)))
        state += [jnp.zeros((tq, 1), F32), jnp.zeros((tq, LANES), F32)]
    state = _sb_tiles(qms, [kd_ref[:, sl].astype(BF16) for sl in lane_sl],
                      [vd_ref[:, sl].astype(BF16) for sl in lane_sl], trid, state, causal)

    def min_carry(state):
        m = state[0]
        for i in range(1, len(heads)):
            m = jnp.minimum(m, state[2 * i])
        return jnp.min(m)

    def cond(loop_state):
        it, smallest = loop_state[0], loop_state[1]
        return (it < n_full) & (smallest < SB_UNDERFLOW)

    def body(loop_state):
        it, state = loop_state[0], list(loop_state[2:])
        start = pl.multiple_of((n_full - 1 - it) * tk, tk)
        k_ts = [kf_ref[pl.ds(start, tk), sl].astype(BF16) for sl in lane_sl]
        v_ts = [vf_ref[pl.ds(start, tk), sl].astype(BF16) for sl in lane_sl]
        state = _sb_tiles(qms, k_ts, v_ts, trif, state, None)
        return (it + 1, min_carry(state), *state)

    state = lax.while_loop(cond, body, (jnp.int32(0), min_carry(state), *state))[2:]
    for p in range(width // LANES):
        sl = slice(p * LANES, (p + 1) * LANES)
        accs = [state[2 * (p * HEADS_PER_TILE + hh) + 1] for hh in range(HEADS_PER_TILE)]
        o_pair = accs[0]
        for hh in range(1, HEADS_PER_TILE):
            o_pair = jnp.where(in_head[hh], accs[hh], o_pair)
        out_ref[:, sl] = (o_pair * _silu(g_ref[:, sl])).astype(BF16)


def _sb_attention(q16, g, kd, vd, kf, vf, *, tq, td, tk, width, prompt):
    bsz, sq, db = q16.shape
    sf = kf.shape[1]
    n_tiles = db // width
    nq = sq // tq
    assert sq % tq == 0 and sf % tk == 0 and db % width == 0 and width % LANES == 0
    if prompt:
        assert tq == td == tk
    qblk = lambda b, p, i: (b, i, p)
    dblk = qblk if prompt else (lambda b, p, i: (b, 0, p))
    whole = lambda b, p, i: (b, 0, p)
    const = lambda b, p, i: (0, 0)
    return pl.pallas_call(
        functools.partial(_sb_kernel, n_full_static=None if prompt else sf // tk),
        out_shape=jax.ShapeDtypeStruct((bsz, sq, db), BF16),
        grid=(bsz, n_tiles, nq),
        in_specs=[pl.BlockSpec((None, tq, width), qblk),
                  pl.BlockSpec((None, tq, width), qblk),
                  pl.BlockSpec((None, td, width), dblk),
                  pl.BlockSpec((None, td, width), dblk),
                  pl.BlockSpec((None, sf, width), whole),
                  pl.BlockSpec((None, sf, width), whole),
                  pl.BlockSpec((2 * td, td), const),
                  pl.BlockSpec((2 * tk, tk), const)],
        out_specs=pl.BlockSpec((None, tq, width), qblk),
        compiler_params=_params(3),
        name="sb_prompt" if prompt else "sb_sample",
    )(q16, g, kd, vd, kf, vf, _suffix_matrix(td), _suffix_matrix(tk))


def _row_tile(t):
    return 512 if t % 512 == 0 else t


def kernel(x_prompt, x_sample, p_prompt, p_sample, cache_a_k, cache_a_v, cache_b_k, cache_b_v,
           g_pre, w_in, rel_bias, w_out, g_post, w_ple, w_ple_gate):
    depth = w_in.shape[0]
    bsz, seq, d = x_prompt.shape
    dbsz, dseq, _ = x_sample.shape
    past = cache_b_k.shape[2]
    la = cache_a_k.shape[2]
    keep = min(BAND_PAST, seq)
    da = w_in.shape[2] // 8
    n_heads = da // HEAD_DIM
    tp, ts = bsz * seq, dbsz * dseq
    sample_diag = LANES
    assert dseq <= sample_diag

    hp = x_prompt.reshape(tp, d)
    hs = x_sample.reshape(ts, d)
    pak, pav, pbk, pbv, sak, sav, sbk, sbv = ([] for _ in range(8))
    for i in range(depth):
        w16 = w_in[i].astype(BF16)
        wo16 = w_out[i].astype(BF16)
        wg16 = w_ple_gate[i].astype(BF16)
        wp16 = w_ple[i].astype(BF16)

        (qa, ka, va, ga, qb, kb, vb, gb, ka16, va16, kb16, vb16) = _inproj(
            hp, g_pre[i], w16, _row_tile(tp))
        r3 = lambda a: a.reshape(bsz, seq, da)
        oa = _band_prompt(r3(qa), r3(ga), r3(ka16), r3(va16), rel_bias[i])
        ob = _sb_attention(r3(qb), r3(gb), r3(kb16), r3(vb16), r3(kb16), r3(vb16),
                           tq=SB_TILE, td=SB_TILE, tk=SB_TILE, width=SB_WIDTH, prompt=True)
        hp = _finish(oa.reshape(tp, da), ob.reshape(tp, da), hp, p_prompt[i].reshape(tp, -1),
                     wo16, g_post[i], wg16, wp16, _row_tile(tp))
        r4 = lambda a: a.reshape(bsz, seq, n_heads, HEAD_DIM)
        pak.append(r4(ka)[:, seq - keep:])
        pav.append(r4(va)[:, seq - keep:])
        pbk.append(r4(kb))
        pbv.append(r4(vb))

        (qa, ka, va, ga, qb, kb, vb, gb, ka16, va16, kb16, vb16) = _inproj(
            hs, g_pre[i], w16, _row_tile(ts))
        s3 = lambda a: a.reshape(dbsz, dseq, da)
        oa = _band_sample(s3(qa), s3(ga), cache_a_k[i].reshape(dbsz, la, da),
                          cache_a_v[i].reshape(dbsz, la, da), s3(ka16), s3(va16),
                          rel_bias[i], past)
        pad_new = lambda a: jnp.pad(s3(a), ((0, 0), (0, sample_diag - dseq), (0, 0)))
        ob = _sb_attention(s3(qb), s3(gb), pad_new(kb16), pad_new(vb16),
                           cache_b_k[i].reshape(dbsz, past, da),
                           cache_b_v[i].reshape(dbsz, past, da),
                           tq=dseq, td=sample_diag, tk=SB_TILE, width=SB_WIDTH, prompt=False)
        hs = _finish(oa.reshape(ts, da), ob.reshape(ts, da), hs, p_sample[i].reshape(ts, -1),
                     wo16, g_post[i], wg16, wp16, _row_tile(ts))
        s4 = lambda a: a.reshape(dbsz, dseq, n_heads, HEAD_DIM)
        sak.append(s4(ka))
        sav.append(s4(va))
        sbk.append(s4(kb))
        sbv.append(s4(vb))

    return (hp.reshape(bsz, seq, d), hs.reshape(dbsz, dseq, d),
            jnp.stack(pak), jnp.stack(pav), jnp.stack(pbk), jnp.stack(pbv),
            jnp.stack(sak), jnp.stack(sav), jnp.stack(sbk), jnp.stack(sbv))
```

```python
import functools

import numpy as np
import jax
import jax.numpy as jnp
from jax import lax
from jax.experimental import pallas as pl
from jax.experimental.pallas import tpu as pltpu

HEAD_DIM = 64
CHUNK = 64
N_PREV_CHUNKS = 8
BAND_PAST = N_PREV_CHUNKS * CHUNK
REL_CLIP = 128
RMS_EPS = 1e-6
NEG_INF = -1e30
ATTN_SCALE = HEAD_DIM ** -0.5

LANES = 128
HEADS_PER_TILE = LANES // HEAD_DIM
VMEM_LIMIT_BYTES = 56 * 1024 * 1024

KEY_BLOCK = LANES
BAND_Q_ROWS = 2 * CHUNK
BAND_WINDOW = BAND_PAST + BAND_Q_ROWS
BIAS_TABLE_PAD = 384
BIAS_EXT = 768

SB_TILE = 256
SB_WIDTH = 256
SB_UNDERFLOW = 105.0

INPROJ_ROWS = 256
ROW_TILE = 512

BF16 = jnp.bfloat16
F32 = jnp.float32


def _params(n_axes):
    return pltpu.CompilerParams(
        dimension_semantics=("arbitrary",) * n_axes,
        vmem_limit_bytes=VMEM_LIMIT_BYTES)


def _nt_dot(a, b):
    return lax.dot_general(a, b, (((1,), (1,)), ((), ())), preferred_element_type=F32)


def _nn_dot(a, b):
    return jnp.dot(a, b, preferred_element_type=F32)


def _silu(g):
    return g * (1.0 / (1.0 + jnp.exp(-g)))


def _rms_scale(x, gain):
    ms = jnp.mean(x * x, axis=-1, keepdims=True)
    return x * lax.rsqrt(ms + RMS_EPS) * gain


def _head_masks(m_rows):
    lane = lax.broadcasted_iota(jnp.int32, (m_rows, LANES), 1)
    return [(lane >= hh * HEAD_DIM) & (lane < (hh + 1) * HEAD_DIM)
            for hh in range(HEADS_PER_TILE)]


def _stack_heads(q_pair, in_head):
    zero = jnp.zeros_like(q_pair)
    return jnp.concatenate([jnp.where(m, q_pair, zero) for m in in_head], axis=0)


def _unstack_heads(o2, in_head):
    m_rows = o2.shape[0] // HEADS_PER_TILE
    o = o2[0:m_rows]
    for hh in range(1, HEADS_PER_TILE):
        o = jnp.where(in_head[hh], o2[hh * m_rows:(hh + 1) * m_rows], o)
    return o


def _pad_rows(a, rows):
    return jnp.concatenate([a, jnp.zeros((rows - a.shape[0], a.shape[1]), a.dtype)], axis=0)


def _inproj_kernel(x_ref, g_ref, w_ref,
                   qa_ref, ka_ref, va_ref, ga_ref, qb_ref, kb_ref, vb_ref, gb_ref,
                   ka16_ref, va16_ref, kb16_ref, vb16_ref):
    hn = _rms_scale(x_ref[...], g_ref[...]).astype(BF16)
    gw = qa_ref.shape[-1]

    def proj(n):
        return _nn_dot(hn, w_ref[:, n * gw:(n + 1) * gw])

    qa_ref[...] = (proj(0) * ATTN_SCALE).astype(BF16)
    u = proj(1)
    ka_ref[...] = u
    ka16_ref[...] = u.astype(BF16)
    u = proj(2)
    va_ref[...] = u
    va16_ref[...] = u.astype(BF16)
    ga_ref[...] = proj(3)
    qb_ref[...] = (proj(4) * ATTN_SCALE).astype(BF16)
    u = proj(5)
    kb_ref[...] = u
    kb16_ref[...] = u.astype(BF16)
    u = proj(6)
    vb_ref[...] = u
    vb16_ref[...] = u.astype(BF16)
    gb_ref[...] = proj(7)


def _inproj(x2d, g_pre, w16, tm):
    t, d = x2d.shape
    d_in = w16.shape[1]
    gw = d_in // 8
    row = lambda i: (i, 0)
    const = lambda i: (0, 0)
    f32_out = jax.ShapeDtypeStruct((t, gw), F32)
    b16_out = jax.ShapeDtypeStruct((t, gw), BF16)
    out_shape = (b16_out, f32_out, f32_out, f32_out, b16_out, f32_out, f32_out, f32_out,
                 b16_out, b16_out, b16_out, b16_out)
    return pl.pallas_call(
        _inproj_kernel,
        out_shape=out_shape,
        grid=(t // tm,),
        in_specs=[pl.BlockSpec((tm, d), row),
                  pl.BlockSpec((1, d), const),
                  pl.BlockSpec((d, d_in), const)],
        out_specs=tuple(pl.BlockSpec((tm, gw), row) for _ in out_shape),
        compiler_params=_params(1),
        name="inproj_sample",
    )(x2d, g_pre.reshape(1, d), w16)


def _inproj_prompt_kernel(*refs, n_prev, keep_first):
    x_ref, g_ref, wnn_ref, wnt_ref = refs[:4]
    prev = refs[4:4 + (4 if n_prev else 0)]
    (qa_ref, va16_ref, ga_ref, qb_ref, vb16_ref, gb_ref, kat16_ref, kbt16_ref,
     kb_st_ref, vb_st_ref, ka_st_ref, va_st_ref) = refs[4 + len(prev):]
    j = pl.program_id(1)
    gw = qa_ref.shape[-1]
    hn = _rms_scale(x_ref[...], g_ref[...]).astype(BF16)

    def proj(n):
        return _nn_dot(hn, wnn_ref[:, n * gw:(n + 1) * gw])

    qa_ref[...] = (proj(0) * ATTN_SCALE).astype(BF16)
    va16_ref[...] = proj(1).astype(BF16)
    ga_ref[...] = proj(2)
    qb_ref[...] = (proj(3) * ATTN_SCALE).astype(BF16)
    vb16_ref[...] = proj(4).astype(BF16)
    gb_ref[...] = proj(5)

    nt = _nt_dot(wnt_ref[0:3 * gw, :], hn)
    kat, kbt, vbt = nt[0:gw], nt[gw:2 * gw], nt[2 * gw:3 * gw]
    for u in range(kat16_ref.shape[0]):
        blk = slice(u * KEY_BLOCK, (u + 1) * KEY_BLOCK)
        kat16_ref[u] = kat[:, blk].astype(BF16)
        kbt16_ref[u] = kbt[:, blk].astype(BF16)
    for dpt in range(n_prev):
        kb_st_ref[dpt] = prev[0][dpt]
        vb_st_ref[dpt] = prev[1][dpt]
    kb_st_ref[n_prev] = kbt
    vb_st_ref[n_prev] = vbt

    @pl.when(j >= keep_first)
    def _():
        for dpt in range(n_prev):
            ka_st_ref[dpt] = prev[2][dpt]
            va_st_ref[dpt] = prev[3][dpt]
        ka_st_ref[n_prev] = kat
        va_st_ref[n_prev] = _nt_dot(wnt_ref[3 * gw:4 * gw, :], hn)


def _inproj_prompt(x3d, g_pre, wnn16, wnt16, prev, keep):
    bsz, seq, d = x3d.shape
    tm = INPROJ_ROWS
    gw = wnt16.shape[0] // 4
    n_prev = prev[0].shape[0] if prev else 0
    assert seq % tm == 0 and keep % tm == 0 and tm % KEY_BLOCK == 0
    keep_first = (seq - keep) // tm
    tok = lambda b, j: (b, j, 0)
    const = lambda b, j: (0, 0)
    ktb = lambda b, j: (b, j, 0, 0)
    st_all = lambda b, j: (0, b, 0, j)
    st_keep = lambda b, j: (0, b, 0, jnp.maximum(j - keep_first, 0))
    tok_b16 = jax.ShapeDtypeStruct((bsz, seq, gw), BF16)
    tok_f32 = jax.ShapeDtypeStruct((bsz, seq, gw), F32)
    kt_b16 = jax.ShapeDtypeStruct((bsz, seq // KEY_BLOCK, gw, KEY_BLOCK), BF16)
    st_f32 = jax.ShapeDtypeStruct((n_prev + 1, bsz, gw, seq), F32)
    keep_f32 = jax.ShapeDtypeStruct((n_prev + 1, bsz, gw, keep), F32)
    out_shape = (tok_b16, tok_b16, tok_f32, tok_b16, tok_b16, tok_f32, kt_b16, kt_b16,
                 st_f32, st_f32, keep_f32, keep_f32)
    tok_spec = pl.BlockSpec((None, tm, gw), tok)
    kt_spec = pl.BlockSpec((None, tm // KEY_BLOCK, gw, KEY_BLOCK), ktb)
    out_specs = (tok_spec,) * 6 + (kt_spec,) * 2 + (
        pl.BlockSpec((n_prev + 1, None, gw, tm), st_all),
        pl.BlockSpec((n_prev + 1, None, gw, tm), st_all),
        pl.BlockSpec((n_prev + 1, None, gw, tm), st_keep),
        pl.BlockSpec((n_prev + 1, None, gw, tm), st_keep))
    in_specs = [pl.BlockSpec((None, tm, d), tok),
                pl.BlockSpec((1, d), const),
                pl.BlockSpec(wnn16.shape, const),
                pl.BlockSpec(wnt16.shape, const)]
    if n_prev:
        in_specs += [pl.BlockSpec((n_prev, None, gw, tm), st_all),
                     pl.BlockSpec((n_prev, None, gw, tm), st_all),
                     pl.BlockSpec((n_prev, None, gw, tm), st_keep),
                     pl.BlockSpec((n_prev, None, gw, tm), st_keep)]
    return pl.pallas_call(
        functools.partial(_inproj_prompt_kernel, n_prev=n_prev, keep_first=keep_first),
        out_shape=out_shape,
        grid=(bsz, seq // tm),
        in_specs=in_specs,
        out_specs=out_specs,
        compiler_params=_params(2),
        name="inproj_prompt",
    )(x3d, g_pre.reshape(1, d), wnn16, wnt16, *prev)


def _finish_kernel(oa_ref, ob_ref, h_ref, p_ref, woa_ref, wob_ref, gpost_ref, wg_ref, wp_ref,
                   out_ref):
    y = _nn_dot(oa_ref[...], woa_ref[...]) + _nn_dot(ob_ref[...], wob_ref[...])
    h1 = h_ref[...] + _rms_scale(y, gpost_ref[...])
    gate_logit = _nn_dot(h1.astype(BF16), wg_ref[...])
    gate = 1.0 / (1.0 + jnp.exp(-gate_logit))
    emb = _nn_dot(p_ref[...].astype(BF16), wp_ref[...])
    out_ref[...] = h1 + gate * emb


def _finish(oa, ob, h2d, p2d, wo16, g_post, wg16, wp16, tm):
    t, d = h2d.shape
    da = oa.shape[1]
    dp = p2d.shape[1]
    row = lambda i: (i, 0)
    const = lambda i: (0, 0)
    return pl.pallas_call(
        _finish_kernel,
        out_shape=jax.ShapeDtypeStruct((t, d), F32),
        grid=(t // tm,),
        in_specs=[pl.BlockSpec((tm, da), row),
                  pl.BlockSpec((tm, da), row),
                  pl.BlockSpec((tm, d), row),
                  pl.BlockSpec((tm, dp), row),
                  pl.BlockSpec((da, d), const),
                  pl.BlockSpec((da, d), lambda i: (1, 0)),
                  pl.BlockSpec((1, d), const),
                  pl.BlockSpec((d, d), const),
                  pl.BlockSpec((dp, d), const)],
        out_specs=pl.BlockSpec((tm, d), row),
        compiler_params=_params(1),
        name="finish",
    )(oa, ob, h2d, p2d, wo16, wo16, g_post.reshape(1, d), wg16, wp16)


def _build_band_bias(tbl_ref, maskadd_ref, bias_ref):
    m_rows = maskadd_ref.shape[0]
    n_heads = bias_ref.shape[0] // m_rows
    tbl = tbl_ref[...]
    p1 = tbl.astype(BF16)
    r1 = tbl - p1.astype(F32)
    p2 = r1.astype(BF16)
    p3 = (r1 - p2.astype(F32)).astype(BF16)
    r_idx = lax.broadcasted_iota(jnp.int32, (BIAS_TABLE_PAD, BIAS_EXT), 0)
    m_idx = lax.broadcasted_iota(jnp.int32, (BIAS_TABLE_PAD, BIAS_EXT), 1)
    rel = jnp.clip(m_idx - (m_rows - 1) - BAND_PAST, -REL_CLIP, REL_CLIP) + REL_CLIP
    onehot = (r_idx == rel).astype(BF16)
    t_ext = _nn_dot(p1, onehot) + _nn_dot(p2, onehot) + _nn_dot(p3, onehot)
    ql = lax.broadcasted_iota(jnp.int32, (m_rows, BIAS_EXT), 0)
    n_bits = int(m_rows - 1).bit_length()
    for h in range(n_heads):
        x = jnp.broadcast_to(t_ext[h:h + 1, :], (m_rows, BIAS_EXT))
        x = pltpu.roll(x, BIAS_EXT - (m_rows - 1), 1)
        for b in range(n_bits):
            x = jnp.where(((ql >> b) & 1) == 1, pltpu.roll(x, 1 << b, 1), x)
        bias_ref[h * m_rows:(h + 1) * m_rows, :] = x[:, :BAND_WINDOW] + maskadd_ref[...]


def _band_attend(q_ref, g_ref, bias_ref, out_ref, scores, weighted_values, key_lo):
    m_rows = q_ref.shape[0]
    n_tiles = q_ref.shape[1] // LANES
    in_head = _head_masks(m_rows)
    if key_lo is not None:
        kl = lax.broadcasted_iota(jnp.int32, (1, BAND_WINDOW), 1)
        key_ok = kl >= key_lo
    lane_sl = [slice(p * LANES, (p + 1) * LANES) for p in range(n_tiles)]
    rows2 = HEADS_PER_TILE * m_rows

    def tile_scores(p):
        return scores(lane_sl[p], _stack_heads(q_ref[:, lane_sl[p]], in_head))

    s_next = tile_scores(0)
    for p in range(n_tiles):
        s = s_next + bias_ref[p * rows2:(p + 1) * rows2, :]
        if p + 1 < n_tiles:
            s_next = tile_scores(p + 1)
        if key_lo is not None:
            s = jnp.where(key_ok, s, NEG_INF)
        e = jnp.exp(s - jnp.max(s, axis=-1, keepdims=True))
        inv = 1.0 / jnp.sum(e, axis=-1, keepdims=True)
        o2 = weighted_values(lane_sl[p], e.astype(BF16)) * inv
        o_pair = _unstack_heads(o2, in_head)
        out_ref[:, lane_sl[p]] = (o_pair * _silu(g_ref[:, lane_sl[p]])).astype(BF16)


def _band_prompt_kernel(q_ref, g_ref, kt_ref, v_ref, tbl_ref, maskadd_ref, out_ref, bias_ref):
    b = pl.program_id(0)
    c = pl.program_id(1)
    n_prev = BAND_PAST // KEY_BLOCK
    n_blk = BAND_WINDOW // KEY_BLOCK

    @pl.when((b == 0) & (c == 0))
    def _():
        _build_band_bias(tbl_ref, maskadd_ref, bias_ref)

    def run(at_start):
        first = c - n_prev
        blocks = [jnp.maximum(first + t, 0) if at_start else first + t for t in range(n_blk)]

        def scores(sl, q2):
            kt = jnp.concatenate([kt_ref[blk, sl, :] for blk in blocks], axis=1)
            return _nn_dot(q2, kt)

        def weighted_values(sl, e16):
            if at_start:
                v = jnp.concatenate(
                    [v_ref[pl.ds(pl.multiple_of(blk * KEY_BLOCK, KEY_BLOCK), KEY_BLOCK), sl]
                     for blk in blocks], axis=0)
            else:
                v = v_ref[pl.ds(pl.multiple_of(first * KEY_BLOCK, KEY_BLOCK), BAND_WINDOW), sl]
            return _nn_dot(e16, v)

        key_lo = (n_prev - c) * KEY_BLOCK if at_start else None
        _band_attend(q_ref, g_ref, bias_ref, out_ref, scores, weighted_values, key_lo)

    pl.when(c < n_prev)(functools.partial(run, True))
    pl.when(c >= n_prev)(functools.partial(run, False))


def _band_static_mask(q_pos, k_pos):
    qc = q_pos // CHUNK
    kc = k_pos // CHUNK
    ok = (kc[None, :] <= qc[:, None]) & (kc[None, :] >= qc[:, None] - N_PREV_CHUNKS)
    return np.where(ok, 0.0, NEG_INF).astype(np.float32)


def _pad_table(rel_bias):
    return jnp.pad(rel_bias, ((0, 0), (0, BIAS_TABLE_PAD - rel_bias.shape[1])))


def _band_prompt(q16, g, kt16, v16, rel_bias):
    bsz, seq, da = q16.shape
    n_heads = da // HEAD_DIM
    assert BAND_Q_ROWS == KEY_BLOCK and seq % BAND_Q_ROWS == 0
    q_pos = BAND_PAST + np.arange(BAND_Q_ROWS)
    k_pos = np.arange(BAND_WINDOW)
    maskadd = jnp.asarray(_band_static_mask(q_pos, k_pos))
    blk = lambda b, c: (b, c, 0)
    whole = lambda b, c: (b, 0, 0)
    whole4 = lambda b, c: (b, 0, 0, 0)
    const = lambda b, c: (0, 0)
    return pl.pallas_call(
        _band_prompt_kernel,
        out_shape=jax.ShapeDtypeStruct((bsz, seq, da), BF16),
        grid=(bsz, seq // BAND_Q_ROWS),
        in_specs=[pl.BlockSpec((None, BAND_Q_ROWS, da), blk),
                  pl.BlockSpec((None, BAND_Q_ROWS, da), blk),
                  pl.BlockSpec((None, seq // KEY_BLOCK, da, KEY_BLOCK), whole4),
                  pl.BlockSpec((None, seq, da), whole),
                  pl.BlockSpec((n_heads, BIAS_TABLE_PAD), const),
                  pl.BlockSpec((BAND_Q_ROWS, BAND_WINDOW), const)],
        out_specs=pl.BlockSpec((None, BAND_Q_ROWS, da), blk),
        scratch_shapes=[pltpu.VMEM((n_heads * BAND_Q_ROWS, BAND_WINDOW), F32)],
        compiler_params=_params(2),
        name="band_prompt",
    )(q16, g, kt16, v16, _pad_table(rel_bias), maskadd)


def _band_sample_kernel(q_ref, g_ref, ckt_ref, cvt_ref, nk_ref, nv_ref, tbl_ref, maskadd_ref,
                        out_ref, bias_ref):
    b = pl.program_id(0)
    la = ckt_ref.shape[1]

    @pl.when(b == 0)
    def _():
        _build_band_bias(tbl_ref, maskadd_ref, bias_ref)

    def scores(sl, q2):
        s_cache = _nn_dot(q2, ckt_ref[sl, :].astype(BF16))
        s_new = _nt_dot(q2, _pad_rows(nk_ref[:, sl], KEY_BLOCK))
        return jnp.concatenate([s_cache, s_new], axis=1)

    def weighted_values(sl, e16):
        return (_nt_dot(e16[:, :la], cvt_ref[sl, :].astype(BF16))
                + _nn_dot(e16[:, la:], _pad_rows(nv_ref[:, sl], KEY_BLOCK)))

    _band_attend(q_ref, g_ref, bias_ref, out_ref, scores, weighted_values, None)


def _band_sample(q16, g, cache_kt, cache_vt, nk16, nv16, rel_bias, past):
    bsz, t, da = q16.shape
    la = cache_kt.shape[2]
    n_heads = da // HEAD_DIM
    assert la == BAND_PAST and la + KEY_BLOCK == BAND_WINDOW and t <= KEY_BLOCK and t % 16 == 0
    q_pos = past + np.arange(t)
    k_pos = past - la + np.arange(BAND_WINDOW)
    maskadd = _band_static_mask(q_pos, k_pos)
    maskadd[:, la + t:] = NEG_INF
    maskadd = jnp.asarray(maskadd)
    blk = lambda b: (b, 0, 0)
    const = lambda b: (0, 0)
    return pl.pallas_call(
        _band_sample_kernel,
        out_shape=jax.ShapeDtypeStruct((bsz, t, da), BF16),
        grid=(bsz,),
        in_specs=[pl.BlockSpec((None, t, da), blk),
                  pl.BlockSpec((None, t, da), blk),
                  pl.BlockSpec((None, da, la), blk),
                  pl.BlockSpec((None, da, la), blk),
                  pl.BlockSpec((None, t, da), blk),
                  pl.BlockSpec((None, t, da), blk),
                  pl.BlockSpec((n_heads, BIAS_TABLE_PAD), const),
                  pl.BlockSpec((t, BAND_WINDOW), const)],
        out_specs=pl.BlockSpec((None, t, da), blk),
        scratch_shapes=[pltpu.VMEM((n_heads * t, BAND_WINDOW), F32)],
        compiler_params=_params(1),
        name="band_sample",
    )(q16, g, cache_kt, cache_vt, nk16, nv16, _pad_table(rel_bias), maskadd)


def _suffix_matrix(width):
    j = np.arange(width)[:, None]
    s = np.arange(width)[None, :]
    tri = (j > s).astype(np.float32)
    return jnp.asarray(np.concatenate([tri, tri], axis=0), dtype=BF16)


def _sb_tiles(q2s, ks, vs, tri, state, causal, k_is_t, v_is_t):
    n = len(q2s)
    zs = [(_nn_dot if k_is_t else _nt_dot)(q2s[i], ks[i]) for i in range(n)]
    sps, log_betas, split = [], [], []
    for z in zs:
        sp = jnp.maximum(z, 0.0) + jnp.log(1.0 + jnp.exp(-jnp.abs(z)))
        log_betas.append(z - sp)
        if causal is not None:
            sp = jnp.where(causal, sp, 0.0)
        hi = sp.astype(BF16)
        lo = (sp - hi.astype(F32)).astype(BF16)
        sps.append(sp)
        split.append(jnp.concatenate([hi, lo], axis=1))
    afters = [_nn_dot(s, tri) for s in split]
    ws = []
    for i in range(n):
        w = jnp.exp(log_betas[i] - afters[i] - state[2 * i])
        if causal is not None:
            w = jnp.where(causal, w, 0.0)
        ws.append(w.astype(BF16))
    new_state = []
    for i in range(n):
        new_state.append(state[2 * i] + (afters[i][:, 0:1] + sps[i][:, 0:1]))
        new_state.append(state[2 * i + 1] + (_nt_dot if v_is_t else _nn_dot)(ws[i], vs[i]))
    return new_state


def _sb_setup(q_ref, td):
    tq, width = q_ref.shape
    lane_sl = [slice(p * LANES, (p + 1) * LANES) for p in range(width // LANES)]
    in_head = _head_masks(tq)
    rows2 = HEADS_PER_TILE * tq
    row = lax.broadcasted_iota(jnp.int32, (rows2, td), 0) % tq
    col = lax.broadcasted_iota(jnp.int32, (rows2, td), 1)
    causal = col < row
    q2s = [_stack_heads(q_ref[:, sl], in_head) for sl in lane_sl]
    state = []
    for _ in lane_sl:
        state += [jnp.zeros((rows2, 1), F32), jnp.zeros((rows2, LANES), F32)]
    return lane_sl, in_head, causal, q2s, state


def _sb_min_carry(state):
    m = state[0]
    for i in range(2, len(state), 2):
        m = jnp.minimum(m, state[i])
    return jnp.min(m)


def _sb_store(state, lane_sl, in_head, g_ref, out_ref):
    for n, sl in enumerate(lane_sl):
        o_pair = _unstack_heads(state[2 * n + 1], in_head)
        out_ref[:, sl] = (o_pair * _silu(g_ref[:, sl])).astype(BF16)


def _sb_prompt_kernel(q_ref, g_ref, kt_ref, v_ref, tri_ref, out_ref):
    tk = tri_ref.shape[1]
    n_full = pl.program_id(2)
    blocks_per_tile = tk // KEY_BLOCK
    tri = tri_ref[...]
    lane_sl, in_head, causal, q2s, state = _sb_setup(q_ref, tk)

    def load(tile):
        kts = [jnp.concatenate([kt_ref[tile * blocks_per_tile + u, sl, :]
                                for u in range(blocks_per_tile)], axis=1) for sl in lane_sl]
        start = pl.multiple_of(tile * tk, tk)
        return kts, [v_ref[pl.ds(start, tk), sl] for sl in lane_sl]

    kts, vs = load(n_full)
    state = _sb_tiles(q2s, kts, vs, tri, state, causal, True, False)

    def cond(loop_state):
        it, smallest = loop_state[0], loop_state[1]
        return (it < n_full) & (smallest < SB_UNDERFLOW)

    def body(loop_state):
        it, state = loop_state[0], list(loop_state[2:])
        kts, vs = load(n_full - 1 - it)
        state = _sb_tiles(q2s, kts, vs, tri, state, None, True, False)
        return (it + 1, _sb_min_carry(state), *state)

    state = lax.while_loop(cond, body, (jnp.int32(0), _sb_min_carry(state), *state))[2:]
    _sb_store(state, lane_sl, in_head, g_ref, out_ref)


def _sb_prompt(q16, g, kt16, v16):
    bsz, seq, db = q16.shape
    t, width = SB_TILE, SB_WIDTH
    assert seq % t == 0 and db % width == 0 and t % KEY_BLOCK == 0
    qblk = lambda b, p, i: (b, i, p)
    whole = lambda b, p, i: (b, 0, p)
    whole4 = lambda b, p, i: (b, 0, p, 0)
    const = lambda b, p, i: (0, 0)
    return pl.pallas_call(
        _sb_prompt_kernel,
        out_shape=jax.ShapeDtypeStruct((bsz, seq, db), BF16),
        grid=(bsz, db // width, seq // t),
        in_specs=[pl.BlockSpec((None, t, width), qblk),
                  pl.BlockSpec((None, t, width), qblk),
                  pl.BlockSpec((None, seq // KEY_BLOCK, width, KEY_BLOCK), whole4),
                  pl.BlockSpec((None, seq, width), whole),
                  pl.BlockSpec((2 * t, t), const)],
        out_specs=pl.BlockSpec((None, t, width), qblk),
        compiler_params=_params(3),
        name="sb_prompt",
    )(q16, g, kt16, v16, _suffix_matrix(t))


def _sb_sample_kernel(q_ref, g_ref, nk_ref, nv_ref, ckt_ref, cvt_ref, trid_ref, trif_ref,
                      out_ref):
    td = trid_ref.shape[1]
    tk = trif_ref.shape[1]
    n_full = ckt_ref.shape[1] // tk
    trif = trif_ref[...]
    lane_sl, in_head, causal, q2s, state = _sb_setup(q_ref, td)

    state = _sb_tiles(q2s, [_pad_rows(nk_ref[:, sl], td) for sl in lane_sl],
                      [_pad_rows(nv_ref[:, sl], td) for sl in lane_sl],
                      trid_ref[...], state, causal, False, False)

    for tile in reversed(range(n_full)):
        cols = slice(tile * tk, (tile + 1) * tk)

        def step(state, cols=cols):
            kts = [ckt_ref[sl, cols].astype(BF16) for sl in lane_sl]
            vts = [cvt_ref[sl, cols].astype(BF16) for sl in lane_sl]
            return tuple(_sb_tiles(q2s, kts, vts, trif, list(state), None, True, True))

        state = lax.cond(_sb_min_carry(state) < SB_UNDERFLOW, step, lambda s: s, tuple(state))
    _sb_store(state, lane_sl, in_head, g_ref, out_ref)


def _sb_sample(q16, g, nk16, nv16, cache_kt, cache_vt):
    bsz, t, db = q16.shape
    past = cache_kt.shape[2]
    width, tk, td = SB_WIDTH, SB_TILE, KEY_BLOCK
    assert t <= td and t % 16 == 0 and past % tk == 0 and db % width == 0
    blk = lambda b, p: (b, 0, p)
    cblk = lambda b, p: (b, p, 0)
    const = lambda b, p: (0, 0)
    return pl.pallas_call(
        _sb_sample_kernel,
        out_shape=jax.ShapeDtypeStruct((bsz, t, db), BF16),
        grid=(bsz, db // width),
        in_specs=[pl.BlockSpec((None, t, width), blk),
                  pl.BlockSpec((None, t, width), blk),
                  pl.BlockSpec((None, t, width), blk),
                  pl.BlockSpec((None, t, width), blk),
                  pl.BlockSpec((None, width, past), cblk),
                  pl.BlockSpec((None, width, past), cblk),
                  pl.BlockSpec((2 * td, td), const),
                  pl.BlockSpec((2 * tk, tk), const)],
        out_specs=pl.BlockSpec((None, t, width), blk),
        compiler_params=_params(2),
        name="sb_sample",
    )(q16, g, nk16, nv16, cache_kt, cache_vt, _suffix_matrix(td), _suffix_matrix(tk))


def _row_tile(t):
    return ROW_TILE if t % ROW_TILE == 0 else t


def _cache_transposed(cache):
    b, s, h, dh = cache.shape
    return jnp.transpose(cache, (0, 2, 3, 1)).reshape(b, h * dh, s)


def _cache_layout(stacked, n_heads):
    l, b, hd, s = stacked.shape
    return jnp.transpose(stacked.reshape(l, b, n_heads, hd // n_heads, s), (0, 1, 4, 2, 3))


def kernel(x_prompt, x_sample, p_prompt, p_sample, cache_a_k, cache_a_v, cache_b_k, cache_b_v,
           g_pre, w_in, rel_bias, w_out, g_post, w_ple, w_ple_gate):
    depth = w_in.shape[0]
    bsz, seq, d = x_prompt.shape
    dbsz, dseq, _ = x_sample.shape
    past = cache_b_k.shape[2]
    keep = min(BAND_PAST, seq)
    da = w_in.shape[2] // 8
    n_heads = da // HEAD_DIM
    tp, ts = bsz * seq, dbsz * dseq

    hp = x_prompt
    hs = x_sample.reshape(ts, d)
    stacked = ()
    sak, sav, sbk, sbv = [], [], [], []
    for i in range(depth):
        w16 = w_in[i].astype(BF16)
        grp = lambda n: w16[:, n * da:(n + 1) * da]
        wnn16 = jnp.concatenate([grp(0), grp(2), grp(3), grp(4), grp(6), grp(7)], axis=1)
        wnt16 = jnp.concatenate([grp(1), grp(5), grp(6), grp(2)], axis=1).T
        wo16 = w_out[i].astype(BF16)
        wg16 = w_ple_gate[i].astype(BF16)
        wp16 = w_ple[i].astype(BF16)

        (qa, va16, ga, qb, vb16, gb, kat16, kbt16, *stacked) = _inproj_prompt(
            hp, g_pre[i], wnn16, wnt16, tuple(stacked), keep)
        oa = _band_prompt(qa, ga, kat16, va16, rel_bias[i])
        ob = _sb_prompt(qb, gb, kbt16, vb16)
        hp = _finish(oa.reshape(tp, da), ob.reshape(tp, da), hp.reshape(tp, d),
                     p_prompt[i].reshape(tp, -1), wo16, g_post[i], wg16, wp16,
                     _row_tile(tp)).reshape(bsz, seq, d)

        (qa, ka, va, ga, qb, kb, vb, gb, ka16, va16, kb16, vb16) = _inproj(
            hs, g_pre[i], w16, _row_tile(ts))
        s3 = lambda a: a.reshape(dbsz, dseq, da)
        oa = _band_sample(s3(qa), s3(ga), _cache_transposed(cache_a_k[i]),
                          _cache_transposed(cache_a_v[i]), s3(ka16), s3(va16), rel_bias[i], past)
        ob = _sb_sample(s3(qb), s3(gb), s3(kb16), s3(vb16), _cache_transposed(cache_b_k[i]),
                        _cache_transposed(cache_b_v[i]))
        hs = _finish(oa.reshape(ts, da), ob.reshape(ts, da), hs, p_sample[i].reshape(ts, -1),
                     wo16, g_post[i], wg16, wp16, _row_tile(ts))
        s4 = lambda a: a.reshape(dbsz, dseq, n_heads, HEAD_DIM)
        sak.append(s4(ka))
        sav.append(s4(va))
        sbk.append(s4(kb))
        sbv.append(s4(vb))

    kb_st, vb_st, ka_st, va_st = stacked
    return (hp, hs.reshape(dbsz, dseq, d),
            _cache_layout(ka_st, n_heads), _cache_layout(va_st, n_heads),
            _cache_layout(kb_st, n_heads), _cache_layout(vb_st, n_heads),
            jnp.stack(sak), jnp.stack(sav), jnp.stack(sbk), jnp.stack(sbv))
```

```python
import functools

import numpy as np
import jax
import jax.numpy as jnp
from jax import lax
from jax.experimental import pallas as pl
from jax.experimental.pallas import tpu as pltpu

HEAD_DIM = 64
CHUNK = 64
N_PREV_CHUNKS = 8
BAND_PAST = N_PREV_CHUNKS * CHUNK
REL_CLIP = 128
RMS_EPS = 1e-6
NEG_INF = -1e30
ATTN_SCALE = HEAD_DIM ** -0.5

LANES = 128
HEADS_PER_TILE = LANES // HEAD_DIM
VMEM_LIMIT_BYTES = 56 * 1024 * 1024

KEY_BLOCK = LANES
BAND_Q_ROWS = 2 * CHUNK
BAND_STEP_BLOCKS = 4
BAND_WINDOW = BAND_PAST + BAND_Q_ROWS
BIAS_TABLE_PAD = 384
BIAS_EXT = 768

SB_TILE = 256
SB_WIDTH = 512
SB_UNDERFLOW = 105.0

INPROJ_ROWS = 512
ROW_TILE = 1024

BF16 = jnp.bfloat16
F32 = jnp.float32


def _params(n_axes):
    return pltpu.CompilerParams(
        dimension_semantics=("arbitrary",) * n_axes,
        vmem_limit_bytes=VMEM_LIMIT_BYTES)


def _nt_dot(a, b):
    return lax.dot_general(a, b, (((1,), (1,)), ((), ())), preferred_element_type=F32)


def _nn_dot(a, b):
    return jnp.dot(a, b, preferred_element_type=F32)


def _silu(g):
    return g * (1.0 / (1.0 + jnp.exp(-g)))


def _rms_scale(x, gain):
    ms = jnp.mean(x * x, axis=-1, keepdims=True)
    return x * lax.rsqrt(ms + RMS_EPS) * gain


def _head_masks(m_rows):
    lane = lax.broadcasted_iota(jnp.int32, (m_rows, LANES), 1)
    return [(lane >= hh * HEAD_DIM) & (lane < (hh + 1) * HEAD_DIM)
            for hh in range(HEADS_PER_TILE)]


def _stack_heads(q_pair, in_head):
    zero = jnp.zeros_like(q_pair)
    return jnp.concatenate([jnp.where(m, q_pair, zero) for m in in_head], axis=0)


def _unstack_heads(o2, in_head):
    m_rows = o2.shape[0] // HEADS_PER_TILE
    o = o2[0:m_rows]
    for hh in range(1, HEADS_PER_TILE):
        o = jnp.where(in_head[hh], o2[hh * m_rows:(hh + 1) * m_rows], o)
    return o


def _pad_rows(a, rows):
    return jnp.concatenate([a, jnp.zeros((rows - a.shape[0], a.shape[1]), a.dtype)], axis=0)


def _inproj_kernel(x_ref, g_ref, w_ref,
                   qa_ref, ka_ref, va_ref, ga_ref, qb_ref, kb_ref, vb_ref, gb_ref,
                   ka16_ref, va16_ref, kb16_ref, vb16_ref):
    hn = _rms_scale(x_ref[...], g_ref[...]).astype(BF16)
    gw = qa_ref.shape[-1]

    def proj(n):
        return _nn_dot(hn, w_ref[:, n * gw:(n + 1) * gw])

    qa_ref[...] = (proj(0) * ATTN_SCALE).astype(BF16)
    u = proj(1)
    ka_ref[...] = u
    ka16_ref[...] = u.astype(BF16)
    u = proj(2)
    va_ref[...] = u
    va16_ref[...] = u.astype(BF16)
    ga_ref[...] = proj(3)
    qb_ref[...] = (proj(4) * ATTN_SCALE).astype(BF16)
    u = proj(5)
    kb_ref[...] = u
    kb16_ref[...] = u.astype(BF16)
    u = proj(6)
    vb_ref[...] = u
    vb16_ref[...] = u.astype(BF16)
    gb_ref[...] = proj(7)


def _inproj(x2d, g_pre, w16, tm):
    t, d = x2d.shape
    d_in = w16.shape[1]
    gw = d_in // 8
    row = lambda i: (i, 0)
    const = lambda i: (0, 0)
    f32_out = jax.ShapeDtypeStruct((t, gw), F32)
    b16_out = jax.ShapeDtypeStruct((t, gw), BF16)
    out_shape = (b16_out, f32_out, f32_out, f32_out, b16_out, f32_out, f32_out, f32_out,
                 b16_out, b16_out, b16_out, b16_out)
    return pl.pallas_call(
        _inproj_kernel,
        out_shape=out_shape,
        grid=(t // tm,),
        in_specs=[pl.BlockSpec((tm, d), row),
                  pl.BlockSpec((1, d), const),
                  pl.BlockSpec((d, d_in), const)],
        out_specs=tuple(pl.BlockSpec((tm, gw), row) for _ in out_shape),
        compiler_params=_params(1),
        name="inproj_sample",
    )(x2d, g_pre.reshape(1, d), w16)


def _inproj_prompt_kernel(*refs, n_prev, keep_first):
    x_ref, g_ref, wnn_ref, wnt_ref = refs[:4]
    prev = refs[4:4 + (4 if n_prev else 0)]
    (qa_ref, va16_ref, ga_ref, qb_ref, vb16_ref, gb_ref, kat16_ref, kbt16_ref,
     kb_st_ref, vb_st_ref, ka_st_ref, va_st_ref) = refs[4 + len(prev):]
    j = pl.program_id(1)
    gw = qa_ref.shape[-1]
    hn = _rms_scale(x_ref[...], g_ref[...]).astype(BF16)

    def proj(n):
        return _nn_dot(hn, wnn_ref[:, n * gw:(n + 1) * gw])

    qa_ref[...] = (proj(0) * ATTN_SCALE).astype(BF16)
    va16_ref[...] = proj(1).astype(BF16)
    ga_ref[...] = proj(2)
    qb_ref[...] = (proj(3) * ATTN_SCALE).astype(BF16)
    vb16_ref[...] = proj(4).astype(BF16)
    gb_ref[...] = proj(5)

    nt = _nt_dot(wnt_ref[0:3 * gw, :], hn)
    kat, kbt, vbt = nt[0:gw], nt[gw:2 * gw], nt[2 * gw:3 * gw]
    for u in range(kat16_ref.shape[0]):
        blk = slice(u * KEY_BLOCK, (u + 1) * KEY_BLOCK)
        kat16_ref[u] = kat[:, blk].astype(BF16)
        kbt16_ref[u] = kbt[:, blk].astype(BF16)
    for dpt in range(n_prev):
        kb_st_ref[dpt] = prev[0][dpt]
        vb_st_ref[dpt] = prev[1][dpt]
    kb_st_ref[n_prev] = kbt
    vb_st_ref[n_prev] = vbt

    @pl.when(j >= keep_first)
    def _():
        for dpt in range(n_prev):
            ka_st_ref[dpt] = prev[2][dpt]
            va_st_ref[dpt] = prev[3][dpt]
        ka_st_ref[n_prev] = kat
        va_st_ref[n_prev] = _nt_dot(wnt_ref[3 * gw:4 * gw, :], hn)


def _inproj_prompt(x3d, g_pre, wnn16, wnt16, prev, keep):
    bsz, seq, d = x3d.shape
    tm = INPROJ_ROWS
    gw = wnt16.shape[0] // 4
    n_prev = prev[0].shape[0] if prev else 0
    assert seq % tm == 0 and keep % tm == 0 and tm % KEY_BLOCK == 0
    keep_first = (seq - keep) // tm
    tok = lambda b, j: (b, j, 0)
    const = lambda b, j: (0, 0)
    ktb = lambda b, j: (b, j, 0, 0)
    st_all = lambda b, j: (0, b, 0, j)
    st_keep = lambda b, j: (0, b, 0, jnp.maximum(j - keep_first, 0))
    tok_b16 = jax.ShapeDtypeStruct((bsz, seq, gw), BF16)
    tok_f32 = jax.ShapeDtypeStruct((bsz, seq, gw), F32)
    kt_b16 = jax.ShapeDtypeStruct((bsz, seq // KEY_BLOCK, gw, KEY_BLOCK), BF16)
    st_f32 = jax.ShapeDtypeStruct((n_prev + 1, bsz, gw, seq), F32)
    keep_f32 = jax.ShapeDtypeStruct((n_prev + 1, bsz, gw, keep), F32)
    out_shape = (tok_b16, tok_b16, tok_f32, tok_b16, tok_b16, tok_f32, kt_b16, kt_b16,
                 st_f32, st_f32, keep_f32, keep_f32)
    tok_spec = pl.BlockSpec((None, tm, gw), tok)
    kt_spec = pl.BlockSpec((None, tm // KEY_BLOCK, gw, KEY_BLOCK), ktb)
    out_specs = (tok_spec,) * 6 + (kt_spec,) * 2 + (
        pl.BlockSpec((n_prev + 1, None, gw, tm), st_all),
        pl.BlockSpec((n_prev + 1, None, gw, tm), st_all),
        pl.BlockSpec((n_prev + 1, None, gw, tm), st_keep),
        pl.BlockSpec((n_prev + 1, None, gw, tm), st_keep))
    in_specs = [pl.BlockSpec((None, tm, d), tok),
                pl.BlockSpec((1, d), const),
                pl.BlockSpec(wnn16.shape, const, pipeline_mode=pl.Buffered(1)),
                pl.BlockSpec(wnt16.shape, const, pipeline_mode=pl.Buffered(1))]
    if n_prev:
        in_specs += [pl.BlockSpec((n_prev, None, gw, tm), st_all),
                     pl.BlockSpec((n_prev, None, gw, tm), st_all),
                     pl.BlockSpec((n_prev, None, gw, tm), st_keep),
                     pl.BlockSpec((n_prev, None, gw, tm), st_keep)]
    return pl.pallas_call(
        functools.partial(_inproj_prompt_kernel, n_prev=n_prev, keep_first=keep_first),
        out_shape=out_shape,
        grid=(bsz, seq // tm),
        in_specs=in_specs,
        out_specs=out_specs,
        compiler_params=_params(2),
        name="inproj_prompt",
    )(x3d, g_pre.reshape(1, d), wnn16, wnt16, *prev)


def _finish_kernel(oa_ref, ob_ref, h_ref, p_ref, woa_ref, wob_ref, gpost_ref, wg_ref, wp_ref,
                   out_ref):
    y = _nn_dot(oa_ref[...], woa_ref[...]) + _nn_dot(ob_ref[...], wob_ref[...])
    h1 = h_ref[...] + _rms_scale(y, gpost_ref[...])
    gate_logit = _nn_dot(h1.astype(BF16), wg_ref[...])
    gate = 1.0 / (1.0 + jnp.exp(-gate_logit))
    emb = _nn_dot(p_ref[...].astype(BF16), wp_ref[...])
    out_ref[...] = h1 + gate * emb


def _finish(oa, ob, h2d, p_all, layer, wo16, g_post, wg16, wp16, tm):
    t, d = h2d.shape
    da = oa.shape[1]
    dp = p_all.shape[2]
    row = lambda i: (i, 0)
    const = lambda i: (0, 0)
    return pl.pallas_call(
        _finish_kernel,
        out_shape=jax.ShapeDtypeStruct((t, d), F32),
        grid=(t // tm,),
        in_specs=[pl.BlockSpec((tm, da), row),
                  pl.BlockSpec((tm, da), row),
                  pl.BlockSpec((tm, d), row),
                  pl.BlockSpec((None, tm, dp), lambda i: (layer, i, 0)),
                  pl.BlockSpec((da, d), const),
                  pl.BlockSpec((da, d), lambda i: (1, 0)),
                  pl.BlockSpec((1, d), const),
                  pl.BlockSpec((d, d), const),
                  pl.BlockSpec((dp, d), const)],
        out_specs=pl.BlockSpec((tm, d), row),
        compiler_params=_params(1),
        name="finish",
    )(oa, ob, h2d, p_all, wo16, wo16, g_post.reshape(1, d), wg16, wp16)


def _build_band_bias(tbl_ref, maskadd_ref, bias_ref):
    m_rows = maskadd_ref.shape[0]
    n_heads = bias_ref.shape[0] // m_rows
    tbl = tbl_ref[...]
    p1 = tbl.astype(BF16)
    r1 = tbl - p1.astype(F32)
    p2 = r1.astype(BF16)
    p3 = (r1 - p2.astype(F32)).astype(BF16)
    r_idx = lax.broadcasted_iota(jnp.int32, (BIAS_TABLE_PAD, BIAS_EXT), 0)
    m_idx = lax.broadcasted_iota(jnp.int32, (BIAS_TABLE_PAD, BIAS_EXT), 1)
    rel = jnp.clip(m_idx - (m_rows - 1) - BAND_PAST, -REL_CLIP, REL_CLIP) + REL_CLIP
    onehot = (r_idx == rel).astype(BF16)
    t_ext = _nn_dot(p1, onehot) + _nn_dot(p2, onehot) + _nn_dot(p3, onehot)
    ql = lax.broadcasted_iota(jnp.int32, (m_rows, BIAS_EXT), 0)
    n_bits = int(m_rows - 1).bit_length()
    for h in range(n_heads):
        x = jnp.broadcast_to(t_ext[h:h + 1, :], (m_rows, BIAS_EXT))
        x = pltpu.roll(x, BIAS_EXT - (m_rows - 1), 1)
        for b in range(n_bits):
            x = jnp.where(((ql >> b) & 1) == 1, pltpu.roll(x, 1 << b, 1), x)
        bias_ref[h * m_rows:(h + 1) * m_rows, :] = x[:, :BAND_WINDOW] + maskadd_ref[...]


def _band_attend(q_ref, g_ref, bias_ref, out_ref, m_rows, windows):
    n_tiles = q_ref.shape[1] // LANES
    in_head = _head_masks(m_rows)
    kl = lax.broadcasted_iota(jnp.int32, (1, BAND_WINDOW), 1)
    lane_sl = [slice(p * LANES, (p + 1) * LANES) for p in range(n_tiles)]
    row_sl = [slice(j * m_rows, (j + 1) * m_rows) for j in range(len(windows))]
    rows2 = HEADS_PER_TILE * m_rows
    items = [(j, p) for j in range(len(windows)) for p in range(n_tiles)]

    def item_scores(n):
        j, p = items[n]
        return windows[j][0](lane_sl[p], _stack_heads(q_ref[row_sl[j], lane_sl[p]], in_head))

    s_next = item_scores(0)
    for n, (j, p) in enumerate(items):
        s = s_next + bias_ref[p * rows2:(p + 1) * rows2, :]
        if n + 1 < len(items):
            s_next = item_scores(n + 1)
        key_lo = windows[j][2]
        if key_lo is not None:
            s = jnp.where(kl >= key_lo, s, NEG_INF)
        e = jnp.exp(s - jnp.max(s, axis=-1, keepdims=True))
        inv = 1.0 / jnp.sum(e, axis=-1, keepdims=True)
        o2 = windows[j][1](lane_sl[p], e.astype(BF16)) * inv
        o_pair = _unstack_heads(o2, in_head)
        out_ref[row_sl[j], lane_sl[p]] = (
            o_pair * _silu(g_ref[row_sl[j], lane_sl[p]])).astype(BF16)


def _band_prompt_kernel(q_ref, g_ref, kt_ref, v_ref, tbl_ref, maskadd_ref, out_ref, bias_ref):
    b = pl.program_id(0)
    step = pl.program_id(1)
    n_prev = BAND_PAST // KEY_BLOCK
    n_blk = BAND_WINDOW // KEY_BLOCK

    @pl.when((b == 0) & (step == 0))
    def _():
        _build_band_bias(tbl_ref, maskadd_ref, bias_ref)

    def window(c, at_start):
        first = c - n_prev
        blocks = [max(first + t, 0) if at_start else first + t for t in range(n_blk)]

        def scores(sl, q2):
            kt = jnp.concatenate([kt_ref[blk, sl, :] for blk in blocks], axis=1)
            return _nn_dot(q2, kt)

        def weighted_values(sl, e16):
            if at_start:
                v = jnp.concatenate(
                    [v_ref[blk * KEY_BLOCK:(blk + 1) * KEY_BLOCK, sl] for blk in blocks], axis=0)
            else:
                v = v_ref[pl.ds(pl.multiple_of(first * KEY_BLOCK, KEY_BLOCK), BAND_WINDOW), sl]
            return _nn_dot(e16, v)

        return scores, weighted_values, (n_prev - c) * KEY_BLOCK if at_start else None

    @pl.when(step == 0)
    def _():
        _band_attend(q_ref, g_ref, bias_ref, out_ref, BAND_Q_ROWS,
                     [window(j, True) for j in range(BAND_STEP_BLOCKS)])

    @pl.when(step > 0)
    def _():
        _band_attend(q_ref, g_ref, bias_ref, out_ref, BAND_Q_ROWS,
                     [window(step * BAND_STEP_BLOCKS + j, False)
                      for j in range(BAND_STEP_BLOCKS)])


def _band_static_mask(q_pos, k_pos):
    qc = q_pos // CHUNK
    kc = k_pos // CHUNK
    ok = (kc[None, :] <= qc[:, None]) & (kc[None, :] >= qc[:, None] - N_PREV_CHUNKS)
    return np.where(ok, 0.0, NEG_INF).astype(np.float32)


def _pad_table(rel_bias):
    return jnp.pad(rel_bias, ((0, 0), (0, BIAS_TABLE_PAD - rel_bias.shape[1])))


def _band_prompt(q16, g, kt16, v16, rel_bias):
    bsz, seq, da = q16.shape
    n_heads = da // HEAD_DIM
    step_rows = BAND_STEP_BLOCKS * BAND_Q_ROWS
    assert BAND_Q_ROWS == KEY_BLOCK and seq % step_rows == 0
    assert BAND_STEP_BLOCKS == BAND_PAST // KEY_BLOCK
    q_pos = BAND_PAST + np.arange(BAND_Q_ROWS)
    k_pos = np.arange(BAND_WINDOW)
    maskadd = jnp.asarray(_band_static_mask(q_pos, k_pos))
    blk = lambda b, c: (b, c, 0)
    whole = lambda b, c: (b, 0, 0)
    whole4 = lambda b, c: (b, 0, 0, 0)
    const = lambda b, c: (0, 0)
    return pl.pallas_call(
        _band_prompt_kernel,
        out_shape=jax.ShapeDtypeStruct((bsz, seq, da), BF16),
        grid=(bsz, seq // step_rows),
        in_specs=[pl.BlockSpec((None, step_rows, da), blk),
                  pl.BlockSpec((None, step_rows, da), blk),
                  pl.BlockSpec((None, seq // KEY_BLOCK, da, KEY_BLOCK), whole4),
                  pl.BlockSpec((None, seq, da), whole),
                  pl.BlockSpec((n_heads, BIAS_TABLE_PAD), const),
                  pl.BlockSpec((BAND_Q_ROWS, BAND_WINDOW), const)],
        out_specs=pl.BlockSpec((None, step_rows, da), blk),
        scratch_shapes=[pltpu.VMEM((n_heads * BAND_Q_ROWS, BAND_WINDOW), F32)],
        compiler_params=_params(2),
        name="band_prompt",
    )(q16, g, kt16, v16, _pad_table(rel_bias), maskadd)


def _band_sample_kernel(q_ref, g_ref, ckt_ref, cvt_ref, nk_ref, nv_ref, tbl_ref, maskadd_ref,
                        out_ref, bias_ref):
    b = pl.program_id(0)
    la = ckt_ref.shape[1]

    @pl.when(b == 0)
    def _():
        _build_band_bias(tbl_ref, maskadd_ref, bias_ref)

    def scores(sl, q2):
        s_cache = _nn_dot(q2, ckt_ref[sl, :].astype(BF16))
        s_new = _nt_dot(q2, _pad_rows(nk_ref[:, sl], KEY_BLOCK))
        return jnp.concatenate([s_cache, s_new], axis=1)

    def weighted_values(sl, e16):
        return (_nt_dot(e16[:, :la], cvt_ref[sl, :].astype(BF16))
                + _nn_dot(e16[:, la:], _pad_rows(nv_ref[:, sl], KEY_BLOCK)))

    _band_attend(q_ref, g_ref, bias_ref, out_ref, q_ref.shape[0],
                 [(scores, weighted_values, None)])


def _band_sample(q16, g, cache_kt, cache_vt, layer, nk16, nv16, rel_bias, past):
    bsz, t, da = q16.shape
    la = cache_kt.shape[3]
    n_heads = da // HEAD_DIM
    assert la == BAND_PAST and la + KEY_BLOCK == BAND_WINDOW and t <= KEY_BLOCK and t % 16 == 0
    q_pos = past + np.arange(t)
    k_pos = past - la + np.arange(BAND_WINDOW)
    maskadd = _band_static_mask(q_pos, k_pos)
    maskadd[:, la + t:] = NEG_INF
    maskadd = jnp.asarray(maskadd)
    blk = lambda b: (b, 0, 0)
    const = lambda b: (0, 0)
    return pl.pallas_call(
        _band_sample_kernel,
        out_shape=jax.ShapeDtypeStruct((bsz, t, da), BF16),
        grid=(bsz,),
        in_specs=[pl.BlockSpec((None, t, da), blk),
                  pl.BlockSpec((None, t, da), blk),
                  pl.BlockSpec((None, None, da, la), lambda b: (layer, b, 0, 0)),
                  pl.BlockSpec((None, None, da, la), lambda b: (layer, b, 0, 0)),
                  pl.BlockSpec((None, t, da), blk),
                  pl.BlockSpec((None, t, da), blk),
                  pl.BlockSpec((n_heads, BIAS_TABLE_PAD), const),
                  pl.BlockSpec((t, BAND_WINDOW), const)],
        out_specs=pl.BlockSpec((None, t, da), blk),
        scratch_shapes=[pltpu.VMEM((n_heads * t, BAND_WINDOW), F32)],
        compiler_params=_params(1),
        name="band_sample",
    )(q16, g, cache_kt, cache_vt, nk16, nv16, _pad_table(rel_bias), maskadd)


def _suffix_matrix(width):
    j = np.arange(width)[:, None]
    s = np.arange(width)[None, :]
    tri = (j > s).astype(np.float32)
    return jnp.asarray(np.concatenate([tri, tri], axis=0), dtype=BF16)


def _sb_tiles(q2s, ks, vs, tri, state, causal, k_is_t, v_is_t):
    n = len(q2s)
    zs = [(_nn_dot if k_is_t else _nt_dot)(q2s[i], ks[i]) for i in range(n)]
    sps, log_betas, split = [], [], []
    for z in zs:
        sp = jnp.maximum(z, 0.0) + jnp.log(1.0 + jnp.exp(-jnp.abs(z)))
        log_betas.append(z - sp)
        if causal is not None:
            sp = jnp.where(causal, sp, 0.0)
        hi = sp.astype(BF16)
        lo = (sp - hi.astype(F32)).astype(BF16)
        sps.append(sp)
        split.append(jnp.concatenate([hi, lo], axis=1))
    afters = [_nn_dot(s, tri) for s in split]
    ws = []
    for i in range(n):
        w = jnp.exp(log_betas[i] - afters[i] - state[2 * i])
        if causal is not None:
            w = jnp.where(causal, w, 0.0)
        ws.append(w.astype(BF16))
    new_state = []
    for i in range(n):
        new_state.append(state[2 * i] + (afters[i][:, 0:1] + sps[i][:, 0:1]))
        new_state.append(state[2 * i + 1] + (_nt_dot if v_is_t else _nn_dot)(ws[i], vs[i]))
    return new_state


def _sb_setup(q_ref, td):
    tq, width = q_ref.shape
    lane_sl = [slice(p * LANES, (p + 1) * LANES) for p in range(width // LANES)]
    in_head = _head_masks(tq)
    rows2 = HEADS_PER_TILE * tq
    row = lax.broadcasted_iota(jnp.int32, (rows2, td), 0) % tq
    col = lax.broadcasted_iota(jnp.int32, (rows2, td), 1)
    causal = col < row
    q2s = [_stack_heads(q_ref[:, sl], in_head) for sl in lane_sl]
    state = []
    for _ in lane_sl:
        state += [jnp.zeros((rows2, 1), F32), jnp.zeros((rows2, LANES), F32)]
    return lane_sl, in_head, causal, q2s, state


def _sb_min_carry(state):
    m = state[0]
    for i in range(2, len(state), 2):
        m = jnp.minimum(m, state[i])
    return jnp.min(m)


def _sb_store(state, lane_sl, in_head, g_ref, out_ref):
    for n, sl in enumerate(lane_sl):
        o_pair = _unstack_heads(state[2 * n + 1], in_head)
        out_ref[:, sl] = (o_pair * _silu(g_ref[:, sl])).astype(BF16)


def _sb_prompt_kernel(q_ref, g_ref, kt_ref, v_ref, tri_ref, out_ref):
    tk = tri_ref.shape[1]
    n_full = pl.program_id(2)
    blocks_per_tile = tk // KEY_BLOCK
    tri = tri_ref[...]
    lane_sl, in_head, causal, q2s, state = _sb_setup(q_ref, tk)

    def load(tile):
        kts = [jnp.concatenate([kt_ref[tile * blocks_per_tile + u, sl, :]
                                for u in range(blocks_per_tile)], axis=1) for sl in lane_sl]
        start = pl.multiple_of(tile * tk, tk)
        return kts, [v_ref[pl.ds(start, tk), sl] for sl in lane_sl]

    kts, vs = load(n_full)
    state = _sb_tiles(q2s, kts, vs, tri, state, causal, True, False)

    def cond(loop_state):
        it, smallest = loop_state[0], loop_state[1]
        return (it < n_full) & (smallest < SB_UNDERFLOW)

    def body(loop_state):
        it, state = loop_state[0], list(loop_state[2:])
        kts, vs = load(n_full - 1 - it)
        state = _sb_tiles(q2s, kts, vs, tri, state, None, True, False)
        return (it + 1, _sb_min_carry(state), *state)

    state = lax.while_loop(cond, body, (jnp.int32(0), _sb_min_carry(state), *state))[2:]
    _sb_store(state, lane_sl, in_head, g_ref, out_ref)


def _sb_prompt(q16, g, kt16, v16):
    bsz, seq, db = q16.shape
    t, width = SB_TILE, SB_WIDTH
    assert seq % t == 0 and db % width == 0 and t % KEY_BLOCK == 0
    qblk = lambda b, p, i: (b, i, p)
    whole = lambda b, p, i: (b, 0, p)
    whole4 = lambda b, p, i: (b, 0, p, 0)
    const = lambda b, p, i: (0, 0)
    return pl.pallas_call(
        _sb_prompt_kernel,
        out_shape=jax.ShapeDtypeStruct((bsz, seq, db), BF16),
        grid=(bsz, db // width, seq // t),
        in_specs=[pl.BlockSpec((None, t, width), qblk),
                  pl.BlockSpec((None, t, width), qblk),
                  pl.BlockSpec((None, seq // KEY_BLOCK, width, KEY_BLOCK), whole4),
                  pl.BlockSpec((None, seq, width), whole),
                  pl.BlockSpec((2 * t, t), const)],
        out_specs=pl.BlockSpec((None, t, width), qblk),
        compiler_params=_params(3),
        name="sb_prompt",
    )(q16, g, kt16, v16, _suffix_matrix(t))


def _sb_sample_kernel(q_ref, g_ref, nk_ref, nv_ref, ckt_ref, cvt_ref, trid_ref, trif_ref,
                      out_ref):
    td = trid_ref.shape[1]
    tk = trif_ref.shape[1]
    n_full = ckt_ref.shape[1] // tk
    trif = trif_ref[...]
    lane_sl, in_head, causal, q2s, state = _sb_setup(q_ref, td)

    state = _sb_tiles(q2s, [_pad_rows(nk_ref[:, sl], td) for sl in lane_sl],
                      [_pad_rows(nv_ref[:, sl], td) for sl in lane_sl],
                      trid_ref[...], state, causal, False, False)

    for tile in reversed(range(n_full)):
        cols = slice(tile * tk, (tile + 1) * tk)

        def step(state, cols=cols):
            kts = [ckt_ref[sl, cols].astype(BF16) for sl in lane_sl]
            vts = [cvt_ref[sl, cols].astype(BF16) for sl in lane_sl]
            return tuple(_sb_tiles(q2s, kts, vts, trif, list(state), None, True, True))

        state = lax.cond(_sb_min_carry(state) < SB_UNDERFLOW, step, lambda s: s, tuple(state))
    _sb_store(state, lane_sl, in_head, g_ref, out_ref)


def _sb_sample(q16, g, nk16, nv16, cache_kt, cache_vt, layer):
    bsz, t, db = q16.shape
    past = cache_kt.shape[3]
    width, tk, td = SB_WIDTH, SB_TILE, KEY_BLOCK
    assert t <= td and t % 16 == 0 and past % tk == 0 and db % width == 0
    blk = lambda b, p: (b, 0, p)
    cblk = lambda b, p: (layer, b, p, 0)
    const = lambda b, p: (0, 0)
    return pl.pallas_call(
        _sb_sample_kernel,
        out_shape=jax.ShapeDtypeStruct((bsz, t, db), BF16),
        grid=(bsz, db // width),
        in_specs=[pl.BlockSpec((None, t, width), blk),
                  pl.BlockSpec((None, t, width), blk),
                  pl.BlockSpec((None, t, width), blk),
                  pl.BlockSpec((None, t, width), blk),
                  pl.BlockSpec((None, None, width, past), cblk),
                  pl.BlockSpec((None, None, width, past), cblk),
                  pl.BlockSpec((2 * td, td), const),
                  pl.BlockSpec((2 * tk, tk), const)],
        out_specs=pl.BlockSpec((None, t, width), blk),
        compiler_params=_params(2),
        name="sb_sample",
    )(q16, g, nk16, nv16, cache_kt, cache_vt, _suffix_matrix(td), _suffix_matrix(tk))


def _row_tile(t):
    return ROW_TILE if t % ROW_TILE == 0 else t


def _cache_transposed(cache):
    l, b, s, h, dh = cache.shape
    return jnp.transpose(cache, (0, 1, 3, 4, 2)).reshape(l, b, h * dh, s)


def _cache_layout(stacked, n_heads):
    l, b, hd, s = stacked.shape
    return jnp.transpose(stacked.reshape(l, b, n_heads, hd // n_heads, s), (0, 1, 4, 2, 3))


def kernel(x_prompt, x_sample, p_prompt, p_sample, cache_a_k, cache_a_v, cache_b_k, cache_b_v,
           g_pre, w_in, rel_bias, w_out, g_post, w_ple, w_ple_gate):
    depth = w_in.shape[0]
    bsz, seq, d = x_prompt.shape
    dbsz, dseq, _ = x_sample.shape
    past = cache_b_k.shape[2]
    keep = min(BAND_PAST, seq)
    da = w_in.shape[2] // 8
    n_heads = da // HEAD_DIM
    tp, ts = bsz * seq, dbsz * dseq

    hp = x_prompt
    hs = x_sample.reshape(ts, d)
    p_prompt = p_prompt.reshape(depth, tp, -1)
    p_sample = p_sample.reshape(depth, ts, -1)
    cache_a_kt, cache_a_vt, cache_b_kt, cache_b_vt = (
        _cache_transposed(c) for c in (cache_a_k, cache_a_v, cache_b_k, cache_b_v))
    stacked = ()
    sak, sav, sbk, sbv = [], [], [], []
    for i in range(depth):
        w16 = w_in[i].astype(BF16)
        grp = lambda n: w16[:, n * da:(n + 1) * da]
        wnn16 = jnp.concatenate([grp(0), grp(2), grp(3), grp(4), grp(6), grp(7)], axis=1)
        wnt16 = jnp.concatenate([grp(1), grp(5), grp(6), grp(2)], axis=1).T
        wo16 = w_out[i].astype(BF16)
        wg16 = w_ple_gate[i].astype(BF16)
        wp16 = w_ple[i].astype(BF16)

        (qa, va16, ga, qb, vb16, gb, kat16, kbt16, *stacked) = _inproj_prompt(
            hp, g_pre[i], wnn16, wnt16, tuple(stacked), keep)
        oa = _band_prompt(qa, ga, kat16, va16, rel_bias[i])
        ob = _sb_prompt(qb, gb, kbt16, vb16)
        hp = _finish(oa.reshape(tp, da), ob.reshape(tp, da), hp.reshape(tp, d),
                     p_prompt, i, wo16, g_post[i], wg16, wp16,
                     _row_tile(tp)).reshape(bsz, seq, d)

        (qa, ka, va, ga, qb, kb, vb, gb, ka16, va16, kb16, vb16) = _inproj(
            hs, g_pre[i], w16, _row_tile(ts))
        s3 = lambda a: a.reshape(dbsz, dseq, da)
        oa = _band_sample(s3(qa), s3(ga), cache_a_kt, cache_a_vt, i, s3(ka16), s3(va16),
                          rel_bias[i], past)
        ob = _sb_sample(s3(qb), s3(gb), s3(kb16), s3(vb16), cache_b_kt, cache_b_vt, i)
        hs = _finish(oa.reshape(ts, da), ob.reshape(ts, da), hs, p_sample, i,
                     wo16, g_post[i], wg16, wp16, _row_tile(ts))
        s4 = lambda a: a.reshape(dbsz, dseq, n_heads, HEAD_DIM)
        sak.append(s4(ka))
        sav.append(s4(va))
        sbk.append(s4(kb))
        sbv.append(s4(vb))

    kb_st, vb_st, ka_st, va_st = stacked
    return (hp, hs.reshape(dbsz, dseq, d),
            _cache_layout(ka_st, n_heads), _cache_layout(va_st, n_heads),
            _cache_layout(kb_st, n_heads), _cache_layout(vb_st, n_heads),
            jnp.stack(sak), jnp.stack(sav), jnp.stack(sbk), jnp.stack(sbv))
```

```python
import functools

import numpy as np
import jax
import jax.numpy as jnp
from jax import lax
from jax.experimental import pallas as pl
from jax.experimental.pallas import tpu as pltpu

HEAD_DIM = 64
CHUNK = 64
N_PREV_CHUNKS = 8
BAND_PAST = N_PREV_CHUNKS * CHUNK
REL_CLIP = 128
RMS_EPS = 1e-6
NEG_INF = -1e30
ATTN_SCALE = HEAD_DIM ** -0.5

LANES = 128
HEADS_PER_TILE = LANES // HEAD_DIM
VMEM_LIMIT_BYTES = 56 * 1024 * 1024

KEY_BLOCK = LANES
BAND_Q_ROWS = 2 * CHUNK
BAND_STEP_BLOCKS = 4
BAND_GROUP = 4
BAND_WINDOW = BAND_PAST + BAND_Q_ROWS
BIAS_TABLE_PAD = 384
BIAS_EXT = 768

SB_TILE = 256
SB_WIDTH = 512
SB_UNDERFLOW = 105.0

INPROJ_ROWS = 512
ROW_TILE = 1024

BF16 = jnp.bfloat16
F32 = jnp.float32


def _params(n_axes):
    return pltpu.CompilerParams(
        dimension_semantics=("arbitrary",) * n_axes,
        vmem_limit_bytes=VMEM_LIMIT_BYTES)


def _nt_dot(a, b):
    return lax.dot_general(a, b, (((1,), (1,)), ((), ())), preferred_element_type=F32)


def _nn_dot(a, b):
    return jnp.dot(a, b, preferred_element_type=F32)


def _silu(g):
    return g * (1.0 / (1.0 + jnp.exp(-g)))


def _rms_scale(x, gain):
    ms = jnp.mean(x * x, axis=-1, keepdims=True)
    return x * lax.rsqrt(ms + RMS_EPS) * gain


def _head_masks(m_rows):
    lane = lax.broadcasted_iota(jnp.int32, (m_rows, LANES), 1)
    return [(lane >= hh * HEAD_DIM) & (lane < (hh + 1) * HEAD_DIM)
            for hh in range(HEADS_PER_TILE)]


def _stack_heads(q_pair, in_head):
    zero = jnp.zeros_like(q_pair)
    return jnp.concatenate([jnp.where(m, q_pair, zero) for m in in_head], axis=0)


def _unstack_heads(o2, in_head):
    m_rows = o2.shape[0] // HEADS_PER_TILE
    o = o2[0:m_rows]
    for hh in range(1, HEADS_PER_TILE):
        o = jnp.where(in_head[hh], o2[hh * m_rows:(hh + 1) * m_rows], o)
    return o


def _pad_rows(a, rows):
    return jnp.concatenate([a, jnp.zeros((rows - a.shape[0], a.shape[1]), a.dtype)], axis=0)


def _inproj_kernel(x_ref, g_ref, w_ref,
                   qa_ref, ka_ref, va_ref, ga_ref, qb_ref, kb_ref, vb_ref, gb_ref,
                   ka16_ref, va16_ref, kb16_ref, vb16_ref):
    hn = _rms_scale(x_ref[...], g_ref[...]).astype(BF16)
    gw = qa_ref.shape[-1]

    def proj(n):
        return _nn_dot(hn, w_ref[:, n * gw:(n + 1) * gw])

    qa_ref[...] = (proj(0) * ATTN_SCALE).astype(BF16)
    u = proj(1)
    ka_ref[...] = u
    ka16_ref[...] = u.astype(BF16)
    u = proj(2)
    va_ref[...] = u
    va16_ref[...] = u.astype(BF16)
    ga_ref[...] = proj(3)
    qb_ref[...] = (proj(4) * ATTN_SCALE).astype(BF16)
    u = proj(5)
    kb_ref[...] = u
    kb16_ref[...] = u.astype(BF16)
    u = proj(6)
    vb_ref[...] = u
    vb16_ref[...] = u.astype(BF16)
    gb_ref[...] = proj(7)


def _inproj(x2d, g_pre, w16, tm):
    t, d = x2d.shape
    d_in = w16.shape[1]
    gw = d_in // 8
    row = lambda i: (i, 0)
    const = lambda i: (0, 0)
    f32_out = jax.ShapeDtypeStruct((t, gw), F32)
    b16_out = jax.ShapeDtypeStruct((t, gw), BF16)
    out_shape = (b16_out, f32_out, f32_out, f32_out, b16_out, f32_out, f32_out, f32_out,
                 b16_out, b16_out, b16_out, b16_out)
    return pl.pallas_call(
        _inproj_kernel,
        out_shape=out_shape,
        grid=(t // tm,),
        in_specs=[pl.BlockSpec((tm, d), row),
                  pl.BlockSpec((1, d), const),
                  pl.BlockSpec((d, d_in), const)],
        out_specs=tuple(pl.BlockSpec((tm, gw), row) for _ in out_shape),
        compiler_params=_params(1),
        name="inproj_sample",
    )(x2d, g_pre.reshape(1, d), w16)


def _inproj_prompt_kernel(*refs, n_prev, keep_first):
    x_ref, g_ref, wnn_ref, wnt_ref = refs[:4]
    prev = refs[4:4 + (4 if n_prev else 0)]
    (qa_ref, va16_ref, ga_ref, qb_ref, vb16_ref, gb_ref, kat16_ref, kbt16_ref,
     kb_st_ref, vb_st_ref, ka_st_ref, va_st_ref) = refs[4 + len(prev):]
    j = pl.program_id(1)
    gw = qa_ref.shape[-1]
    hn = _rms_scale(x_ref[...], g_ref[...]).astype(BF16)

    def proj(n):
        return _nn_dot(hn, wnn_ref[:, n * gw:(n + 1) * gw])

    qa_ref[...] = (proj(0) * ATTN_SCALE).astype(BF16)
    va16_ref[...] = proj(1).astype(BF16)
    ga_ref[...] = proj(2)
    qb_ref[...] = (proj(3) * ATTN_SCALE).astype(BF16)
    vb16_ref[...] = proj(4).astype(BF16)
    gb_ref[...] = proj(5)

    nt = _nt_dot(wnt_ref[0:3 * gw, :], hn)
    kat, kbt, vbt = nt[0:gw], nt[gw:2 * gw], nt[2 * gw:3 * gw]
    for u in range(kat16_ref.shape[0]):
        blk = slice(u * KEY_BLOCK, (u + 1) * KEY_BLOCK)
        kat16_ref[u] = kat[:, blk].astype(BF16)
        kbt16_ref[u] = kbt[:, blk].astype(BF16)
    for dpt in range(n_prev):
        kb_st_ref[dpt] = prev[0][dpt]
        vb_st_ref[dpt] = prev[1][dpt]
    kb_st_ref[n_prev] = kbt
    vb_st_ref[n_prev] = vbt

    @pl.when(j >= keep_first)
    def _():
        for dpt in range(n_prev):
            ka_st_ref[dpt] = prev[2][dpt]
            va_st_ref[dpt] = prev[3][dpt]
        ka_st_ref[n_prev] = kat
        va_st_ref[n_prev] = _nt_dot(wnt_ref[3 * gw:4 * gw, :], hn)


def _inproj_prompt(x3d, g_pre, wnn16, wnt16, prev, keep):
    bsz, seq, d = x3d.shape
    tm = INPROJ_ROWS
    gw = wnt16.shape[0] // 4
    n_prev = prev[0].shape[0] if prev else 0
    assert seq % tm == 0 and keep % tm == 0 and tm % KEY_BLOCK == 0
    keep_first = (seq - keep) // tm
    tok = lambda b, j: (b, j, 0)
    const = lambda b, j: (0, 0)
    ktb = lambda b, j: (b, j, 0, 0)
    st_all = lambda b, j: (0, b, 0, j)
    st_keep = lambda b, j: (0, b, 0, jnp.maximum(j - keep_first, 0))
    tok_b16 = jax.ShapeDtypeStruct((bsz, seq, gw), BF16)
    tok_f32 = jax.ShapeDtypeStruct((bsz, seq, gw), F32)
    kt_b16 = jax.ShapeDtypeStruct((bsz, seq // KEY_BLOCK, gw, KEY_BLOCK), BF16)
    st_f32 = jax.ShapeDtypeStruct((n_prev + 1, bsz, gw, seq), F32)
    keep_f32 = jax.ShapeDtypeStruct((n_prev + 1, bsz, gw, keep), F32)
    out_shape = (tok_b16, tok_b16, tok_f32, tok_b16, tok_b16, tok_f32, kt_b16, kt_b16,
                 st_f32, st_f32, keep_f32, keep_f32)
    tok_spec = pl.BlockSpec((None, tm, gw), tok)
    kt_spec = pl.BlockSpec((None, tm // KEY_BLOCK, gw, KEY_BLOCK), ktb)
    out_specs = (tok_spec,) * 6 + (kt_spec,) * 2 + (
        pl.BlockSpec((n_prev + 1, None, gw, tm), st_all),
        pl.BlockSpec((n_prev + 1, None, gw, tm), st_all),
        pl.BlockSpec((n_prev + 1, None, gw, tm), st_keep),
        pl.BlockSpec((n_prev + 1, None, gw, tm), st_keep))
    in_specs = [pl.BlockSpec((None, tm, d), tok),
                pl.BlockSpec((1, d), const),
                pl.BlockSpec(wnn16.shape, const, pipeline_mode=pl.Buffered(1)),
                pl.BlockSpec(wnt16.shape, const, pipeline_mode=pl.Buffered(1))]
    if n_prev:
        in_specs += [pl.BlockSpec((n_prev, None, gw, tm), st_all),
                     pl.BlockSpec((n_prev, None, gw, tm), st_all),
                     pl.BlockSpec((n_prev, None, gw, tm), st_keep),
                     pl.BlockSpec((n_prev, None, gw, tm), st_keep)]
    return pl.pallas_call(
        functools.partial(_inproj_prompt_kernel, n_prev=n_prev, keep_first=keep_first),
        out_shape=out_shape,
        grid=(bsz, seq // tm),
        in_specs=in_specs,
        out_specs=out_specs,
        compiler_params=_params(2),
        name="inproj_prompt",
    )(x3d, g_pre.reshape(1, d), wnn16, wnt16, *prev)


def _finish_kernel(oa_ref, ob_ref, h_ref, p_ref, woa_ref, wob_ref, gpost_ref, wg_ref, wp_ref,
                   out_ref):
    y = _nn_dot(oa_ref[...], woa_ref[...]) + _nn_dot(ob_ref[...], wob_ref[...])
    h1 = h_ref[...] + _rms_scale(y, gpost_ref[...])
    gate_logit = _nn_dot(h1.astype(BF16), wg_ref[...])
    gate = 1.0 / (1.0 + jnp.exp(-gate_logit))
    emb = _nn_dot(p_ref[...].astype(BF16), wp_ref[...])
    out_ref[...] = h1 + gate * emb


def _finish(oa, ob, h2d, p_all, layer, wo16, g_post, wg16, wp16, tm):
    t, d = h2d.shape
    da = oa.shape[1]
    dp = p_all.shape[2]
    row = lambda i: (i, 0)
    const = lambda i: (0, 0)
    return pl.pallas_call(
        _finish_kernel,
        out_shape=jax.ShapeDtypeStruct((t, d), F32),
        grid=(t // tm,),
        in_specs=[pl.BlockSpec((tm, da), row),
                  pl.BlockSpec((tm, da), row),
                  pl.BlockSpec((tm, d), row),
                  pl.BlockSpec((None, tm, dp), lambda i: (layer, i, 0)),
                  pl.BlockSpec((da, d), const),
                  pl.BlockSpec((da, d), lambda i: (1, 0)),
                  pl.BlockSpec((1, d), const),
                  pl.BlockSpec((d, d), const),
                  pl.BlockSpec((dp, d), const)],
        out_specs=pl.BlockSpec((tm, d), row),
        compiler_params=_params(1),
        name="finish",
    )(oa, ob, h2d, p_all, wo16, wo16, g_post.reshape(1, d), wg16, wp16)


def _build_band_bias(tbl_ref, maskadd_ref, bias_ref):
    m_rows = maskadd_ref.shape[0]
    n_heads = bias_ref.shape[0] // m_rows
    tbl = tbl_ref[...]
    p1 = tbl.astype(BF16)
    r1 = tbl - p1.astype(F32)
    p2 = r1.astype(BF16)
    p3 = (r1 - p2.astype(F32)).astype(BF16)
    r_idx = lax.broadcasted_iota(jnp.int32, (BIAS_TABLE_PAD, BIAS_EXT), 0)
    m_idx = lax.broadcasted_iota(jnp.int32, (BIAS_TABLE_PAD, BIAS_EXT), 1)
    rel = jnp.clip(m_idx - (m_rows - 1) - BAND_PAST, -REL_CLIP, REL_CLIP) + REL_CLIP
    onehot = (r_idx == rel).astype(BF16)
    t_ext = _nn_dot(p1, onehot) + _nn_dot(p2, onehot) + _nn_dot(p3, onehot)
    ql = lax.broadcasted_iota(jnp.int32, (m_rows, BIAS_EXT), 0)
    n_bits = int(m_rows - 1).bit_length()
    for h in range(n_heads):
        x = jnp.broadcast_to(t_ext[h:h + 1, :], (m_rows, BIAS_EXT))
        x = pltpu.roll(x, BIAS_EXT - (m_rows - 1), 1)
        for b in range(n_bits):
            x = jnp.where(((ql >> b) & 1) == 1, pltpu.roll(x, 1 << b, 1), x)
        bias_ref[h * m_rows:(h + 1) * m_rows, :] = x[:, :BAND_WINDOW] + maskadd_ref[...]


def _band_attend(q_ref, g_ref, bias_ref, out_ref, m_rows, windows):
    n_tiles = q_ref.shape[1] // LANES
    in_head = _head_masks(m_rows)
    kl = lax.broadcasted_iota(jnp.int32, (1, BAND_WINDOW), 1)
    lane_sl = [slice(p * LANES, (p + 1) * LANES) for p in range(n_tiles)]
    row_sl = [slice(j * m_rows, (j + 1) * m_rows) for j in range(len(windows))]
    rows2 = HEADS_PER_TILE * m_rows
    items = [(j, p) for j in range(len(windows)) for p in range(n_tiles)]

    def item_scores(n):
        j, p = items[n]
        return windows[j][0](lane_sl[p], _stack_heads(q_ref[row_sl[j], lane_sl[p]], in_head))

    groups = [list(range(a, min(a + BAND_GROUP, len(items))))
              for a in range(0, len(items), BAND_GROUP)]
    s_next = [item_scores(n) for n in groups[0]]
    for gi, group in enumerate(groups):
        ss = [s_next[i] + bias_ref[items[n][1] * rows2:(items[n][1] + 1) * rows2, :]
              for i, n in enumerate(group)]
        if gi + 1 < len(groups):
            s_next = [item_scores(n) for n in groups[gi + 1]]
        es, invs = [], []
        for i, n in enumerate(group):
            s, key_lo = ss[i], windows[items[n][0]][2]
            if key_lo is not None:
                s = jnp.where(kl >= key_lo, s, NEG_INF)
            e = jnp.exp(s - jnp.max(s, axis=-1, keepdims=True))
            invs.append(1.0 / jnp.sum(e, axis=-1, keepdims=True))
            es.append(e.astype(BF16))
        for i, n in enumerate(group):
            j, p = items[n]
            o2 = windows[j][1](lane_sl[p], es[i]) * invs[i]
            o_pair = _unstack_heads(o2, in_head)
            out_ref[row_sl[j], lane_sl[p]] = (
                o_pair * _silu(g_ref[row_sl[j], lane_sl[p]])).astype(BF16)


def _band_prompt_kernel(q_ref, g_ref, kt_ref, v_ref, tbl_ref, maskadd_ref, out_ref, bias_ref):
    b = pl.program_id(0)
    step = pl.program_id(1)
    n_prev = BAND_PAST // KEY_BLOCK
    n_blk = BAND_WINDOW // KEY_BLOCK

    @pl.when((b == 0) & (step == 0))
    def _():
        _build_band_bias(tbl_ref, maskadd_ref, bias_ref)

    def window(c, at_start):
        first = c - n_prev
        blocks = [max(first + t, 0) if at_start else first + t for t in range(n_blk)]

        def scores(sl, q2):
            kt = jnp.concatenate([kt_ref[blk, sl, :] for blk in blocks], axis=1)
            return _nn_dot(q2, kt)

        def weighted_values(sl, e16):
            if at_start:
                v = jnp.concatenate(
                    [v_ref[blk * KEY_BLOCK:(blk + 1) * KEY_BLOCK, sl] for blk in blocks], axis=0)
            else:
                v = v_ref[pl.ds(pl.multiple_of(first * KEY_BLOCK, KEY_BLOCK), BAND_WINDOW), sl]
            return _nn_dot(e16, v)

        return scores, weighted_values, (n_prev - c) * KEY_BLOCK if at_start else None

    @pl.when(step == 0)
    def _():
        _band_attend(q_ref, g_ref, bias_ref, out_ref, BAND_Q_ROWS,
                     [window(j, True) for j in range(BAND_STEP_BLOCKS)])

    @pl.when(step > 0)
    def _():
        _band_attend(q_ref, g_ref, bias_ref, out_ref, BAND_Q_ROWS,
                     [window(step * BAND_STEP_BLOCKS + j, False)
                      for j in range(BAND_STEP_BLOCKS)])


def _band_static_mask(q_pos, k_pos):
    qc = q_pos // CHUNK
    kc = k_pos // CHUNK
    ok = (kc[None, :] <= qc[:, None]) & (kc[None, :] >= qc[:, None] - N_PREV_CHUNKS)
    return np.where(ok, 0.0, NEG_INF).astype(np.float32)


def _pad_table(rel_bias):
    return jnp.pad(rel_bias, ((0, 0), (0, BIAS_TABLE_PAD - rel_bias.shape[1])))


def _band_prompt(q16, g, kt16, v16, rel_bias):
    bsz, seq, da = q16.shape
    n_heads = da // HEAD_DIM
    step_rows = BAND_STEP_BLOCKS * BAND_Q_ROWS
    assert BAND_Q_ROWS == KEY_BLOCK and seq % step_rows == 0
    assert BAND_STEP_BLOCKS == BAND_PAST // KEY_BLOCK
    q_pos = BAND_PAST + np.arange(BAND_Q_ROWS)
    k_pos = np.arange(BAND_WINDOW)
    maskadd = jnp.asarray(_band_static_mask(q_pos, k_pos))
    blk = lambda b, c: (b, c, 0)
    whole = lambda b, c: (b, 0, 0)
    whole4 = lambda b, c: (b, 0, 0, 0)
    const = lambda b, c: (0, 0)
    return pl.pallas_call(
        _band_prompt_kernel,
        out_shape=jax.ShapeDtypeStruct((bsz, seq, da), BF16),
        grid=(bsz, seq // step_rows),
        in_specs=[pl.BlockSpec((None, step_rows, da), blk),
                  pl.BlockSpec((None, step_rows, da), blk),
                  pl.BlockSpec((None, seq // KEY_BLOCK, da, KEY_BLOCK), whole4),
                  pl.BlockSpec((None, seq, da), whole),
                  pl.BlockSpec((n_heads, BIAS_TABLE_PAD), const),
                  pl.BlockSpec((BAND_Q_ROWS, BAND_WINDOW), const)],
        out_specs=pl.BlockSpec((None, step_rows, da), blk),
        scratch_shapes=[pltpu.VMEM((n_heads * BAND_Q_ROWS, BAND_WINDOW), F32)],
        compiler_params=_params(2),
        name="band_prompt",
    )(q16, g, kt16, v16, _pad_table(rel_bias), maskadd)


def _band_sample_kernel(q_ref, g_ref, ckt_ref, cvt_ref, nk_ref, nv_ref, tbl_ref, maskadd_ref,
                        out_ref, bias_ref):
    b = pl.program_id(0)
    la = ckt_ref.shape[1]

    @pl.when(b == 0)
    def _():
        _build_band_bias(tbl_ref, maskadd_ref, bias_ref)

    def scores(sl, q2):
        s_cache = _nn_dot(q2, ckt_ref[sl, :].astype(BF16))
        s_new = _nt_dot(q2, _pad_rows(nk_ref[:, sl], KEY_BLOCK))
        return jnp.concatenate([s_cache, s_new], axis=1)

    def weighted_values(sl, e16):
        return (_nt_dot(e16[:, :la], cvt_ref[sl, :].astype(BF16))
                + _nn_dot(e16[:, la:], _pad_rows(nv_ref[:, sl], KEY_BLOCK)))

    _band_attend(q_ref, g_ref, bias_ref, out_ref, q_ref.shape[0],
                 [(scores, weighted_values, None)])


def _band_sample(q16, g, cache_kt, cache_vt, layer, nk16, nv16, rel_bias, past):
    bsz, t, da = q16.shape
    la = cache_kt.shape[3]
    n_heads = da // HEAD_DIM
    assert la == BAND_PAST and la + KEY_BLOCK == BAND_WINDOW and t <= KEY_BLOCK and t % 16 == 0
    q_pos = past + np.arange(t)
    k_pos = past - la + np.arange(BAND_WINDOW)
    maskadd = _band_static_mask(q_pos, k_pos)
    maskadd[:, la + t:] = NEG_INF
    maskadd = jnp.asarray(maskadd)
    blk = lambda b: (b, 0, 0)
    const = lambda b: (0, 0)
    return pl.pallas_call(
        _band_sample_kernel,
        out_shape=jax.ShapeDtypeStruct((bsz, t, da), BF16),
        grid=(bsz,),
        in_specs=[pl.BlockSpec((None, t, da), blk),
                  pl.BlockSpec((None, t, da), blk),
                  pl.BlockSpec((None, None, da, la), lambda b: (layer, b, 0, 0)),
                  pl.BlockSpec((None, None, da, la), lambda b: (layer, b, 0, 0)),
                  pl.BlockSpec((None, t, da), blk),
                  pl.BlockSpec((None, t, da), blk),
                  pl.BlockSpec((n_heads, BIAS_TABLE_PAD), const),
                  pl.BlockSpec((t, BAND_WINDOW), const)],
        out_specs=pl.BlockSpec((None, t, da), blk),
        scratch_shapes=[pltpu.VMEM((n_heads * t, BAND_WINDOW), F32)],
        compiler_params=_params(1),
        name="band_sample",
    )(q16, g, cache_kt, cache_vt, nk16, nv16, _pad_table(rel_bias), maskadd)


def _suffix_matrix(width):
    j = np.arange(width)[:, None]
    s = np.arange(width)[None, :]
    tri = (j > s).astype(np.float32)
    return jnp.asarray(np.concatenate([tri, tri], axis=0), dtype=BF16)


def _sb_tiles(q2s, tiles, carry_ref, acc_ref, first, k_is_t, v_is_t):
    n = len(q2s)
    zs = [[(_nn_dot if k_is_t else _nt_dot)(q2s[i], ks[i]) for i in range(n)]
          for ks, _, _, _ in tiles]
    sps, log_betas, split = [], [], []
    for t, (_, _, _, causal) in enumerate(tiles):
        sps.append([])
        log_betas.append([])
        split.append([])
        for z in zs[t]:
            sp = jnp.maximum(z, 0.0) + jnp.log(1.0 + jnp.exp(-jnp.abs(z)))
            log_betas[t].append(z - sp)
            if causal is not None:
                sp = jnp.where(causal, sp, 0.0)
            hi = sp.astype(BF16)
            lo = (sp - hi.astype(F32)).astype(BF16)
            sps[t].append(sp)
            split[t].append(jnp.concatenate([hi, lo], axis=1))
    afters = [[_nn_dot(s, tri) for s in split[t]] for t, (_, _, tri, _) in enumerate(tiles)]
    carries = [None if first else carry_ref[i] for i in range(n)]
    ws = []
    for t, (_, _, _, causal) in enumerate(tiles):
        ws.append([])
        for i in range(n):
            log_w = log_betas[t][i] - afters[t][i]
            total = afters[t][i][:, 0:1] + sps[t][i][:, 0:1]
            if carries[i] is not None:
                log_w = log_w - carries[i]
                total = total + carries[i]
            w = jnp.exp(log_w)
            if causal is not None:
                w = jnp.where(causal, w, 0.0)
            ws[t].append(w.astype(BF16))
            carries[i] = total
    for i in range(n):
        acc = None if first else acc_ref[i]
        for t, (_, vs, _, _) in enumerate(tiles):
            part = (_nt_dot if v_is_t else _nn_dot)(ws[t][i], vs[i])
            acc = part if acc is None else acc + part
        acc_ref[i] = acc
        carry_ref[i] = carries[i]
    return carries


def _sb_setup(q_ref, td):
    tq, width = q_ref.shape
    lane_sl = [slice(p * LANES, (p + 1) * LANES) for p in range(width // LANES)]
    in_head = _head_masks(tq)
    rows2 = HEADS_PER_TILE * tq
    row = lax.broadcasted_iota(jnp.int32, (rows2, td), 0) % tq
    col = lax.broadcasted_iota(jnp.int32, (rows2, td), 1)
    causal = col < row
    q2s = [_stack_heads(q_ref[:, sl], in_head) for sl in lane_sl]
    return lane_sl, in_head, causal, q2s


def _sb_min_carry(carries):
    m = carries[0]
    for c in carries[1:]:
        m = jnp.minimum(m, c)
    return jnp.min(m)


def _sb_store(acc_ref, lane_sl, in_head, g_ref, out_ref):
    for n, sl in enumerate(lane_sl):
        o_pair = _unstack_heads(acc_ref[n], in_head)
        out_ref[:, sl] = (o_pair * _silu(g_ref[:, sl])).astype(BF16)


def _sb_scratch(rows, width):
    n = width // LANES
    return [pltpu.VMEM((n, HEADS_PER_TILE * rows, 1), F32),
            pltpu.VMEM((n, HEADS_PER_TILE * rows, LANES), F32)]


def _sb_prompt_kernel(q_ref, g_ref, kt_ref, v_ref, tri_ref, out_ref, carry_ref, acc_ref):
    tk = tri_ref.shape[1]
    n_full = pl.program_id(2)
    blocks_per_tile = tk // KEY_BLOCK
    tri = tri_ref[...]
    lane_sl, in_head, causal, q2s = _sb_setup(q_ref, tk)
    n = len(lane_sl)

    def load(tile):
        kts = [jnp.concatenate([kt_ref[tile * blocks_per_tile + u, sl, :]
                                for u in range(blocks_per_tile)], axis=1) for sl in lane_sl]
        start = pl.multiple_of(tile * tk, tk)
        return kts, [v_ref[pl.ds(start, tk), sl] for sl in lane_sl]

    @pl.when(n_full == 0)
    def _():
        _sb_tiles(q2s, [load(n_full) + (tri, causal)], carry_ref, acc_ref, True, True, False)

    @pl.when(n_full > 0)
    def _():
        _sb_tiles(q2s, [load(n_full) + (tri, causal), load(n_full - 1) + (tri, None)],
                  carry_ref, acc_ref, True, True, False)

    def cond(loop_state):
        it, smallest = loop_state
        return (it < n_full) & (smallest < SB_UNDERFLOW)

    def body(loop_state):
        it, _ = loop_state
        carries = _sb_tiles(q2s, [load(n_full - 1 - it) + (tri, None)], carry_ref, acc_ref,
                            False, True, False)
        return it + 1, _sb_min_carry(carries)

    lax.while_loop(cond, body,
                   (jnp.int32(1), _sb_min_carry([carry_ref[i] for i in range(n)])))
    _sb_store(acc_ref, lane_sl, in_head, g_ref, out_ref)


def _sb_prompt(q16, g, kt16, v16):
    bsz, seq, db = q16.shape
    t, width = SB_TILE, SB_WIDTH
    assert seq % t == 0 and db % width == 0 and t % KEY_BLOCK == 0
    qblk = lambda b, p, i: (b, i, p)
    whole = lambda b, p, i: (b, 0, p)
    whole4 = lambda b, p, i: (b, 0, p, 0)
    const = lambda b, p, i: (0, 0)
    return pl.pallas_call(
        _sb_prompt_kernel,
        out_shape=jax.ShapeDtypeStruct((bsz, seq, db), BF16),
        grid=(bsz, db // width, seq // t),
        in_specs=[pl.BlockSpec((None, t, width), qblk),
                  pl.BlockSpec((None, t, width), qblk),
                  pl.BlockSpec((None, seq // KEY_BLOCK, width, KEY_BLOCK), whole4),
                  pl.BlockSpec((None, seq, width), whole),
                  pl.BlockSpec((2 * t, t), const)],
        out_specs=pl.BlockSpec((None, t, width), qblk),
        scratch_shapes=_sb_scratch(t, width),
        compiler_params=_params(3),
        name="sb_prompt",
    )(q16, g, kt16, v16, _suffix_matrix(t))


def _sb_sample_kernel(q_ref, g_ref, nk_ref, nv_ref, ckt_ref, cvt_ref, trid_ref, trif_ref,
                      out_ref, carry_ref, acc_ref):
    td = trid_ref.shape[1]
    tk = trif_ref.shape[1]
    n_full = ckt_ref.shape[1] // tk
    trif = trif_ref[...]
    lane_sl, in_head, causal, q2s = _sb_setup(q_ref, td)

    carries = _sb_tiles(q2s, [([_pad_rows(nk_ref[:, sl], td) for sl in lane_sl],
                               [_pad_rows(nv_ref[:, sl], td) for sl in lane_sl],
                               trid_ref[...], causal)], carry_ref, acc_ref, True, False, False)
    smallest = _sb_min_carry(carries)

    for tile in reversed(range(n_full)):
        cols = slice(tile * tk, (tile + 1) * tk)

        def step(cols=cols):
            kts = [ckt_ref[sl, cols].astype(BF16) for sl in lane_sl]
            vts = [cvt_ref[sl, cols].astype(BF16) for sl in lane_sl]
            return _sb_min_carry(_sb_tiles(q2s, [(kts, vts, trif, None)], carry_ref, acc_ref,
                                           False, True, True))

        smallest = lax.cond(smallest < SB_UNDERFLOW, step, lambda s=smallest: s)
    _sb_store(acc_ref, lane_sl, in_head, g_ref, out_ref)


def _sb_sample(q16, g, nk16, nv16, cache_kt, cache_vt, layer):
    bsz, t, db = q16.shape
    past = cache_kt.shape[3]
    width, tk, td = SB_WIDTH, SB_TILE, KEY_BLOCK
    assert t <= td and t % 16 == 0 and past % tk == 0 and db % width == 0
    blk = lambda b, p: (b, 0, p)
    cblk = lambda b, p: (layer, b, p, 0)
    const = lambda b, p: (0, 0)
    return pl.pallas_call(
        _sb_sample_kernel,
        out_shape=jax.ShapeDtypeStruct((bsz, t, db), BF16),
        grid=(bsz, db // width),
        in_specs=[pl.BlockSpec((None, t, width), blk),
                  pl.BlockSpec((None, t, width), blk),
                  pl.BlockSpec((None, t, width), blk),
                  pl.BlockSpec((None, t, width), blk),
                  pl.BlockSpec((None, None, width, past), cblk),
                  pl.BlockSpec((None, None, width, past), cblk),
                  pl.BlockSpec((2 * td, td), const),
                  pl.BlockSpec((2 * tk, tk), const)],
        out_specs=pl.BlockSpec((None, t, width), blk),
        scratch_shapes=_sb_scratch(t, width),
        compiler_params=_params(2),
        name="sb_sample",
    )(q16, g, nk16, nv16, cache_kt, cache_vt, _suffix_matrix(td), _suffix_matrix(tk))


def _row_tile(t):
    return ROW_TILE if t % ROW_TILE == 0 else t


def _cache_transposed(cache):
    l, b, s, h, dh = cache.shape
    return jnp.transpose(cache, (0, 1, 3, 4, 2)).reshape(l, b, h * dh, s)


def _cache_layout(stacked, n_heads):
    l, b, hd, s = stacked.shape
    return jnp.transpose(stacked.reshape(l, b, n_heads, hd // n_heads, s), (0, 1, 4, 2, 3))


def kernel(x_prompt, x_sample, p_prompt, p_sample, cache_a_k, cache_a_v, cache_b_k, cache_b_v,
           g_pre, w_in, rel_bias, w_out, g_post, w_ple, w_ple_gate):
    depth = w_in.shape[0]
    bsz, seq, d = x_prompt.shape
    dbsz, dseq, _ = x_sample.shape
    past = cache_b_k.shape[2]
    keep = min(BAND_PAST, seq)
    da = w_in.shape[2] // 8
    n_heads = da // HEAD_DIM
    tp, ts = bsz * seq, dbsz * dseq

    hp = x_prompt
    hs = x_sample.reshape(ts, d)
    p_prompt = p_prompt.reshape(depth, tp, -1)
    p_sample = p_sample.reshape(depth, ts, -1)
    cache_a_kt, cache_a_vt, cache_b_kt, cache_b_vt = (
        _cache_transposed(c) for c in (cache_a_k, cache_a_v, cache_b_k, cache_b_v))
    stacked = ()
    sak, sav, sbk, sbv = [], [], [], []
    for i in range(depth):
        w16 = w_in[i].astype(BF16)
        grp = lambda n: w16[:, n * da:(n + 1) * da]
        wnn16 = jnp.concatenate([grp(0), grp(2), grp(3), grp(4), grp(6), grp(7)], axis=1)
        wnt16 = jnp.concatenate([grp(1), grp(5), grp(6), grp(2)], axis=1).T
        wo16 = w_out[i].astype(BF16)
        wg16 = w_ple_gate[i].astype(BF16)
        wp16 = w_ple[i].astype(BF16)

        (qa, va16, ga, qb, vb16, gb, kat16, kbt16, *stacked) = _inproj_prompt(
            hp, g_pre[i], wnn16, wnt16, tuple(stacked), keep)
        oa = _band_prompt(qa, ga, kat16, va16, rel_bias[i])
        ob = _sb_prompt(qb, gb, kbt16, vb16)
        hp = _finish(oa.reshape(tp, da), ob.reshape(tp, da), hp.reshape(tp, d),
                     p_prompt, i, wo16, g_post[i], wg16, wp16,
                     _row_tile(tp)).reshape(bsz, seq, d)

        (qa, ka, va, ga, qb, kb, vb, gb, ka16, va16, kb16, vb16) = _inproj(
            hs, g_pre[i], w16, _row_tile(ts))
        s3 = lambda a: a.reshape(dbsz, dseq, da)
        oa = _band_sample(s3(qa), s3(ga), cache_a_kt, cache_a_vt, i, s3(ka16), s3(va16),
                          rel_bias[i], past)
        ob = _sb_sample(s3(qb), s3(gb), s3(kb16), s3(vb16), cache_b_kt, cache_b_vt, i)
        hs = _finish(oa.reshape(ts, da), ob.reshape(ts, da), hs, p_sample, i,
                     wo16, g_post[i], wg16, wp16, _row_tile(ts))
        s4 = lambda a: a.reshape(dbsz, dseq, n_heads, HEAD_DIM)
        sak.append(s4(ka))
        sav.append(s4(va))
        sbk.append(s4(kb))
        sbv.append(s4(vb))

    kb_st, vb_st, ka_st, va_st = stacked
    return (hp, hs.reshape(dbsz, dseq, d),
            _cache_layout(ka_st, n_heads), _cache_layout(va_st, n_heads),
            _cache_layout(kb_st, n_heads), _cache_layout(vb_st, n_heads),
            jnp.stack(sak), jnp.stack(sav), jnp.stack(sbk), jnp.stack(sbv))
```

```python
import functools

import numpy as np
import jax
import jax.numpy as jnp
from jax import lax
from jax.experimental import pallas as pl
from jax.experimental.pallas import tpu as pltpu

HEAD_DIM = 64
CHUNK = 64
N_PREV_CHUNKS = 8
BAND_PAST = N_PREV_CHUNKS * CHUNK
REL_CLIP = 128
RMS_EPS = 1e-6
NEG_INF = -1e30
ATTN_SCALE = HEAD_DIM ** -0.5
NEG_LOG2E = -1.4426950408889634

LANES = 128
HEADS_PER_TILE = LANES // HEAD_DIM
VMEM_LIMIT_BYTES = 56 * 1024 * 1024

KEY_BLOCK = LANES
BAND_Q_ROWS = 2 * CHUNK
BAND_STEP_BLOCKS = 4
BAND_GROUP = 4
BAND_WINDOW = BAND_PAST + BAND_Q_ROWS
BIAS_TABLE_PAD = 384
BIAS_EXT = 768

SB_TILE = 256
SB_WIDTH = 512
SB_UNDERFLOW = 105.0

INPROJ_ROWS = 512
ROW_TILE = 1024

BF16 = jnp.bfloat16
F32 = jnp.float32


def _params(n_axes):
    return pltpu.CompilerParams(
        dimension_semantics=("arbitrary",) * n_axes,
        vmem_limit_bytes=VMEM_LIMIT_BYTES)


def _nt_dot(a, b):
    return lax.dot_general(a, b, (((1,), (1,)), ((), ())), preferred_element_type=F32)


def _nn_dot(a, b):
    return jnp.dot(a, b, preferred_element_type=F32)


def _silu(g):
    return g * (1.0 / (1.0 + jnp.exp(-g)))


def _rms_scale(x, gain):
    ms = jnp.mean(x * x, axis=-1, keepdims=True)
    return x * lax.rsqrt(ms + RMS_EPS) * gain


def _head_masks(m_rows):
    lane = lax.broadcasted_iota(jnp.int32, (m_rows, LANES), 1)
    return [(lane >= hh * HEAD_DIM) & (lane < (hh + 1) * HEAD_DIM)
            for hh in range(HEADS_PER_TILE)]


def _stack_heads(q_pair, in_head):
    zero = jnp.zeros_like(q_pair)
    return jnp.concatenate([jnp.where(m, q_pair, zero) for m in in_head], axis=0)


def _unstack_heads(o2, in_head):
    m_rows = o2.shape[0] // HEADS_PER_TILE
    o = o2[0:m_rows]
    for hh in range(1, HEADS_PER_TILE):
        o = jnp.where(in_head[hh], o2[hh * m_rows:(hh + 1) * m_rows], o)
    return o


def _pad_rows(a, rows):
    return jnp.concatenate([a, jnp.zeros((rows - a.shape[0], a.shape[1]), a.dtype)], axis=0)


def _inproj_kernel(x_ref, g_ref, w_ref,
                   qa_ref, ka_ref, va_ref, ga_ref, qb_ref, kb_ref, vb_ref, gb_ref,
                   ka16_ref, va16_ref, kb16_ref, vb16_ref):
    hn = _rms_scale(x_ref[...], g_ref[...]).astype(BF16)
    gw = qa_ref.shape[-1]

    def proj(n):
        return _nn_dot(hn, w_ref[:, n * gw:(n + 1) * gw])

    qa_ref[...] = (proj(0) * ATTN_SCALE).astype(BF16)
    u = proj(1)
    ka_ref[...] = u
    ka16_ref[...] = u.astype(BF16)
    u = proj(2)
    va_ref[...] = u
    va16_ref[...] = u.astype(BF16)
    ga_ref[...] = proj(3)
    qb_ref[...] = (proj(4) * ATTN_SCALE).astype(BF16)
    u = proj(5)
    kb_ref[...] = u
    kb16_ref[...] = u.astype(BF16)
    u = proj(6)
    vb_ref[...] = u
    vb16_ref[...] = u.astype(BF16)
    gb_ref[...] = proj(7)


def _inproj(x2d, g_pre, w16, tm):
    t, d = x2d.shape
    d_in = w16.shape[1]
    gw = d_in // 8
    row = lambda i: (i, 0)
    const = lambda i: (0, 0)
    f32_out = jax.ShapeDtypeStruct((t, gw), F32)
    b16_out = jax.ShapeDtypeStruct((t, gw), BF16)
    out_shape = (b16_out, f32_out, f32_out, f32_out, b16_out, f32_out, f32_out, f32_out,
                 b16_out, b16_out, b16_out, b16_out)
    return pl.pallas_call(
        _inproj_kernel,
        out_shape=out_shape,
        grid=(t // tm,),
        in_specs=[pl.BlockSpec((tm, d), row),
                  pl.BlockSpec((1, d), const),
                  pl.BlockSpec((d, d_in), const)],
        out_specs=tuple(pl.BlockSpec((tm, gw), row) for _ in out_shape),
        compiler_params=_params(1),
        name="inproj_sample",
    )(x2d, g_pre.reshape(1, d), w16)


def _inproj_prompt_kernel(*refs, n_prev, keep_first):
    x_ref, g_ref, wnn_ref, wnt_ref = refs[:4]
    prev = refs[4:4 + (4 if n_prev else 0)]
    (qa_ref, va16_ref, ga_ref, qb_ref, vb16_ref, gb_ref, kat16_ref, kbt16_ref,
     kb_st_ref, vb_st_ref, ka_st_ref, va_st_ref) = refs[4 + len(prev):]
    j = pl.program_id(1)
    gw = qa_ref.shape[-1]
    hn = _rms_scale(x_ref[...], g_ref[...]).astype(BF16)

    def proj(n):
        return _nn_dot(hn, wnn_ref[:, n * gw:(n + 1) * gw])

    qa_ref[...] = (proj(0) * ATTN_SCALE).astype(BF16)
    va16_ref[...] = proj(1).astype(BF16)
    ga_ref[...] = proj(2)
    qb_ref[...] = (proj(3) * ATTN_SCALE).astype(BF16)
    vb16_ref[...] = proj(4).astype(BF16)
    gb_ref[...] = proj(5)

    nt = _nt_dot(wnt_ref[0:3 * gw, :], hn)
    kat, kbt, vbt = nt[0:gw], nt[gw:2 * gw], nt[2 * gw:3 * gw]
    for u in range(kat16_ref.shape[0]):
        blk = slice(u * KEY_BLOCK, (u + 1) * KEY_BLOCK)
        kat16_ref[u] = kat[:, blk].astype(BF16)
        kbt16_ref[u] = kbt[:, blk].astype(BF16)
    for dpt in range(n_prev):
        kb_st_ref[dpt] = prev[0][dpt]
        vb_st_ref[dpt] = prev[1][dpt]
    kb_st_ref[n_prev] = kbt
    vb_st_ref[n_prev] = vbt

    @pl.when(j >= keep_first)
    def _():
        for dpt in range(n_prev):
            ka_st_ref[dpt] = prev[2][dpt]
            va_st_ref[dpt] = prev[3][dpt]
        ka_st_ref[n_prev] = kat
        va_st_ref[n_prev] = _nt_dot(wnt_ref[3 * gw:4 * gw, :], hn)


def _inproj_prompt(x3d, g_pre, wnn16, wnt16, prev, keep):
    bsz, seq, d = x3d.shape
    tm = INPROJ_ROWS
    gw = wnt16.shape[0] // 4
    n_prev = prev[0].shape[0] if prev else 0
    assert seq % tm == 0 and keep % tm == 0 and tm % KEY_BLOCK == 0
    keep_first = (seq - keep) // tm
    tok = lambda b, j: (b, j, 0)
    const = lambda b, j: (0, 0)
    ktb = lambda b, j: (b, j, 0, 0)
    st_all = lambda b, j: (0, b, 0, j)
    st_keep = lambda b, j: (0, b, 0, jnp.maximum(j - keep_first, 0))
    tok_b16 = jax.ShapeDtypeStruct((bsz, seq, gw), BF16)
    tok_f32 = jax.ShapeDtypeStruct((bsz, seq, gw), F32)
    kt_b16 = jax.ShapeDtypeStruct((bsz, seq // KEY_BLOCK, gw, KEY_BLOCK), BF16)
    st_f32 = jax.ShapeDtypeStruct((n_prev + 1, bsz, gw, seq), F32)
    keep_f32 = jax.ShapeDtypeStruct((n_prev + 1, bsz, gw, keep), F32)
    out_shape = (tok_b16, tok_b16, tok_f32, tok_b16, tok_b16, tok_f32, kt_b16, kt_b16,
                 st_f32, st_f32, keep_f32, keep_f32)
    tok_spec = pl.BlockSpec((None, tm, gw), tok)
    kt_spec = pl.BlockSpec((None, tm // KEY_BLOCK, gw, KEY_BLOCK), ktb)
    out_specs = (tok_spec,) * 6 + (kt_spec,) * 2 + (
        pl.BlockSpec((n_prev + 1, None, gw, tm), st_all),
        pl.BlockSpec((n_prev + 1, None, gw, tm), st_all),
        pl.BlockSpec((n_prev + 1, None, gw, tm), st_keep),
        pl.BlockSpec((n_prev + 1, None, gw, tm), st_keep))
    in_specs = [pl.BlockSpec((None, tm, d), tok),
                pl.BlockSpec((1, d), const),
                pl.BlockSpec(wnn16.shape, const, pipeline_mode=pl.Buffered(1)),
                pl.BlockSpec(wnt16.shape, const, pipeline_mode=pl.Buffered(1))]
    if n_prev:
        in_specs += [pl.BlockSpec((n_prev, None, gw, tm), st_all),
                     pl.BlockSpec((n_prev, None, gw, tm), st_all),
                     pl.BlockSpec((n_prev, None, gw, tm), st_keep),
                     pl.BlockSpec((n_prev, None, gw, tm), st_keep)]
    return pl.pallas_call(
        functools.partial(_inproj_prompt_kernel, n_prev=n_prev, keep_first=keep_first),
        out_shape=out_shape,
        grid=(bsz, seq // tm),
        in_specs=in_specs,
        out_specs=out_specs,
        compiler_params=_params(2),
        name="inproj_prompt",
    )(x3d, g_pre.reshape(1, d), wnn16, wnt16, *prev)


def _finish_kernel(oa_ref, ob_ref, h_ref, p_ref, woa_ref, wob_ref, gpost_ref, wg_ref, wp_ref,
                   out_ref):
    y = _nn_dot(oa_ref[...], woa_ref[...]) + _nn_dot(ob_ref[...], wob_ref[...])
    h1 = h_ref[...] + _rms_scale(y, gpost_ref[...])
    gate_logit = _nn_dot(h1.astype(BF16), wg_ref[...])
    gate = 1.0 / (1.0 + jnp.exp(-gate_logit))
    emb = _nn_dot(p_ref[...].astype(BF16), wp_ref[...])
    out_ref[...] = h1 + gate * emb


def _finish(oa, ob, h2d, p_all, layer, wo16, g_post, wg16, wp16, tm):
    t, d = h2d.shape
    da = oa.shape[1]
    dp = p_all.shape[2]
    row = lambda i: (i, 0)
    const = lambda i: (0, 0)
    return pl.pallas_call(
        _finish_kernel,
        out_shape=jax.ShapeDtypeStruct((t, d), F32),
        grid=(t // tm,),
        in_specs=[pl.BlockSpec((tm, da), row),
                  pl.BlockSpec((tm, da), row),
                  pl.BlockSpec((tm, d), row),
                  pl.BlockSpec((None, tm, dp), lambda i: (layer, i, 0)),
                  pl.BlockSpec((da, d), const),
                  pl.BlockSpec((da, d), lambda i: (1, 0)),
                  pl.BlockSpec((1, d), const),
                  pl.BlockSpec((d, d), const),
                  pl.BlockSpec((dp, d), const)],
        out_specs=pl.BlockSpec((tm, d), row),
        compiler_params=_params(1),
        name="finish",
    )(oa, ob, h2d, p_all, wo16, wo16, g_post.reshape(1, d), wg16, wp16)


def _build_band_bias(tbl_ref, maskadd_ref, bias_ref):
    m_rows = maskadd_ref.shape[0]
    n_heads = bias_ref.shape[0] // m_rows
    tbl = tbl_ref[...]
    p1 = tbl.astype(BF16)
    r1 = tbl - p1.astype(F32)
    p2 = r1.astype(BF16)
    p3 = (r1 - p2.astype(F32)).astype(BF16)
    r_idx = lax.broadcasted_iota(jnp.int32, (BIAS_TABLE_PAD, BIAS_EXT), 0)
    m_idx = lax.broadcasted_iota(jnp.int32, (BIAS_TABLE_PAD, BIAS_EXT), 1)
    rel = jnp.clip(m_idx - (m_rows - 1) - BAND_PAST, -REL_CLIP, REL_CLIP) + REL_CLIP
    onehot = (r_idx == rel).astype(BF16)
    t_ext = _nn_dot(p1, onehot) + _nn_dot(p2, onehot) + _nn_dot(p3, onehot)
    ql = lax.broadcasted_iota(jnp.int32, (m_rows, BIAS_EXT), 0)
    n_bits = int(m_rows - 1).bit_length()
    for h in range(n_heads):
        x = jnp.broadcast_to(t_ext[h:h + 1, :], (m_rows, BIAS_EXT))
        x = pltpu.roll(x, BIAS_EXT - (m_rows - 1), 1)
        for b in range(n_bits):
            x = jnp.where(((ql >> b) & 1) == 1, pltpu.roll(x, 1 << b, 1), x)
        bias_ref[h * m_rows:(h + 1) * m_rows, :] = x[:, :BAND_WINDOW] + maskadd_ref[...]


def _band_attend(q_ref, g_ref, bias_ref, out_ref, m_rows, windows):
    n_tiles = q_ref.shape[1] // LANES
    in_head = _head_masks(m_rows)
    kl = lax.broadcasted_iota(jnp.int32, (1, BAND_WINDOW), 1)
    lane_sl = [slice(p * LANES, (p + 1) * LANES) for p in range(n_tiles)]
    row_sl = [slice(j * m_rows, (j + 1) * m_rows) for j in range(len(windows))]
    rows2 = HEADS_PER_TILE * m_rows
    items = [(j, p) for j in range(len(windows)) for p in range(n_tiles)]

    def item_scores(n):
        j, p = items[n]
        return windows[j][0](lane_sl[p], _stack_heads(q_ref[row_sl[j], lane_sl[p]], in_head))

    groups = [list(range(a, min(a + BAND_GROUP, len(items))))
              for a in range(0, len(items), BAND_GROUP)]
    s_next = [item_scores(n) for n in groups[0]]
    for gi, group in enumerate(groups):
        ss = [s_next[i] + bias_ref[items[n][1] * rows2:(items[n][1] + 1) * rows2, :]
              for i, n in enumerate(group)]
        if gi + 1 < len(groups):
            s_next = [item_scores(n) for n in groups[gi + 1]]
        es = []
        for i, n in enumerate(group):
            s, key_lo = ss[i], windows[items[n][0]][2]
            if key_lo is not None:
                s = jnp.where(kl >= key_lo, s, NEG_INF)
            es.append(jnp.exp(s - jnp.max(s, axis=-1, keepdims=True)).astype(BF16))
        for i, n in enumerate(group):
            j, p = items[n]
            o_ext = windows[j][1](lane_sl[p], es[i])
            o2 = o_ext[:, :LANES] * (1.0 / o_ext[:, LANES:])
            o_pair = _unstack_heads(o2, in_head)
            out_ref[row_sl[j], lane_sl[p]] = (
                o_pair * _silu(g_ref[row_sl[j], lane_sl[p]])).astype(BF16)


def _band_prompt_kernel(q_ref, g_ref, kt_ref, v_ref, tbl_ref, maskadd_ref, out_ref, bias_ref):
    b = pl.program_id(0)
    step = pl.program_id(1)
    n_prev = BAND_PAST // KEY_BLOCK
    n_blk = BAND_WINDOW // KEY_BLOCK

    @pl.when((b == 0) & (step == 0))
    def _():
        _build_band_bias(tbl_ref, maskadd_ref, bias_ref)

    def window(c, at_start):
        first = c - n_prev
        blocks = [max(first + t, 0) if at_start else first + t for t in range(n_blk)]

        def scores(sl, q2):
            kt = jnp.concatenate([kt_ref[blk, sl, :] for blk in blocks], axis=1)
            return _nn_dot(q2, kt)

        def weighted_values(sl, e16):
            if at_start:
                v = jnp.concatenate(
                    [v_ref[blk * KEY_BLOCK:(blk + 1) * KEY_BLOCK, sl] for blk in blocks], axis=0)
            else:
                v = v_ref[pl.ds(pl.multiple_of(first * KEY_BLOCK, KEY_BLOCK), BAND_WINDOW), sl]
            return _nn_dot(e16, jnp.concatenate([v, jnp.ones_like(v)], axis=1))

        return scores, weighted_values, (n_prev - c) * KEY_BLOCK if at_start else None

    @pl.when(step == 0)
    def _():
        _band_attend(q_ref, g_ref, bias_ref, out_ref, BAND_Q_ROWS,
                     [window(j, True) for j in range(BAND_STEP_BLOCKS)])

    @pl.when(step > 0)
    def _():
        _band_attend(q_ref, g_ref, bias_ref, out_ref, BAND_Q_ROWS,
                     [window(step * BAND_STEP_BLOCKS + j, False)
                      for j in range(BAND_STEP_BLOCKS)])


def _band_static_mask(q_pos, k_pos):
    qc = q_pos // CHUNK
    kc = k_pos // CHUNK
    ok = (kc[None, :] <= qc[:, None]) & (kc[None, :] >= qc[:, None] - N_PREV_CHUNKS)
    return np.where(ok, 0.0, NEG_INF).astype(np.float32)


def _pad_table(rel_bias):
    return jnp.pad(rel_bias, ((0, 0), (0, BIAS_TABLE_PAD - rel_bias.shape[1])))


def _band_prompt(q16, g, kt16, v16, rel_bias):
    bsz, seq, da = q16.shape
    n_heads = da // HEAD_DIM
    step_rows = BAND_STEP_BLOCKS * BAND_Q_ROWS
    assert BAND_Q_ROWS == KEY_BLOCK and seq % step_rows == 0
    assert BAND_STEP_BLOCKS == BAND_PAST // KEY_BLOCK
    q_pos = BAND_PAST + np.arange(BAND_Q_ROWS)
    k_pos = np.arange(BAND_WINDOW)
    maskadd = jnp.asarray(_band_static_mask(q_pos, k_pos))
    blk = lambda b, c: (b, c, 0)
    whole = lambda b, c: (b, 0, 0)
    whole4 = lambda b, c: (b, 0, 0, 0)
    const = lambda b, c: (0, 0)
    return pl.pallas_call(
        _band_prompt_kernel,
        out_shape=jax.ShapeDtypeStruct((bsz, seq, da), BF16),
        grid=(bsz, seq // step_rows),
        in_specs=[pl.BlockSpec((None, step_rows, da), blk),
                  pl.BlockSpec((None, step_rows, da), blk),
                  pl.BlockSpec((None, seq // KEY_BLOCK, da, KEY_BLOCK), whole4),
                  pl.BlockSpec((None, seq, da), whole),
                  pl.BlockSpec((n_heads, BIAS_TABLE_PAD), const),
                  pl.BlockSpec((BAND_Q_ROWS, BAND_WINDOW), const)],
        out_specs=pl.BlockSpec((None, step_rows, da), blk),
        scratch_shapes=[pltpu.VMEM((n_heads * BAND_Q_ROWS, BAND_WINDOW), F32)],
        compiler_params=_params(2),
        name="band_prompt",
    )(q16, g, kt16, v16, _pad_table(rel_bias), maskadd)


def _band_sample_kernel(q_ref, g_ref, ckt_ref, cvt_ref, nk_ref, nv_ref, tbl_ref, maskadd_ref,
                        out_ref, bias_ref):
    b = pl.program_id(0)
    la = ckt_ref.shape[1]

    @pl.when(b == 0)
    def _():
        _build_band_bias(tbl_ref, maskadd_ref, bias_ref)

    def scores(sl, q2):
        s_cache = _nn_dot(q2, ckt_ref[sl, :].astype(BF16))
        s_new = _nt_dot(q2, _pad_rows(nk_ref[:, sl], KEY_BLOCK))
        return jnp.concatenate([s_cache, s_new], axis=1)

    def weighted_values(sl, e16):
        cvt = cvt_ref[sl, :].astype(BF16)
        nv = _pad_rows(nv_ref[:, sl], KEY_BLOCK)
        return (_nt_dot(e16[:, :la], jnp.concatenate([cvt, jnp.ones_like(cvt)], axis=0))
                + _nn_dot(e16[:, la:], jnp.concatenate([nv, jnp.ones_like(nv)], axis=1)))

    _band_attend(q_ref, g_ref, bias_ref, out_ref, q_ref.shape[0],
                 [(scores, weighted_values, None)])


def _band_sample(q16, g, cache_kt, cache_vt, layer, nk16, nv16, rel_bias, past):
    bsz, t, da = q16.shape
    la = cache_kt.shape[3]
    n_heads = da // HEAD_DIM
    assert la == BAND_PAST and la + KEY_BLOCK == BAND_WINDOW and t <= KEY_BLOCK and t % 16 == 0
    q_pos = past + np.arange(t)
    k_pos = past - la + np.arange(BAND_WINDOW)
    maskadd = _band_static_mask(q_pos, k_pos)
    maskadd[:, la + t:] = NEG_INF
    maskadd = jnp.asarray(maskadd)
    blk = lambda b: (b, 0, 0)
    const = lambda b: (0, 0)
    return pl.pallas_call(
        _band_sample_kernel,
        out_shape=jax.ShapeDtypeStruct((bsz, t, da), BF16),
        grid=(bsz,),
        in_specs=[pl.BlockSpec((None, t, da), blk),
                  pl.BlockSpec((None, t, da), blk),
                  pl.BlockSpec((None, None, da, la), lambda b: (layer, b, 0, 0)),
                  pl.BlockSpec((None, None, da, la), lambda b: (layer, b, 0, 0)),
                  pl.BlockSpec((None, t, da), blk),
                  pl.BlockSpec((None, t, da), blk),
                  pl.BlockSpec((n_heads, BIAS_TABLE_PAD), const),
                  pl.BlockSpec((t, BAND_WINDOW), const)],
        out_specs=pl.BlockSpec((None, t, da), blk),
        scratch_shapes=[pltpu.VMEM((n_heads * t, BAND_WINDOW), F32)],
        compiler_params=_params(1),
        name="band_sample",
    )(q16, g, cache_kt, cache_vt, nk16, nv16, _pad_table(rel_bias), maskadd)


def _suffix_matrix(width):
    j = np.arange(width)[:, None]
    s = np.arange(width)[None, :]
    tri = (j >= s).astype(np.float32)
    return jnp.asarray(np.concatenate([tri, tri], axis=0), dtype=BF16)


def _sb_tiles(q2s, tiles, carry_ref, acc_ref, first, k_is_t, v_is_t):
    n = len(q2s)
    zs = [[(_nn_dot if k_is_t else _nt_dot)(q2s[i], ks[i]) for i in range(n)]
          for ks, _, _, _ in tiles]
    split = []
    for t, (_, _, _, causal) in enumerate(tiles):
        split.append([])
        for z in zs[t]:
            sp = jnp.maximum(z, 0.0) + jnp.log(1.0 + jnp.exp2(jnp.abs(z) * NEG_LOG2E))
            if causal is not None:
                sp = jnp.where(causal, sp, 0.0)
            hi = sp.astype(BF16)
            lo = (sp - hi.astype(F32)).astype(BF16)
            split[t].append(jnp.concatenate([hi, lo], axis=1))
    sums = [[_nn_dot(s, tri) for s in split[t]] for t, (_, _, tri, _) in enumerate(tiles)]
    carries = [None if first else carry_ref[i] for i in range(n)]
    ws = []
    for t, (_, _, _, causal) in enumerate(tiles):
        ws.append([])
        for i in range(n):
            log_w = zs[t][i] - sums[t][i]
            total = sums[t][i][:, 0:1]
            if carries[i] is not None:
                log_w = log_w - carries[i]
                total = total + carries[i]
            w = jnp.exp(log_w)
            if causal is not None:
                w = jnp.where(causal, w, 0.0)
            ws[t].append(w.astype(BF16))
            carries[i] = total
    for i in range(n):
        acc = None if first else acc_ref[i]
        for t, (_, vs, _, _) in enumerate(tiles):
            part = (_nt_dot if v_is_t else _nn_dot)(ws[t][i], vs[i])
            acc = part if acc is None else acc + part
        acc_ref[i] = acc
        carry_ref[i] = carries[i]
    return carries


def _sb_setup(q_ref, td):
    tq, width = q_ref.shape
    lane_sl = [slice(p * LANES, (p + 1) * LANES) for p in range(width // LANES)]
    in_head = _head_masks(tq)
    rows2 = HEADS_PER_TILE * tq
    row = lax.broadcasted_iota(jnp.int32, (rows2, td), 0) % tq
    col = lax.broadcasted_iota(jnp.int32, (rows2, td), 1)
    causal = col < row
    q2s = [_stack_heads(q_ref[:, sl], in_head) for sl in lane_sl]
    return lane_sl, in_head, causal, q2s


def _sb_min_carry(carries):
    m = carries[0]
    for c in carries[1:]:
        m = jnp.minimum(m, c)
    return jnp.min(m)


def _sb_store(acc_ref, lane_sl, in_head, g_ref, out_ref):
    for n, sl in enumerate(lane_sl):
        o_pair = _unstack_heads(acc_ref[n], in_head)
        out_ref[:, sl] = (o_pair * _silu(g_ref[:, sl])).astype(BF16)


def _sb_scratch(rows, width):
    n = width // LANES
    return [pltpu.VMEM((n, HEADS_PER_TILE * rows, 1), F32),
            pltpu.VMEM((n, HEADS_PER_TILE * rows, LANES), F32)]


def _sb_prompt_kernel(q_ref, g_ref, kt_ref, v_ref, tri_ref, out_ref, carry_ref, acc_ref):
    tk = tri_ref.shape[1]
    n_full = pl.program_id(2)
    blocks_per_tile = tk // KEY_BLOCK
    tri = tri_ref[...]
    lane_sl, in_head, causal, q2s = _sb_setup(q_ref, tk)
    n = len(lane_sl)

    def load(tile):
        kts = [jnp.concatenate([kt_ref[tile * blocks_per_tile + u, sl, :]
                                for u in range(blocks_per_tile)], axis=1) for sl in lane_sl]
        start = pl.multiple_of(tile * tk, tk)
        return kts, [v_ref[pl.ds(start, tk), sl] for sl in lane_sl]

    @pl.when(n_full == 0)
    def _():
        _sb_tiles(q2s, [load(n_full) + (tri, causal)], carry_ref, acc_ref, True, True, False)

    @pl.when(n_full > 0)
    def _():
        _sb_tiles(q2s, [load(n_full) + (tri, causal), load(n_full - 1) + (tri, None)],
                  carry_ref, acc_ref, True, True, False)

    def cond(loop_state):
        it, smallest = loop_state
        return (it < n_full) & (smallest < SB_UNDERFLOW)

    def body(loop_state):
        it, _ = loop_state
        carries = _sb_tiles(q2s, [load(n_full - 1 - it) + (tri, None)], carry_ref, acc_ref,
                            False, True, False)
        return it + 1, _sb_min_carry(carries)

    lax.while_loop(cond, body,
                   (jnp.int32(1), _sb_min_carry([carry_ref[i] for i in range(n)])))
    _sb_store(acc_ref, lane_sl, in_head, g_ref, out_ref)


def _sb_prompt(q16, g, kt16, v16):
    bsz, seq, db = q16.shape
    t, width = SB_TILE, SB_WIDTH
    assert seq % t == 0 and db % width == 0 and t % KEY_BLOCK == 0
    qblk = lambda b, p, i: (b, i, p)
    whole = lambda b, p, i: (b, 0, p)
    whole4 = lambda b, p, i: (b, 0, p, 0)
    const = lambda b, p, i: (0, 0)
    return pl.pallas_call(
        _sb_prompt_kernel,
        out_shape=jax.ShapeDtypeStruct((bsz, seq, db), BF16),
        grid=(bsz, db // width, seq // t),
        in_specs=[pl.BlockSpec((None, t, width), qblk),
                  pl.BlockSpec((None, t, width), qblk),
                  pl.BlockSpec((None, seq // KEY_BLOCK, width, KEY_BLOCK), whole4),
                  pl.BlockSpec((None, seq, width), whole),
                  pl.BlockSpec((2 * t, t), const)],
        out_specs=pl.BlockSpec((None, t, width), qblk),
        scratch_shapes=_sb_scratch(t, width),
        compiler_params=_params(3),
        name="sb_prompt",
    )(q16, g, kt16, v16, _suffix_matrix(t))


def _sb_sample_kernel(q_ref, g_ref, nk_ref, nv_ref, ckt_ref, cvt_ref, trid_ref, trif_ref,
                      out_ref, carry_ref, acc_ref):
    td = trid_ref.shape[1]
    tk = trif_ref.shape[1]
    n_full = ckt_ref.shape[1] // tk
    trif = trif_ref[...]
    lane_sl, in_head, causal, q2s = _sb_setup(q_ref, td)

    carries = _sb_tiles(q2s, [([_pad_rows(nk_ref[:, sl], td) for sl in lane_sl],
                               [_pad_rows(nv_ref[:, sl], td) for sl in lane_sl],
                               trid_ref[...], causal)], carry_ref, acc_ref, True, False, False)
    smallest = _sb_min_carry(carries)

    for tile in reversed(range(n_full)):
        cols = slice(tile * tk, (tile + 1) * tk)

        def step(cols=cols):
            kts = [ckt_ref[sl, cols].astype(BF16) for sl in lane_sl]
            vts = [cvt_ref[sl, cols].astype(BF16) for sl in lane_sl]
            return _sb_min_carry(_sb_tiles(q2s, [(kts, vts, trif, None)], carry_ref, acc_ref,
                                           False, True, True))

        smallest = lax.cond(smallest < SB_UNDERFLOW, step, lambda s=smallest: s)
    _sb_store(acc_ref, lane_sl, in_head, g_ref, out_ref)


def _sb_sample(q16, g, nk16, nv16, cache_kt, cache_vt, layer):
    bsz, t, db = q16.shape
    past = cache_kt.shape[3]
    width, tk, td = SB_WIDTH, SB_TILE, KEY_BLOCK
    assert t <= td and t % 16 == 0 and past % tk == 0 and db % width == 0
    blk = lambda b, p: (b, 0, p)
    cblk = lambda b, p: (layer, b, p, 0)
    const = lambda b, p: (0, 0)
    return pl.pallas_call(
        _sb_sample_kernel,
        out_shape=jax.ShapeDtypeStruct((bsz, t, db), BF16),
        grid=(bsz, db // width),
        in_specs=[pl.BlockSpec((None, t, width), blk),
                  pl.BlockSpec((None, t, width), blk),
                  pl.BlockSpec((None, t, width), blk),
                  pl.BlockSpec((None, t, width), blk),
                  pl.BlockSpec((None, None, width, past), cblk),
                  pl.BlockSpec((None, None, width, past), cblk),
                  pl.BlockSpec((2 * td, td), const),
                  pl.BlockSpec((2 * tk, tk), const)],
        out_specs=pl.BlockSpec((None, t, width), blk),
        scratch_shapes=_sb_scratch(t, width),
        compiler_params=_params(2),
        name="sb_sample",
    )(q16, g, nk16, nv16, cache_kt, cache_vt, _suffix_matrix(td), _suffix_matrix(tk))


def _row_tile(t):
    return ROW_TILE if t % ROW_TILE == 0 else t


def _cache_transposed(cache):
    l, b, s, h, dh = cache.shape
    return jnp.transpose(cache, (0, 1, 3, 4, 2)).reshape(l, b, h * dh, s)


def _cache_layout(stacked, n_heads):
    l, b, hd, s = stacked.shape
    return jnp.transpose(stacked.reshape(l, b, n_heads, hd // n_heads, s), (0, 1, 4, 2, 3))


def kernel(x_prompt, x_sample, p_prompt, p_sample, cache_a_k, cache_a_v, cache_b_k, cache_b_v,
           g_pre, w_in, rel_bias, w_out, g_post, w_ple, w_ple_gate):
    depth = w_in.shape[0]
    bsz, seq, d = x_prompt.shape
    dbsz, dseq, _ = x_sample.shape
    past = cache_b_k.shape[2]
    keep = min(BAND_PAST, seq)
    da = w_in.shape[2] // 8
    n_heads = da // HEAD_DIM
    tp, ts = bsz * seq, dbsz * dseq

    hp = x_prompt
    hs = x_sample.reshape(ts, d)
    p_prompt = p_prompt.reshape(depth, tp, -1)
    p_sample = p_sample.reshape(depth, ts, -1)
    cache_a_kt, cache_a_vt, cache_b_kt, cache_b_vt = (
        _cache_transposed(c) for c in (cache_a_k, cache_a_v, cache_b_k, cache_b_v))
    stacked = ()
    sak, sav, sbk, sbv = [], [], [], []
    for i in range(depth):
        w16 = w_in[i].astype(BF16)
        grp = lambda n: w16[:, n * da:(n + 1) * da]
        wnn16 = jnp.concatenate([grp(0), grp(2), grp(3), grp(4), grp(6), grp(7)], axis=1)
        wnt16 = jnp.concatenate([grp(1), grp(5), grp(6), grp(2)], axis=1).T
        wo16 = w_out[i].astype(BF16)
        wg16 = w_ple_gate[i].astype(BF16)
        wp16 = w_ple[i].astype(BF16)

        (qa, va16, ga, qb, vb16, gb, kat16, kbt16, *stacked) = _inproj_prompt(
            hp, g_pre[i], wnn16, wnt16, tuple(stacked), keep)
        oa = _band_prompt(qa, ga, kat16, va16, rel_bias[i])
        ob = _sb_prompt(qb, gb, kbt16, vb16)
        hp = _finish(oa.reshape(tp, da), ob.reshape(tp, da), hp.reshape(tp, d),
                     p_prompt, i, wo16, g_post[i], wg16, wp16,
                     _row_tile(tp)).reshape(bsz, seq, d)

        (qa, ka, va, ga, qb, kb, vb, gb, ka16, va16, kb16, vb16) = _inproj(
            hs, g_pre[i], w16, _row_tile(ts))
        s3 = lambda a: a.reshape(dbsz, dseq, da)
        oa = _band_sample(s3(qa), s3(ga), cache_a_kt, cache_a_vt, i, s3(ka16), s3(va16),
                          rel_bias[i], past)
        ob = _sb_sample(s3(qb), s3(gb), s3(kb16), s3(vb16), cache_b_kt, cache_b_vt, i)
        hs = _finish(oa.reshape(ts, da), ob.reshape(ts, da), hs, p_sample, i,
                     wo16, g_post[i], wg16, wp16, _row_tile(ts))
        s4 = lambda a: a.reshape(dbsz, dseq, n_heads, HEAD_DIM)
        sak.append(s4(ka))
        sav.append(s4(va))
        sbk.append(s4(kb))
        sbv.append(s4(vb))

    kb_st, vb_st, ka_st, va_st = stacked
    return (hp, hs.reshape(dbsz, dseq, d),
            _cache_layout(ka_st, n_heads), _cache_layout(va_st, n_heads),
            _cache_layout(kb_st, n_heads), _cache_layout(vb_st, n_heads),
            jnp.stack(sak), jnp.stack(sav), jnp.stack(sbk), jnp.stack(sbv))
```

```python
import functools

import numpy as np
import jax
import jax.numpy as jnp
from jax import lax
from jax.experimental import pallas as pl
from jax.experimental.pallas import tpu as pltpu

HEAD_DIM = 64
CHUNK = 64
N_PREV_CHUNKS = 8
BAND_PAST = N_PREV_CHUNKS * CHUNK
REL_CLIP = 128
RMS_EPS = 1e-6
NEG_INF = -1e30
ATTN_SCALE = HEAD_DIM ** -0.5
NEG_LOG2E = -1.4426950408889634

LANES = 128
HEADS_PER_TILE = LANES // HEAD_DIM
VMEM_LIMIT_BYTES = 56 * 1024 * 1024

KEY_BLOCK = LANES
BAND_Q_ROWS = 2 * CHUNK
BAND_STEP_BLOCKS = 4
BAND_GROUP = 4
BAND_WINDOW = BAND_PAST + BAND_Q_ROWS
BIAS_TABLE_PAD = 384
BIAS_EXT = 768

SB_TILE = 256
SB_WIDTH = 512
SB_UNDERFLOW = 105.0

INPROJ_ROWS = 512
ROW_TILE = 1024

BF16 = jnp.bfloat16
F32 = jnp.float32


def _params(n_axes):
    return pltpu.CompilerParams(
        dimension_semantics=("arbitrary",) * n_axes,
        vmem_limit_bytes=VMEM_LIMIT_BYTES)


def _nt_dot(a, b):
    return lax.dot_general(a, b, (((1,), (1,)), ((), ())), preferred_element_type=F32)


def _nn_dot(a, b):
    return jnp.dot(a, b, preferred_element_type=F32)


def _silu(g):
    return g * (1.0 / (1.0 + jnp.exp(-g)))


def _rms_scale(x, gain):
    ms = jnp.mean(x * x, axis=-1, keepdims=True)
    return x * lax.rsqrt(ms + RMS_EPS) * gain


def _head_masks(m_rows):
    lane = lax.broadcasted_iota(jnp.int32, (m_rows, LANES), 1)
    return [(lane >= hh * HEAD_DIM) & (lane < (hh + 1) * HEAD_DIM)
            for hh in range(HEADS_PER_TILE)]


def _stack_heads(q_pair, in_head):
    zero = jnp.zeros_like(q_pair)
    return jnp.concatenate([jnp.where(m, q_pair, zero) for m in in_head], axis=0)


def _unstack_heads(o2, in_head):
    m_rows = o2.shape[0] // HEADS_PER_TILE
    o = o2[0:m_rows]
    for hh in range(1, HEADS_PER_TILE):
        o = jnp.where(in_head[hh], o2[hh * m_rows:(hh + 1) * m_rows], o)
    return o


def _pad_rows(a, rows):
    return jnp.concatenate([a, jnp.zeros((rows - a.shape[0], a.shape[1]), a.dtype)], axis=0)


def _inproj_kernel(x_ref, g_ref, w_ref,
                   qa_ref, ka_ref, va_ref, ga_ref, qb_ref, kb_ref, vb_ref, gb_ref,
                   ka16_ref, va16_ref, kb16_ref, vb16_ref):
    hn = _rms_scale(x_ref[...], g_ref[...]).astype(BF16)
    gw = qa_ref.shape[-1]

    def proj(n):
        return _nn_dot(hn, w_ref[:, n * gw:(n + 1) * gw])

    qa_ref[...] = (proj(0) * ATTN_SCALE).astype(BF16)
    u = proj(1)
    ka_ref[...] = u
    ka16_ref[...] = u.astype(BF16)
    u = proj(2)
    va_ref[...] = u
    va16_ref[...] = u.astype(BF16)
    ga_ref[...] = proj(3)
    qb_ref[...] = (proj(4) * ATTN_SCALE).astype(BF16)
    u = proj(5)
    kb_ref[...] = u
    kb16_ref[...] = u.astype(BF16)
    u = proj(6)
    vb_ref[...] = u
    vb16_ref[...] = u.astype(BF16)
    gb_ref[...] = proj(7)


def _inproj(x2d, g_pre, w16, tm):
    t, d = x2d.shape
    d_in = w16.shape[1]
    gw = d_in // 8
    row = lambda i: (i, 0)
    const = lambda i: (0, 0)
    f32_out = jax.ShapeDtypeStruct((t, gw), F32)
    b16_out = jax.ShapeDtypeStruct((t, gw), BF16)
    out_shape = (b16_out, f32_out, f32_out, f32_out, b16_out, f32_out, f32_out, f32_out,
                 b16_out, b16_out, b16_out, b16_out)
    return pl.pallas_call(
        _inproj_kernel,
        out_shape=out_shape,
        grid=(t // tm,),
        in_specs=[pl.BlockSpec((tm, d), row),
                  pl.BlockSpec((1, d), const),
                  pl.BlockSpec((d, d_in), const)],
        out_specs=tuple(pl.BlockSpec((tm, gw), row) for _ in out_shape),
        compiler_params=_params(1),
        name="inproj_sample",
    )(x2d, g_pre.reshape(1, d), w16)


def _inproj_prompt_kernel(*refs, n_prev, keep_first):
    x_ref, g_ref, wnn_ref, wnt_ref = refs[:4]
    prev = refs[4:4 + (4 if n_prev else 0)]
    (qa_ref, va16_ref, ga_ref, qb_ref, vb16_ref, gb_ref, kat16_ref, kbt16_ref,
     kb_st_ref, vb_st_ref, ka_st_ref, va_st_ref) = refs[4 + len(prev):]
    j = pl.program_id(1)
    gw = qa_ref.shape[-1]
    hn = _rms_scale(x_ref[...], g_ref[...]).astype(BF16)

    def proj(n):
        return _nn_dot(hn, wnn_ref[:, n * gw:(n + 1) * gw])

    qa_ref[...] = (proj(0) * ATTN_SCALE).astype(BF16)
    va = proj(1)
    va16_ref[...] = va.astype(BF16)
    ga_ref[...] = proj(2)
    qb_ref[...] = (proj(3) * ATTN_SCALE).astype(BF16)
    gb_ref[...] = proj(4)

    nt = _nt_dot(wnt_ref[...], hn)
    kat, kbt, vbt = nt[0:gw], nt[gw:2 * gw], nt[2 * gw:3 * gw]
    vb16_ref[...] = vbt.T.astype(BF16)
    for u in range(kat16_ref.shape[0]):
        blk = slice(u * KEY_BLOCK, (u + 1) * KEY_BLOCK)
        kat16_ref[u] = kat[:, blk].astype(BF16)
        kbt16_ref[u] = kbt[:, blk].astype(BF16)
    for dpt in range(n_prev):
        kb_st_ref[dpt] = prev[0][dpt]
        vb_st_ref[dpt] = prev[1][dpt]
    kb_st_ref[n_prev] = kbt
    vb_st_ref[n_prev] = vbt

    @pl.when(j >= keep_first)
    def _():
        for dpt in range(n_prev):
            ka_st_ref[dpt] = prev[2][dpt]
            va_st_ref[dpt] = prev[3][dpt]
        ka_st_ref[n_prev] = kat
        va_st_ref[n_prev] = va.T


def _inproj_prompt(x3d, g_pre, wnn16, wnt16, prev, keep):
    bsz, seq, d = x3d.shape
    tm = INPROJ_ROWS
    gw = wnt16.shape[0] // 3
    n_prev = prev[0].shape[0] if prev else 0
    assert seq % tm == 0 and keep % tm == 0 and tm % KEY_BLOCK == 0
    keep_first = (seq - keep) // tm
    tok = lambda b, j: (b, j, 0)
    const = lambda b, j: (0, 0)
    ktb = lambda b, j: (b, j, 0, 0)
    st_all = lambda b, j: (0, b, 0, j)
    st_keep = lambda b, j: (0, b, 0, jnp.maximum(j - keep_first, 0))
    tok_b16 = jax.ShapeDtypeStruct((bsz, seq, gw), BF16)
    tok_f32 = jax.ShapeDtypeStruct((bsz, seq, gw), F32)
    kt_b16 = jax.ShapeDtypeStruct((bsz, seq // KEY_BLOCK, gw, KEY_BLOCK), BF16)
    st_f32 = jax.ShapeDtypeStruct((n_prev + 1, bsz, gw, seq), F32)
    keep_f32 = jax.ShapeDtypeStruct((n_prev + 1, bsz, gw, keep), F32)
    out_shape = (tok_b16, tok_b16, tok_f32, tok_b16, tok_b16, tok_f32, kt_b16, kt_b16,
                 st_f32, st_f32, keep_f32, keep_f32)
    tok_spec = pl.BlockSpec((None, tm, gw), tok)
    kt_spec = pl.BlockSpec((None, tm // KEY_BLOCK, gw, KEY_BLOCK), ktb)
    out_specs = (tok_spec,) * 6 + (kt_spec,) * 2 + (
        pl.BlockSpec((n_prev + 1, None, gw, tm), st_all),
        pl.BlockSpec((n_prev + 1, None, gw, tm), st_all),
        pl.BlockSpec((n_prev + 1, None, gw, tm), st_keep),
        pl.BlockSpec((n_prev + 1, None, gw, tm), st_keep))
    in_specs = [pl.BlockSpec((None, tm, d), tok),
                pl.BlockSpec((1, d), const),
                pl.BlockSpec(wnn16.shape, const, pipeline_mode=pl.Buffered(1)),
                pl.BlockSpec(wnt16.shape, const, pipeline_mode=pl.Buffered(1))]
    if n_prev:
        in_specs += [pl.BlockSpec((n_prev, None, gw, tm), st_all),
                     pl.BlockSpec((n_prev, None, gw, tm), st_all),
                     pl.BlockSpec((n_prev, None, gw, tm), st_keep),
                     pl.BlockSpec((n_prev, None, gw, tm), st_keep)]
    return pl.pallas_call(
        functools.partial(_inproj_prompt_kernel, n_prev=n_prev, keep_first=keep_first),
        out_shape=out_shape,
        grid=(bsz, seq // tm),
        in_specs=in_specs,
        out_specs=out_specs,
        compiler_params=_params(2),
        name="inproj_prompt",
    )(x3d, g_pre.reshape(1, d), wnn16, wnt16, *prev)


def _finish_kernel(oa_ref, ob_ref, h_ref, p_ref, woa_ref, wob_ref, gpost_ref, wg_ref, wp_ref,
                   out_ref):
    y = _nn_dot(oa_ref[...], woa_ref[...]) + _nn_dot(ob_ref[...], wob_ref[...])
    h1 = h_ref[...] + _rms_scale(y, gpost_ref[...])
    gate_logit = _nn_dot(h1.astype(BF16), wg_ref[...])
    gate = 1.0 / (1.0 + jnp.exp(-gate_logit))
    emb = _nn_dot(p_ref[...].astype(BF16), wp_ref[...])
    out_ref[...] = h1 + gate * emb


def _finish(oa, ob, h2d, p_all, layer, wo16, g_post, wg16, wp16, tm):
    t, d = h2d.shape
    da = oa.shape[1]
    dp = p_all.shape[2]
    row = lambda i: (i, 0)
    const = lambda i: (0, 0)
    return pl.pallas_call(
        _finish_kernel,
        out_shape=jax.ShapeDtypeStruct((t, d), F32),
        grid=(t // tm,),
        in_specs=[pl.BlockSpec((tm, da), row),
                  pl.BlockSpec((tm, da), row),
                  pl.BlockSpec((tm, d), row),
                  pl.BlockSpec((None, tm, dp), lambda i: (layer, i, 0)),
                  pl.BlockSpec((da, d), const),
                  pl.BlockSpec((da, d), lambda i: (1, 0)),
                  pl.BlockSpec((1, d), const),
                  pl.BlockSpec((d, d), const),
                  pl.BlockSpec((dp, d), const)],
        out_specs=pl.BlockSpec((tm, d), row),
        compiler_params=_params(1),
        name="finish",
    )(oa, ob, h2d, p_all, wo16, wo16, g_post.reshape(1, d), wg16, wp16)


def _build_band_bias(tbl_ref, maskadd_ref, bias_ref):
    m_rows = maskadd_ref.shape[0]
    n_heads = bias_ref.shape[0] // m_rows
    tbl = tbl_ref[...]
    p1 = tbl.astype(BF16)
    r1 = tbl - p1.astype(F32)
    p2 = r1.astype(BF16)
    p3 = (r1 - p2.astype(F32)).astype(BF16)
    r_idx = lax.broadcasted_iota(jnp.int32, (BIAS_TABLE_PAD, BIAS_EXT), 0)
    m_idx = lax.broadcasted_iota(jnp.int32, (BIAS_TABLE_PAD, BIAS_EXT), 1)
    rel = jnp.clip(m_idx - (m_rows - 1) - BAND_PAST, -REL_CLIP, REL_CLIP) + REL_CLIP
    onehot = (r_idx == rel).astype(BF16)
    t_ext = _nn_dot(p1, onehot) + _nn_dot(p2, onehot) + _nn_dot(p3, onehot)
    ql = lax.broadcasted_iota(jnp.int32, (m_rows, BIAS_EXT), 0)
    n_bits = int(m_rows - 1).bit_length()
    for h in range(n_heads):
        x = jnp.broadcast_to(t_ext[h:h + 1, :], (m_rows, BIAS_EXT))
        x = pltpu.roll(x, BIAS_EXT - (m_rows - 1), 1)
        for b in range(n_bits):
            x = jnp.where(((ql >> b) & 1) == 1, pltpu.roll(x, 1 << b, 1), x)
        bias_ref[h * m_rows:(h + 1) * m_rows, :] = x[:, :BAND_WINDOW] + maskadd_ref[...]


def _band_attend(q_ref, g_ref, bias_ref, out_ref, m_rows, windows):
    n_tiles = q_ref.shape[1] // LANES
    in_head = _head_masks(m_rows)
    kl = lax.broadcasted_iota(jnp.int32, (1, BAND_WINDOW), 1)
    lane_sl = [slice(p * LANES, (p + 1) * LANES) for p in range(n_tiles)]
    row_sl = [slice(j * m_rows, (j + 1) * m_rows) for j in range(len(windows))]
    rows2 = HEADS_PER_TILE * m_rows
    items = [(j, p) for j in range(len(windows)) for p in range(n_tiles)]

    def item_scores(n):
        j, p = items[n]
        return windows[j][0](lane_sl[p], _stack_heads(q_ref[row_sl[j], lane_sl[p]], in_head))

    groups = [list(range(a, min(a + BAND_GROUP, len(items))))
              for a in range(0, len(items), BAND_GROUP)]
    s_next = [item_scores(n) for n in groups[0]]
    for gi, group in enumerate(groups):
        ss = [s_next[i] + bias_ref[items[n][1] * rows2:(items[n][1] + 1) * rows2, :]
              for i, n in enumerate(group)]
        if gi + 1 < len(groups):
            s_next = [item_scores(n) for n in groups[gi + 1]]
        es = []
        for i, n in enumerate(group):
            s, key_lo = ss[i], windows[items[n][0]][2]
            if key_lo is not None:
                s = jnp.where(kl >= key_lo, s, NEG_INF)
            es.append(jnp.exp(s - jnp.max(s, axis=-1, keepdims=True)).astype(BF16))
        for i, n in enumerate(group):
            j, p = items[n]
            o_ext = windows[j][1](lane_sl[p], es[i])
            o2 = o_ext[:, :LANES] * (1.0 / o_ext[:, LANES:])
            o_pair = _unstack_heads(o2, in_head)
            out_ref[row_sl[j], lane_sl[p]] = (
                o_pair * _silu(g_ref[row_sl[j], lane_sl[p]])).astype(BF16)


def _band_prompt_kernel(q_ref, g_ref, kt_ref, v_ref, tbl_ref, maskadd_ref, out_ref, bias_ref):
    b = pl.program_id(0)
    step = pl.program_id(1)
    n_prev = BAND_PAST // KEY_BLOCK
    n_blk = BAND_WINDOW // KEY_BLOCK

    @pl.when((b == 0) & (step == 0))
    def _():
        _build_band_bias(tbl_ref, maskadd_ref, bias_ref)

    def window(c, at_start):
        first = c - n_prev
        blocks = [max(first + t, 0) if at_start else first + t for t in range(n_blk)]

        def scores(sl, q2):
            kt = jnp.concatenate([kt_ref[blk, sl, :] for blk in blocks], axis=1)
            return _nn_dot(q2, kt)

        def weighted_values(sl, e16):
            if at_start:
                v = jnp.concatenate(
                    [v_ref[blk * KEY_BLOCK:(blk + 1) * KEY_BLOCK, sl] for blk in blocks], axis=0)
            else:
                v = v_ref[pl.ds(pl.multiple_of(first * KEY_BLOCK, KEY_BLOCK), BAND_WINDOW), sl]
            return _nn_dot(e16, jnp.concatenate([v, jnp.ones_like(v)], axis=1))

        return scores, weighted_values, (n_prev - c) * KEY_BLOCK if at_start else None

    @pl.when(step == 0)
    def _():
        _band_attend(q_ref, g_ref, bias_ref, out_ref, BAND_Q_ROWS,
                     [window(j, True) for j in range(BAND_STEP_BLOCKS)])

    @pl.when(step > 0)
    def _():
        _band_attend(q_ref, g_ref, bias_ref, out_ref, BAND_Q_ROWS,
                     [window(step * BAND_STEP_BLOCKS + j, False)
                      for j in range(BAND_STEP_BLOCKS)])


def _band_static_mask(q_pos, k_pos):
    qc = q_pos // CHUNK
    kc = k_pos // CHUNK
    ok = (kc[None, :] <= qc[:, None]) & (kc[None, :] >= qc[:, None] - N_PREV_CHUNKS)
    return np.where(ok, 0.0, NEG_INF).astype(np.float32)


def _pad_table(rel_bias):
    return jnp.pad(rel_bias, ((0, 0), (0, BIAS_TABLE_PAD - rel_bias.shape[1])))


def _band_prompt(q16, g, kt16, v16, rel_bias):
    bsz, seq, da = q16.shape
    n_heads = da // HEAD_DIM
    step_rows = BAND_STEP_BLOCKS * BAND_Q_ROWS
    assert BAND_Q_ROWS == KEY_BLOCK and seq % step_rows == 0
    assert BAND_STEP_BLOCKS == BAND_PAST // KEY_BLOCK
    q_pos = BAND_PAST + np.arange(BAND_Q_ROWS)
    k_pos = np.arange(BAND_WINDOW)
    maskadd = jnp.asarray(_band_static_mask(q_pos, k_pos))
    blk = lambda b, c: (b, c, 0)
    whole = lambda b, c: (b, 0, 0)
    whole4 = lambda b, c: (b, 0, 0, 0)
    const = lambda b, c: (0, 0)
    return pl.pallas_call(
        _band_prompt_kernel,
        out_shape=jax.ShapeDtypeStruct((bsz, seq, da), BF16),
        grid=(bsz, seq // step_rows),
        in_specs=[pl.BlockSpec((None, step_rows, da), blk),
                  pl.BlockSpec((None, step_rows, da), blk),
                  pl.BlockSpec((None, seq // KEY_BLOCK, da, KEY_BLOCK), whole4),
                  pl.BlockSpec((None, seq, da), whole),
                  pl.BlockSpec((n_heads, BIAS_TABLE_PAD), const),
                  pl.BlockSpec((BAND_Q_ROWS, BAND_WINDOW), const)],
        out_specs=pl.BlockSpec((None, step_rows, da), blk),
        scratch_shapes=[pltpu.VMEM((n_heads * BAND_Q_ROWS, BAND_WINDOW), F32)],
        compiler_params=_params(2),
        name="band_prompt",
    )(q16, g, kt16, v16, _pad_table(rel_bias), maskadd)


def _band_sample_kernel(q_ref, g_ref, ckt_ref, cvt_ref, nk_ref, nv_ref, tbl_ref, maskadd_ref,
                        out_ref, bias_ref):
    b = pl.program_id(0)
    la = ckt_ref.shape[1]

    @pl.when(b == 0)
    def _():
        _build_band_bias(tbl_ref, maskadd_ref, bias_ref)

    def scores(sl, q2):
        s_cache = _nn_dot(q2, ckt_ref[sl, :].astype(BF16))
        s_new = _nt_dot(q2, _pad_rows(nk_ref[:, sl], KEY_BLOCK))
        return jnp.concatenate([s_cache, s_new], axis=1)

    def weighted_values(sl, e16):
        cvt = cvt_ref[sl, :].astype(BF16)
        nv = _pad_rows(nv_ref[:, sl], KEY_BLOCK)
        return (_nt_dot(e16[:, :la], jnp.concatenate([cvt, jnp.ones_like(cvt)], axis=0))
                + _nn_dot(e16[:, la:], jnp.concatenate([nv, jnp.ones_like(nv)], axis=1)))

    _band_attend(q_ref, g_ref, bias_ref, out_ref, q_ref.shape[0],
                 [(scores, weighted_values, None)])


def _band_sample(q16, g, cache_kt, cache_vt, layer, nk16, nv16, rel_bias, past):
    bsz, t, da = q16.shape
    la = cache_kt.shape[3]
    n_heads = da // HEAD_DIM
    assert la == BAND_PAST and la + KEY_BLOCK == BAND_WINDOW and t <= KEY_BLOCK and t % 16 == 0
    q_pos = past + np.arange(t)
    k_pos = past - la + np.arange(BAND_WINDOW)
    maskadd = _band_static_mask(q_pos, k_pos)
    maskadd[:, la + t:] = NEG_INF
    maskadd = jnp.asarray(maskadd)
    blk = lambda b: (b, 0, 0)
    const = lambda b: (0, 0)
    return pl.pallas_call(
        _band_sample_kernel,
        out_shape=jax.ShapeDtypeStruct((bsz, t, da), BF16),
        grid=(bsz,),
        in_specs=[pl.BlockSpec((None, t, da), blk),
                  pl.BlockSpec((None, t, da), blk),
                  pl.BlockSpec((None, None, da, la), lambda b: (layer, b, 0, 0)),
                  pl.BlockSpec((None, None, da, la), lambda b: (layer, b, 0, 0)),
                  pl.BlockSpec((None, t, da), blk),
                  pl.BlockSpec((None, t, da), blk),
                  pl.BlockSpec((n_heads, BIAS_TABLE_PAD), const),
                  pl.BlockSpec((t, BAND_WINDOW), const)],
        out_specs=pl.BlockSpec((None, t, da), blk),
        scratch_shapes=[pltpu.VMEM((n_heads * t, BAND_WINDOW), F32)],
        compiler_params=_params(1),
        name="band_sample",
    )(q16, g, cache_kt, cache_vt, nk16, nv16, _pad_table(rel_bias), maskadd)


def _suffix_matrix(width):
    j = np.arange(width)[:, None]
    s = np.arange(width)[None, :]
    tri = (j >= s).astype(np.float32)
    return jnp.asarray(np.concatenate([tri, tri], axis=0), dtype=BF16)


def _sb_tiles(q2s, tiles, carry_ref, acc_ref, first, k_is_t, v_is_t):
    n = len(q2s)
    zs = [[(_nn_dot if k_is_t else _nt_dot)(q2s[i], ks[i]) for i in range(n)]
          for ks, _, _, _ in tiles]
    split = []
    for t, (_, _, _, causal) in enumerate(tiles):
        split.append([])
        for z in zs[t]:
            sp = jnp.maximum(z, 0.0) + jnp.log(1.0 + jnp.exp2(jnp.abs(z) * NEG_LOG2E))
            if causal is not None:
                sp = jnp.where(causal, sp, 0.0)
            hi = sp.astype(BF16)
            lo = (sp - hi.astype(F32)).astype(BF16)
            split[t].append(jnp.concatenate([hi, lo], axis=1))
    sums = [[_nn_dot(s, tri) for s in split[t]] for t, (_, _, tri, _) in enumerate(tiles)]
    carries = [None if first else carry_ref[i] for i in range(n)]
    ws = []
    for t, (_, _, _, causal) in enumerate(tiles):
        ws.append([])
        for i in range(n):
            log_w = zs[t][i] - sums[t][i]
            total = sums[t][i][:, 0:1]
            if carries[i] is not None:
                log_w = log_w - carries[i]
                total = total + carries[i]
            w = jnp.exp(log_w)
            if causal is not None:
                w = jnp.where(causal, w, 0.0)
            ws[t].append(w.astype(BF16))
            carries[i] = total
    for i in range(n):
        acc = None if first else acc_ref[i]
        for t, (_, vs, _, _) in enumerate(tiles):
            part = (_nt_dot if v_is_t else _nn_dot)(ws[t][i], vs[i])
            acc = part if acc is None else acc + part
        acc_ref[i] = acc
        carry_ref[i] = carries[i]
    return carries


def _sb_setup(q_ref, td):
    tq, width = q_ref.shape
    lane_sl = [slice(p * LANES, (p + 1) * LANES) for p in range(width // LANES)]
    in_head = _head_masks(tq)
    rows2 = HEADS_PER_TILE * tq
    row = lax.broadcasted_iota(jnp.int32, (rows2, td), 0) % tq
    col = lax.broadcasted_iota(jnp.int32, (rows2, td), 1)
    causal = col < row
    q2s = [_stack_heads(q_ref[:, sl], in_head) for sl in lane_sl]
    return lane_sl, in_head, causal, q2s


def _sb_min_carry(carries):
    m = carries[0]
    for c in carries[1:]:
        m = jnp.minimum(m, c)
    return jnp.min(m)


def _sb_store(acc_ref, lane_sl, in_head, g_ref, out_ref):
    for n, sl in enumerate(lane_sl):
        o_pair = _unstack_heads(acc_ref[n], in_head)
        out_ref[:, sl] = (o_pair * _silu(g_ref[:, sl])).astype(BF16)


def _sb_scratch(rows, width):
    n = width // LANES
    return [pltpu.VMEM((n, HEADS_PER_TILE * rows, 1), F32),
            pltpu.VMEM((n, HEADS_PER_TILE * rows, LANES), F32)]


def _sb_prompt_kernel(q_ref, g_ref, kt_ref, v_ref, tri_ref, out_ref, carry_ref, acc_ref):
    tk = tri_ref.shape[1]
    n_full = pl.program_id(2)
    blocks_per_tile = tk // KEY_BLOCK
    tri = tri_ref[...]
    lane_sl, in_head, causal, q2s = _sb_setup(q_ref, tk)
    n = len(lane_sl)

    def load(tile):
        kts = [jnp.concatenate([kt_ref[tile * blocks_per_tile + u, sl, :]
                                for u in range(blocks_per_tile)], axis=1) for sl in lane_sl]
        start = pl.multiple_of(tile * tk, tk)
        return kts, [v_ref[pl.ds(start, tk), sl] for sl in lane_sl]

    @pl.when(n_full == 0)
    def _():
        _sb_tiles(q2s, [load(n_full) + (tri, causal)], carry_ref, acc_ref, True, True, False)

    @pl.when(n_full > 0)
    def _():
        _sb_tiles(q2s, [load(n_full) + (tri, causal), load(n_full - 1) + (tri, None)],
                  carry_ref, acc_ref, True, True, False)

    def cond(loop_state):
        it, smallest = loop_state
        return (it < n_full) & (smallest < SB_UNDERFLOW)

    def body(loop_state):
        it, _ = loop_state
        carries = _sb_tiles(q2s, [load(n_full - 1 - it) + (tri, None)], carry_ref, acc_ref,
                            False, True, False)
        return it + 1, _sb_min_carry(carries)

    lax.while_loop(cond, body,
                   (jnp.int32(1), _sb_min_carry([carry_ref[i] for i in range(n)])))
    _sb_store(acc_ref, lane_sl, in_head, g_ref, out_ref)


def _sb_prompt(q16, g, kt16, v16):
    bsz, seq, db = q16.shape
    t, width = SB_TILE, SB_WIDTH
    assert seq % t == 0 and db % width == 0 and t % KEY_BLOCK == 0
    qblk = lambda b, p, i: (b, i, p)
    whole = lambda b, p, i: (b, 0, p)
    whole4 = lambda b, p, i: (b, 0, p, 0)
    const = lambda b, p, i: (0, 0)
    return pl.pallas_call(
        _sb_prompt_kernel,
        out_shape=jax.ShapeDtypeStruct((bsz, seq, db), BF16),
        grid=(bsz, db // width, seq // t),
        in_specs=[pl.BlockSpec((None, t, width), qblk),
                  pl.BlockSpec((None, t, width), qblk),
                  pl.BlockSpec((None, seq // KEY_BLOCK, width, KEY_BLOCK), whole4),
                  pl.BlockSpec((None, seq, width), whole),
                  pl.BlockSpec((2 * t, t), const)],
        out_specs=pl.BlockSpec((None, t, width), qblk),
        scratch_shapes=_sb_scratch(t, width),
        compiler_params=_params(3),
        name="sb_prompt",
    )(q16, g, kt16, v16, _suffix_matrix(t))


def _sb_sample_kernel(q_ref, g_ref, nk_ref, nv_ref, ckt_ref, cvt_ref, trid_ref, trif_ref,
                      out_ref, carry_ref, acc_ref):
    td = trid_ref.shape[1]
    tk = trif_ref.shape[1]
    n_full = ckt_ref.shape[1] // tk
    trif = trif_ref[...]
    lane_sl, in_head, causal, q2s = _sb_setup(q_ref, td)

    carries = _sb_tiles(q2s, [([_pad_rows(nk_ref[:, sl], td) for sl in lane_sl],
                               [_pad_rows(nv_ref[:, sl], td) for sl in lane_sl],
                               trid_ref[...], causal)], carry_ref, acc_ref, True, False, False)
    smallest = _sb_min_carry(carries)

    for tile in reversed(range(n_full)):
        cols = slice(tile * tk, (tile + 1) * tk)

        def step(cols=cols):
            kts = [ckt_ref[sl, cols].astype(BF16) for sl in lane_sl]
            vts = [cvt_ref[sl, cols].astype(BF16) for sl in lane_sl]
            return _sb_min_carry(_sb_tiles(q2s, [(kts, vts, trif, None)], carry_ref, acc_ref,
                                           False, True, True))

        smallest = lax.cond(smallest < SB_UNDERFLOW, step, lambda s=smallest: s)
    _sb_store(acc_ref, lane_sl, in_head, g_ref, out_ref)


def _sb_sample(q16, g, nk16, nv16, cache_kt, cache_vt, layer):
    bsz, t, db = q16.shape
    past = cache_kt.shape[3]
    width, tk, td = SB_WIDTH, SB_TILE, KEY_BLOCK
    assert t <= td and t % 16 == 0 and past % tk == 0 and db % width == 0
    blk = lambda b, p: (b, 0, p)
    cblk = lambda b, p: (layer, b, p, 0)
    const = lambda b, p: (0, 0)
    return pl.pallas_call(
        _sb_sample_kernel,
        out_shape=jax.ShapeDtypeStruct((bsz, t, db), BF16),
        grid=(bsz, db // width),
        in_specs=[pl.BlockSpec((None, t, width), blk),
                  pl.BlockSpec((None, t, width), blk),
                  pl.BlockSpec((None, t, width), blk),
                  pl.BlockSpec((None, t, width), blk),
                  pl.BlockSpec((None, None, width, past), cblk),
                  pl.BlockSpec((None, None, width, past), cblk),
                  pl.BlockSpec((2 * td, td), const),
                  pl.BlockSpec((2 * tk, tk), const)],
        out_specs=pl.BlockSpec((None, t, width), blk),
        scratch_shapes=_sb_scratch(t, width),
        compiler_params=_params(2),
        name="sb_sample",
    )(q16, g, nk16, nv16, cache_kt, cache_vt, _suffix_matrix(td), _suffix_matrix(tk))


def _row_tile(t):
    return ROW_TILE if t % ROW_TILE == 0 else t


def _cache_transposed(cache):
    l, b, s, h, dh = cache.shape
    return jnp.transpose(cache, (0, 1, 3, 4, 2)).reshape(l, b, h * dh, s)


def _cache_layout(stacked, n_heads):
    l, b, hd, s = stacked.shape
    return jnp.transpose(stacked.reshape(l, b, n_heads, hd // n_heads, s), (0, 1, 4, 2, 3))


def kernel(x_prompt, x_sample, p_prompt, p_sample, cache_a_k, cache_a_v, cache_b_k, cache_b_v,
           g_pre, w_in, rel_bias, w_out, g_post, w_ple, w_ple_gate):
    depth = w_in.shape[0]
    bsz, seq, d = x_prompt.shape
    dbsz, dseq, _ = x_sample.shape
    past = cache_b_k.shape[2]
    keep = min(BAND_PAST, seq)
    da = w_in.shape[2] // 8
    n_heads = da // HEAD_DIM
    tp, ts = bsz * seq, dbsz * dseq

    hp = x_prompt
    hs = x_sample.reshape(ts, d)
    p_prompt = p_prompt.reshape(depth, tp, -1)
    p_sample = p_sample.reshape(depth, ts, -1)
    cache_a_kt, cache_a_vt, cache_b_kt, cache_b_vt = (
        _cache_transposed(c) for c in (cache_a_k, cache_a_v, cache_b_k, cache_b_v))
    stacked = ()
    sak, sav, sbk, sbv = [], [], [], []
    for i in range(depth):
        w16 = w_in[i].astype(BF16)
        grp = lambda n: w16[:, n * da:(n + 1) * da]
        wnn16 = jnp.concatenate([grp(0), grp(2), grp(3), grp(4), grp(7)], axis=1)
        wnt16 = jnp.concatenate([grp(1), grp(5), grp(6)], axis=1).T
        wo16 = w_out[i].astype(BF16)
        wg16 = w_ple_gate[i].astype(BF16)
        wp16 = w_ple[i].astype(BF16)

        (qa, va16, ga, qb, vb16, gb, kat16, kbt16, *stacked) = _inproj_prompt(
            hp, g_pre[i], wnn16, wnt16, tuple(stacked), keep)
        oa = _band_prompt(qa, ga, kat16, va16, rel_bias[i])
        ob = _sb_prompt(qb, gb, kbt16, vb16)
        hp = _finish(oa.reshape(tp, da), ob.reshape(tp, da), hp.reshape(tp, d),
                     p_prompt, i, wo16, g_post[i], wg16, wp16,
                     _row_tile(tp)).reshape(bsz, seq, d)

        (qa, ka, va, ga, qb, kb, vb, gb, ka16, va16, kb16, vb16) = _inproj(
            hs, g_pre[i], w16, _row_tile(ts))
        s3 = lambda a: a.reshape(dbsz, dseq, da)
        oa = _band_sample(s3(qa), s3(ga), cache_a_kt, cache_a_vt, i, s3(ka16), s3(va16),
                          rel_bias[i], past)
        ob = _sb_sample(s3(qb), s3(gb), s3(kb16), s3(vb16), cache_b_kt, cache_b_vt, i)
        hs = _finish(oa.reshape(ts, da), ob.reshape(ts, da), hs, p_sample, i,
                     wo16, g_post[i], wg16, wp16, _row_tile(ts))
        s4 = lambda a: a.reshape(dbsz, dseq, n_heads, HEAD_DIM)
        sak.append(s4(ka))
        sav.append(s4(va))
        sbk.append(s4(kb))
        sbv.append(s4(vb))

    kb_st, vb_st, ka_st, va_st = stacked
    return (hp, hs.reshape(dbsz, dseq, d),
            _cache_layout(ka_st, n_heads), _cache_layout(va_st, n_heads),
            _cache_layout(kb_st, n_heads), _cache_layout(vb_st, n_heads),
            jnp.stack(sak), jnp.stack(sav), jnp.stack(sbk), jnp.stack(sbv))
```

```python
import functools

import numpy as np
import jax
import jax.numpy as jnp
from jax import lax
from jax.experimental import pallas as pl
from jax.experimental.pallas import tpu as pltpu

HEAD_DIM = 64
CHUNK = 64
N_PREV_CHUNKS = 8
BAND_PAST = N_PREV_CHUNKS * CHUNK
REL_CLIP = 128
RMS_EPS = 1e-6
NEG_INF = -1e30
ATTN_SCALE = HEAD_DIM ** -0.5
NEG_LOG2E = -1.4426950408889634

LANES = 128
HEADS_PER_TILE = LANES // HEAD_DIM
VMEM_LIMIT_BYTES = 56 * 1024 * 1024

KEY_BLOCK = LANES
BAND_Q_ROWS = 2 * CHUNK
BAND_STEP_BLOCKS = 4
BAND_GROUP = 4
BAND_WINDOW = BAND_PAST + BAND_Q_ROWS
BIAS_TABLE_PAD = 384
BIAS_EXT = 768

SB_TILE = 256
SB_Q_ROWS = 128
SB_STEP_BLOCKS = 2
SB_WINDOW_BLOCKS = 3
SB_WIDTH = 512
SB_UNDERFLOW = 105.0

INPROJ_ROWS = 512
ROW_TILE = 1024

BF16 = jnp.bfloat16
F32 = jnp.float32


def _params(n_axes):
    return pltpu.CompilerParams(
        dimension_semantics=("arbitrary",) * n_axes,
        vmem_limit_bytes=VMEM_LIMIT_BYTES)


def _nt_dot(a, b):
    return lax.dot_general(a, b, (((1,), (1,)), ((), ())), preferred_element_type=F32)


def _nn_dot(a, b):
    return jnp.dot(a, b, preferred_element_type=F32)


def _silu(g):
    return g * (1.0 / (1.0 + jnp.exp(-g)))


def _rms_scale(x, gain):
    ms = jnp.mean(x * x, axis=-1, keepdims=True)
    return x * lax.rsqrt(ms + RMS_EPS) * gain


def _head_masks(m_rows):
    lane = lax.broadcasted_iota(jnp.int32, (m_rows, LANES), 1)
    return [(lane >= hh * HEAD_DIM) & (lane < (hh + 1) * HEAD_DIM)
            for hh in range(HEADS_PER_TILE)]


def _stack_heads(q_pair, in_head):
    zero = jnp.zeros_like(q_pair)
    return jnp.concatenate([jnp.where(m, q_pair, zero) for m in in_head], axis=0)


def _unstack_heads(o2, in_head):
    m_rows = o2.shape[0] // HEADS_PER_TILE
    o = o2[0:m_rows]
    for hh in range(1, HEADS_PER_TILE):
        o = jnp.where(in_head[hh], o2[hh * m_rows:(hh + 1) * m_rows], o)
    return o


def _pad_rows(a, rows):
    return jnp.concatenate([a, jnp.zeros((rows - a.shape[0], a.shape[1]), a.dtype)], axis=0)


def _inproj_kernel(x_ref, g_ref, w_ref,
                   qa_ref, ka_ref, va_ref, ga_ref, qb_ref, kb_ref, vb_ref, gb_ref,
                   ka16_ref, va16_ref, kb16_ref, vb16_ref):
    hn = _rms_scale(x_ref[...], g_ref[...]).astype(BF16)
    gw = qa_ref.shape[-1]

    def proj(n):
        return _nn_dot(hn, w_ref[:, n * gw:(n + 1) * gw])

    qa_ref[...] = (proj(0) * ATTN_SCALE).astype(BF16)
    u = proj(1)
    ka_ref[...] = u
    ka16_ref[...] = u.astype(BF16)
    u = proj(2)
    va_ref[...] = u
    va16_ref[...] = u.astype(BF16)
    ga_ref[...] = proj(3)
    qb_ref[...] = (proj(4) * ATTN_SCALE).astype(BF16)
    u = proj(5)
    kb_ref[...] = u
    kb16_ref[...] = u.astype(BF16)
    u = proj(6)
    vb_ref[...] = u
    vb16_ref[...] = u.astype(BF16)
    gb_ref[...] = proj(7)


def _inproj(x2d, g_pre, w16, tm):
    t, d = x2d.shape
    d_in = w16.shape[1]
    gw = d_in // 8
    row = lambda i: (i, 0)
    const = lambda i: (0, 0)
    f32_out = jax.ShapeDtypeStruct((t, gw), F32)
    b16_out = jax.ShapeDtypeStruct((t, gw), BF16)
    out_shape = (b16_out, f32_out, f32_out, f32_out, b16_out, f32_out, f32_out, f32_out,
                 b16_out, b16_out, b16_out, b16_out)
    return pl.pallas_call(
        _inproj_kernel,
        out_shape=out_shape,
        grid=(t // tm,),
        in_specs=[pl.BlockSpec((tm, d), row),
                  pl.BlockSpec((1, d), const),
                  pl.BlockSpec((d, d_in), const)],
        out_specs=tuple(pl.BlockSpec((tm, gw), row) for _ in out_shape),
        compiler_params=_params(1),
        name="inproj_sample",
    )(x2d, g_pre.reshape(1, d), w16)


def _inproj_prompt_kernel(*refs, n_prev, keep_first):
    x_ref, g_ref, wnn_ref, wnt_ref = refs[:4]
    prev = refs[4:4 + (4 if n_prev else 0)]
    (qa_ref, va16_ref, ga_ref, qb_ref, vb16_ref, gb_ref, kat16_ref, kbt16_ref,
     kb_st_ref, vb_st_ref, ka_st_ref, va_st_ref) = refs[4 + len(prev):]
    j = pl.program_id(1)
    gw = qa_ref.shape[-1]
    hn = _rms_scale(x_ref[...], g_ref[...]).astype(BF16)

    def proj(n):
        return _nn_dot(hn, wnn_ref[:, n * gw:(n + 1) * gw])

    nt = _nt_dot(wnt_ref[...], hn)
    kat, kbt, vbt = nt[0:gw], nt[gw:2 * gw], nt[2 * gw:3 * gw]
    vb16_ref[...] = vbt.T.astype(BF16)
    for u in range(kat16_ref.shape[0]):
        blk = slice(u * KEY_BLOCK, (u + 1) * KEY_BLOCK)
        kat16_ref[u] = kat[:, blk].astype(BF16)
        kbt16_ref[u] = kbt[:, blk].astype(BF16)
    for dpt in range(n_prev):
        kb_st_ref[dpt] = prev[0][dpt]
        vb_st_ref[dpt] = prev[1][dpt]
    kb_st_ref[n_prev] = kbt
    vb_st_ref[n_prev] = vbt

    qa_ref[...] = (proj(0) * ATTN_SCALE).astype(BF16)
    va = proj(1)
    va16_ref[...] = va.astype(BF16)
    ga_ref[...] = proj(2)
    qb_ref[...] = (proj(3) * ATTN_SCALE).astype(BF16)
    gb_ref[...] = proj(4)

    @pl.when(j >= keep_first)
    def _():
        for dpt in range(n_prev):
            ka_st_ref[dpt] = prev[2][dpt]
            va_st_ref[dpt] = prev[3][dpt]
        ka_st_ref[n_prev] = kat
        va_st_ref[n_prev] = va.T


def _inproj_prompt(x3d, g_pre, wnn16, wnt16, prev, keep):
    bsz, seq, d = x3d.shape
    tm = INPROJ_ROWS
    gw = wnt16.shape[0] // 3
    n_prev = prev[0].shape[0] if prev else 0
    assert seq % tm == 0 and keep % tm == 0 and tm % KEY_BLOCK == 0
    keep_first = (seq - keep) // tm
    tok = lambda b, j: (b, j, 0)
    const = lambda b, j: (0, 0)
    ktb = lambda b, j: (b, j, 0, 0)
    st_all = lambda b, j: (0, b, 0, j)
    st_keep = lambda b, j: (0, b, 0, jnp.maximum(j - keep_first, 0))
    tok_b16 = jax.ShapeDtypeStruct((bsz, seq, gw), BF16)
    tok_f32 = jax.ShapeDtypeStruct((bsz, seq, gw), F32)
    kt_b16 = jax.ShapeDtypeStruct((bsz, seq // KEY_BLOCK, gw, KEY_BLOCK), BF16)
    st_f32 = jax.ShapeDtypeStruct((n_prev + 1, bsz, gw, seq), F32)
    keep_f32 = jax.ShapeDtypeStruct((n_prev + 1, bsz, gw, keep), F32)
    out_shape = (tok_b16, tok_b16, tok_f32, tok_b16, tok_b16, tok_f32, kt_b16, kt_b16,
                 st_f32, st_f32, keep_f32, keep_f32)
    tok_spec = pl.BlockSpec((None, tm, gw), tok)
    kt_spec = pl.BlockSpec((None, tm // KEY_BLOCK, gw, KEY_BLOCK), ktb)
    out_specs = (tok_spec,) * 6 + (kt_spec,) * 2 + (
        pl.BlockSpec((n_prev + 1, None, gw, tm), st_all),
        pl.BlockSpec((n_prev + 1, None, gw, tm), st_all),
        pl.BlockSpec((n_prev + 1, None, gw, tm), st_keep),
        pl.BlockSpec((n_prev + 1, None, gw, tm), st_keep))
    in_specs = [pl.BlockSpec((None, tm, d), tok),
                pl.BlockSpec((1, d), const),
                pl.BlockSpec(wnn16.shape, const, pipeline_mode=pl.Buffered(1)),
                pl.BlockSpec(wnt16.shape, const, pipeline_mode=pl.Buffered(1))]
    if n_prev:
        in_specs += [pl.BlockSpec((n_prev, None, gw, tm), st_all),
                     pl.BlockSpec((n_prev, None, gw, tm), st_all),
                     pl.BlockSpec((n_prev, None, gw, tm), st_keep),
                     pl.BlockSpec((n_prev, None, gw, tm), st_keep)]
    return pl.pallas_call(
        functools.partial(_inproj_prompt_kernel, n_prev=n_prev, keep_first=keep_first),
        out_shape=out_shape,
        grid=(bsz, seq // tm),
        in_specs=in_specs,
        out_specs=out_specs,
        compiler_params=_params(2),
        name="inproj_prompt",
    )(x3d, g_pre.reshape(1, d), wnn16, wnt16, *prev)


def _finish_kernel(oa_ref, ob_ref, h_ref, p_ref, woa_ref, wob_ref, gpost_ref, wg_ref, wp_ref,
                   out_ref):
    y = _nn_dot(oa_ref[...], woa_ref[...]) + _nn_dot(ob_ref[...], wob_ref[...])
    h1 = h_ref[...] + _rms_scale(y, gpost_ref[...])
    gate_logit = _nn_dot(h1.astype(BF16), wg_ref[...])
    gate = 1.0 / (1.0 + jnp.exp(-gate_logit))
    emb = _nn_dot(p_ref[...].astype(BF16), wp_ref[...])
    out_ref[...] = h1 + gate * emb


def _finish(oa, ob, h2d, p_all, layer, wo16, g_post, wg16, wp16, tm):
    t, d = h2d.shape
    da = oa.shape[1]
    dp = p_all.shape[2]
    row = lambda i: (i, 0)
    const = lambda i: (0, 0)
    return pl.pallas_call(
        _finish_kernel,
        out_shape=jax.ShapeDtypeStruct((t, d), F32),
        grid=(t // tm,),
        in_specs=[pl.BlockSpec((tm, da), row),
                  pl.BlockSpec((tm, da), row),
                  pl.BlockSpec((tm, d), row),
                  pl.BlockSpec((None, tm, dp), lambda i: (layer, i, 0)),
                  pl.BlockSpec((da, d), const),
                  pl.BlockSpec((da, d), lambda i: (1, 0)),
                  pl.BlockSpec((1, d), const),
                  pl.BlockSpec((d, d), const),
                  pl.BlockSpec((dp, d), const)],
        out_specs=pl.BlockSpec((tm, d), row),
        compiler_params=_params(1),
        name="finish",
    )(oa, ob, h2d, p_all, wo16, wo16, g_post.reshape(1, d), wg16, wp16)


def _build_band_bias(tbl_ref, maskadd_ref, bias_ref):
    m_rows = maskadd_ref.shape[0]
    n_heads = bias_ref.shape[0] // m_rows
    tbl = tbl_ref[...]
    p1 = tbl.astype(BF16)
    r1 = tbl - p1.astype(F32)
    p2 = r1.astype(BF16)
    p3 = (r1 - p2.astype(F32)).astype(BF16)
    r_idx = lax.broadcasted_iota(jnp.int32, (BIAS_TABLE_PAD, BIAS_EXT), 0)
    m_idx = lax.broadcasted_iota(jnp.int32, (BIAS_TABLE_PAD, BIAS_EXT), 1)
    rel = jnp.clip(m_idx - (m_rows - 1) - BAND_PAST, -REL_CLIP, REL_CLIP) + REL_CLIP
    onehot = (r_idx == rel).astype(BF16)
    t_ext = _nn_dot(p1, onehot) + _nn_dot(p2, onehot) + _nn_dot(p3, onehot)
    ql = lax.broadcasted_iota(jnp.int32, (m_rows, BIAS_EXT), 0)
    n_bits = int(m_rows - 1).bit_length()
    for h in range(n_heads):
        x = jnp.broadcast_to(t_ext[h:h + 1, :], (m_rows, BIAS_EXT))
        x = pltpu.roll(x, BIAS_EXT - (m_rows - 1), 1)
        for b in range(n_bits):
            x = jnp.where(((ql >> b) & 1) == 1, pltpu.roll(x, 1 << b, 1), x)
        bias_ref[h * m_rows:(h + 1) * m_rows, :] = x[:, :BAND_WINDOW] + maskadd_ref[...]


def _band_attend(q_ref, g_ref, bias_ref, out_ref, m_rows, windows):
    n_tiles = q_ref.shape[1] // LANES
    in_head = _head_masks(m_rows)
    kl = lax.broadcasted_iota(jnp.int32, (1, BAND_WINDOW), 1)
    lane_sl = [slice(p * LANES, (p + 1) * LANES) for p in range(n_tiles)]
    row_sl = [slice(j * m_rows, (j + 1) * m_rows) for j in range(len(windows))]
    rows2 = HEADS_PER_TILE * m_rows
    items = [(j, p) for j in range(len(windows)) for p in range(n_tiles)]

    def item_scores(n):
        j, p = items[n]
        return windows[j][0](lane_sl[p], _stack_heads(q_ref[row_sl[j], lane_sl[p]], in_head))

    groups = [list(range(a, min(a + BAND_GROUP, len(items))))
              for a in range(0, len(items), BAND_GROUP)]
    s_next = [item_scores(n) for n in groups[0]]
    for gi, group in enumerate(groups):
        ss = [s_next[i] + bias_ref[items[n][1] * rows2:(items[n][1] + 1) * rows2, :]
              for i, n in enumerate(group)]
        if gi + 1 < len(groups):
            s_next = [item_scores(n) for n in groups[gi + 1]]
        es = []
        for i, n in enumerate(group):
            s, key_lo = ss[i], windows[items[n][0]][2]
            if key_lo is not None:
                s = jnp.where(kl >= key_lo, s, NEG_INF)
            es.append(jnp.exp(s - jnp.max(s, axis=-1, keepdims=True)).astype(BF16))
        for i, n in enumerate(group):
            j, p = items[n]
            o_ext = windows[j][1](lane_sl[p], es[i])
            o2 = o_ext[:, :LANES] * (1.0 / o_ext[:, LANES:])
            o_pair = _unstack_heads(o2, in_head)
            out_ref[row_sl[j], lane_sl[p]] = (
                o_pair * _silu(g_ref[row_sl[j], lane_sl[p]])).astype(BF16)


def _band_prompt_kernel(q_ref, g_ref, kt_ref, v_ref, tbl_ref, maskadd_ref, out_ref, bias_ref):
    b = pl.program_id(0)
    step = pl.program_id(1)
    n_prev = BAND_PAST // KEY_BLOCK
    n_blk = BAND_WINDOW // KEY_BLOCK

    @pl.when((b == 0) & (step == 0))
    def _():
        _build_band_bias(tbl_ref, maskadd_ref, bias_ref)

    def window(c, at_start):
        first = c - n_prev
        blocks = [max(first + t, 0) if at_start else first + t for t in range(n_blk)]

        def scores(sl, q2):
            kt = jnp.concatenate([kt_ref[blk, sl, :] for blk in blocks], axis=1)
            return _nn_dot(q2, kt)

        def weighted_values(sl, e16):
            if at_start:
                v = jnp.concatenate(
                    [v_ref[blk * KEY_BLOCK:(blk + 1) * KEY_BLOCK, sl] for blk in blocks], axis=0)
            else:
                v = v_ref[pl.ds(pl.multiple_of(first * KEY_BLOCK, KEY_BLOCK), BAND_WINDOW), sl]
            return _nn_dot(e16, jnp.concatenate([v, jnp.ones_like(v)], axis=1))

        return scores, weighted_values, (n_prev - c) * KEY_BLOCK if at_start else None

    @pl.when(step == 0)
    def _():
        _band_attend(q_ref, g_ref, bias_ref, out_ref, BAND_Q_ROWS,
                     [window(j, True) for j in range(BAND_STEP_BLOCKS)])

    @pl.when(step > 0)
    def _():
        _band_attend(q_ref, g_ref, bias_ref, out_ref, BAND_Q_ROWS,
                     [window(step * BAND_STEP_BLOCKS + j, False)
                      for j in range(BAND_STEP_BLOCKS)])


def _band_static_mask(q_pos, k_pos):
    qc = q_pos // CHUNK
    kc = k_pos // CHUNK
    ok = (kc[None, :] <= qc[:, None]) & (kc[None, :] >= qc[:, None] - N_PREV_CHUNKS)
    return np.where(ok, 0.0, NEG_INF).astype(np.float32)


def _pad_table(rel_bias):
    return jnp.pad(rel_bias, ((0, 0), (0, BIAS_TABLE_PAD - rel_bias.shape[1])))


def _band_prompt(q16, g, kt16, v16, rel_bias):
    bsz, seq, da = q16.shape
    n_heads = da // HEAD_DIM
    step_rows = BAND_STEP_BLOCKS * BAND_Q_ROWS
    assert BAND_Q_ROWS == KEY_BLOCK and seq % step_rows == 0
    assert BAND_STEP_BLOCKS == BAND_PAST // KEY_BLOCK
    q_pos = BAND_PAST + np.arange(BAND_Q_ROWS)
    k_pos = np.arange(BAND_WINDOW)
    maskadd = jnp.asarray(_band_static_mask(q_pos, k_pos))
    blk = lambda b, c: (b, c, 0)
    whole = lambda b, c: (b, 0, 0)
    whole4 = lambda b, c: (b, 0, 0, 0)
    const = lambda b, c: (0, 0)
    return pl.pallas_call(
        _band_prompt_kernel,
        out_shape=jax.ShapeDtypeStruct((bsz, seq, da), BF16),
        grid=(bsz, seq // step_rows),
        in_specs=[pl.BlockSpec((None, step_rows, da), blk),
                  pl.BlockSpec((None, step_rows, da), blk),
                  pl.BlockSpec((None, seq // KEY_BLOCK, da, KEY_BLOCK), whole4),
                  pl.BlockSpec((None, seq, da), whole),
                  pl.BlockSpec((n_heads, BIAS_TABLE_PAD), const),
                  pl.BlockSpec((BAND_Q_ROWS, BAND_WINDOW), const)],
        out_specs=pl.BlockSpec((None, step_rows, da), blk),
        scratch_shapes=[pltpu.VMEM((n_heads * BAND_Q_ROWS, BAND_WINDOW), F32)],
        compiler_params=_params(2),
        name="band_prompt",
    )(q16, g, kt16, v16, _pad_table(rel_bias), maskadd)


def _band_sample_kernel(q_ref, g_ref, ckt_ref, cvt_ref, nk_ref, nv_ref, tbl_ref, maskadd_ref,
                        out_ref, bias_ref):
    b = pl.program_id(0)
    la = ckt_ref.shape[1]

    @pl.when(b == 0)
    def _():
        _build_band_bias(tbl_ref, maskadd_ref, bias_ref)

    def scores(sl, q2):
        s_cache = _nn_dot(q2, ckt_ref[sl, :].astype(BF16))
        s_new = _nt_dot(q2, _pad_rows(nk_ref[:, sl], KEY_BLOCK))
        return jnp.concatenate([s_cache, s_new], axis=1)

    def weighted_values(sl, e16):
        cvt = cvt_ref[sl, :].astype(BF16)
        nv = _pad_rows(nv_ref[:, sl], KEY_BLOCK)
        return (_nt_dot(e16[:, :la], jnp.concatenate([cvt, jnp.ones_like(cvt)], axis=0))
                + _nn_dot(e16[:, la:], jnp.concatenate([nv, jnp.ones_like(nv)], axis=1)))

    _band_attend(q_ref, g_ref, bias_ref, out_ref, q_ref.shape[0],
                 [(scores, weighted_values, None)])


def _band_sample(q16, g, cache_kt, cache_vt, layer, nk16, nv16, rel_bias, past):
    bsz, t, da = q16.shape
    la = cache_kt.shape[3]
    n_heads = da // HEAD_DIM
    assert la == BAND_PAST and la + KEY_BLOCK == BAND_WINDOW and t <= KEY_BLOCK and t % 16 == 0
    q_pos = past + np.arange(t)
    k_pos = past - la + np.arange(BAND_WINDOW)
    maskadd = _band_static_mask(q_pos, k_pos)
    maskadd[:, la + t:] = NEG_INF
    maskadd = jnp.asarray(maskadd)
    blk = lambda b: (b, 0, 0)
    const = lambda b: (0, 0)
    return pl.pallas_call(
        _band_sample_kernel,
        out_shape=jax.ShapeDtypeStruct((bsz, t, da), BF16),
        grid=(bsz,),
        in_specs=[pl.BlockSpec((None, t, da), blk),
                  pl.BlockSpec((None, t, da), blk),
                  pl.BlockSpec((None, None, da, la), lambda b: (layer, b, 0, 0)),
                  pl.BlockSpec((None, None, da, la), lambda b: (layer, b, 0, 0)),
                  pl.BlockSpec((None, t, da), blk),
                  pl.BlockSpec((None, t, da), blk),
                  pl.BlockSpec((n_heads, BIAS_TABLE_PAD), const),
                  pl.BlockSpec((t, BAND_WINDOW), const)],
        out_specs=pl.BlockSpec((None, t, da), blk),
        scratch_shapes=[pltpu.VMEM((n_heads * t, BAND_WINDOW), F32)],
        compiler_params=_params(1),
        name="band_sample",
    )(q16, g, cache_kt, cache_vt, nk16, nv16, _pad_table(rel_bias), maskadd)


def _suffix_matrix(width):
    j = np.arange(width)[:, None]
    s = np.arange(width)[None, :]
    tri = (j >= s).astype(np.float32)
    return jnp.asarray(np.concatenate([tri, tri], axis=0), dtype=BF16)


def _sb_tiles(q2s, tiles, carry_ref, acc_ref, first, k_is_t, v_is_t):
    n = len(q2s)
    zs = [[(_nn_dot if k_is_t else _nt_dot)(q2s[i], ks[i]) for i in range(n)]
          for ks, _, _, _ in tiles]
    split = []
    for t, (_, _, _, causal) in enumerate(tiles):
        split.append([])
        for z in zs[t]:
            sp = _softplus(z)
            if causal is not None:
                sp = jnp.where(causal, sp, 0.0)
            hi = sp.astype(BF16)
            lo = (sp - hi.astype(F32)).astype(BF16)
            split[t].append(jnp.concatenate([hi, lo], axis=1))
    sums = [[_nn_dot(s, tri) for s in split[t]] for t, (_, _, tri, _) in enumerate(tiles)]
    carries = [None if first else carry_ref[i] for i in range(n)]
    ws = []
    for t, (_, _, _, causal) in enumerate(tiles):
        ws.append([])
        for i in range(n):
            log_w = zs[t][i] - sums[t][i]
            total = sums[t][i][:, 0:1]
            if carries[i] is not None:
                log_w = log_w - carries[i]
                total = total + carries[i]
            w = jnp.exp(log_w)
            if causal is not None:
                w = jnp.where(causal, w, 0.0)
            ws[t].append(w.astype(BF16))
            carries[i] = total
    for i in range(n):
        acc = None if first else acc_ref[i]
        for t, (_, vs, _, _) in enumerate(tiles):
            part = (_nt_dot if v_is_t else _nn_dot)(ws[t][i], vs[i])
            acc = part if acc is None else acc + part
        acc_ref[i] = acc
        carry_ref[i] = carries[i]
    return carries


def _sb_setup(q_ref, td):
    tq, width = q_ref.shape
    lane_sl = [slice(p * LANES, (p + 1) * LANES) for p in range(width // LANES)]
    in_head = _head_masks(tq)
    rows2 = HEADS_PER_TILE * tq
    row = lax.broadcasted_iota(jnp.int32, (rows2, td), 0) % tq
    col = lax.broadcasted_iota(jnp.int32, (rows2, td), 1)
    causal = col < row
    q2s = [_stack_heads(q_ref[:, sl], in_head) for sl in lane_sl]
    return lane_sl, in_head, causal, q2s


def _sb_min_carry(carries):
    m = carries[0]
    for c in carries[1:]:
        m = jnp.minimum(m, c)
    return jnp.min(m)


def _sb_store(acc_ref, lane_sl, in_head, g_ref, out_ref):
    for n, sl in enumerate(lane_sl):
        o_pair = _unstack_heads(acc_ref[n], in_head)
        out_ref[:, sl] = (o_pair * _silu(g_ref[:, sl])).astype(BF16)


def _sb_scratch(rows, width):
    n = width // LANES
    return [pltpu.VMEM((n, HEADS_PER_TILE * rows, 1), F32),
            pltpu.VMEM((n, HEADS_PER_TILE * rows, LANES), F32)]


def _softplus(z):
    return jnp.maximum(z, 0.0) + jnp.log(1.0 + jnp.exp2(jnp.abs(z) * NEG_LOG2E))


def _sb_window(q2s, kts, vs, tri, causal, valid, slots, carry_ref, acc_ref, first):
    n = len(q2s)
    n_blk = kts[0].shape[1] // KEY_BLOCK
    newest_first = list(reversed(range(n_blk)))
    zs = [_nn_dot(q2s[c], kts[c]) for c in range(n)]

    def block_mask(c, b):
        mask = causal if b == n_blk - 1 else None
        if valid[c] is not None:
            mask = valid[c] if mask is None else mask & valid[c]
        return mask

    split = []
    for c in range(n):
        split.append({})
        for b in newest_first:
            sp = _softplus(zs[c][:, b * KEY_BLOCK:(b + 1) * KEY_BLOCK])
            mask = block_mask(c, b)
            if mask is not None:
                sp = jnp.where(mask, sp, 0.0)
            hi = sp.astype(BF16)
            lo = (sp - hi.astype(F32)).astype(BF16)
            split[c][b] = jnp.concatenate([hi, lo], axis=1)
    sums = [{b: _nn_dot(split[c][b], tri) for b in newest_first} for c in range(n)]
    carries, ws = [], []
    for c in range(n):
        carry = None if first else carry_ref[slots[c]]
        pieces = [None] * n_blk
        for b in newest_first:
            log_w = zs[c][:, b * KEY_BLOCK:(b + 1) * KEY_BLOCK] - sums[c][b]
            total = sums[c][b][:, 0:1]
            if carry is not None:
                log_w = log_w - carry
                total = total + carry
            w = jnp.exp(log_w)
            mask = block_mask(c, b)
            if mask is not None:
                w = jnp.where(mask, w, 0.0)
            pieces[b] = w.astype(BF16)
            carry = total
        carries.append(carry)
        ws.append(pieces[0] if n_blk == 1 else jnp.concatenate(pieces, axis=1))
    for c in range(n):
        part = _nn_dot(ws[c], vs[c])
        acc_ref[slots[c]] = part if first else acc_ref[slots[c]] + part
        carry_ref[slots[c]] = carries[c]
    return carries


def _sb_prompt_kernel(q_ref, g_ref, kt_ref, v_ref, tri_ref, out_ref, carry_ref, acc_ref):
    step = pl.program_id(1)
    r, n_qb, n_win = SB_Q_ROWS, SB_STEP_BLOCKS, SB_WINDOW_BLOCKS
    lane_sl = [slice(p * LANES, (p + 1) * LANES) for p in range(q_ref.shape[1] // LANES)]
    n_lane = len(lane_sl)
    in_head = _head_masks(r)
    row = lax.broadcasted_iota(jnp.int32, (HEADS_PER_TILE * r, KEY_BLOCK), 0) % r
    col = lax.broadcasted_iota(jnp.int32, (HEADS_PER_TILE * r, KEY_BLOCK), 1)
    causal = col < row
    tri = tri_ref[...]
    chains = [(a, p) for a in range(n_qb) for p in range(n_lane)]
    q2s = [_stack_heads(q_ref[a * r:(a + 1) * r, lane_sl[p]], in_head) for a, p in chains]

    def keys(first_blk, n_blk, sl):
        return jnp.concatenate([kt_ref[first_blk + u, sl, :] for u in range(n_blk)], axis=1)

    def values(first_blk, n_blk, sl):
        if isinstance(first_blk, int):
            return v_ref[first_blk * KEY_BLOCK:(first_blk + n_blk) * KEY_BLOCK, sl]
        start = pl.multiple_of(first_blk * KEY_BLOCK, KEY_BLOCK)
        return v_ref[pl.ds(start, n_blk * KEY_BLOCK), sl]

    def run_windows(which, first_blk, n_blk):
        for a in sorted({chains[c][0] for c in which}):
            mine = [c for c in which if chains[c][0] == a]
            _sb_window([q2s[c] for c in mine],
                       [keys(first_blk(a), n_blk(a), lane_sl[chains[c][1]]) for c in mine],
                       [values(first_blk(a), n_blk(a), lane_sl[chains[c][1]]) for c in mine],
                       tri, causal, [None] * len(mine), mine, carry_ref, acc_ref, True)

    every = list(range(len(chains)))

    @pl.when(step == 0)
    def _():
        run_windows(every, lambda a: max(a + 1 - n_win, 0), lambda a: min(a + 1, n_win))

    @pl.when(step > 0)
    def _():
        _sb_window(q2s,
                   [keys(step * n_qb + a + 1 - n_win, n_win, lane_sl[p]) for a, p in chains],
                   [values(step * n_qb + a + 1 - n_win, n_win, lane_sl[p]) for a, p in chains],
                   tri, causal, [None] * len(chains), every, carry_ref, acc_ref, True)

    n_more = step * n_qb + n_qb - n_win

    def cond(loop_state):
        it, smallest = loop_state
        return (it < n_more) & (smallest < SB_UNDERFLOW)

    def body(loop_state):
        it, _ = loop_state
        blks = [step * n_qb + a - n_win - it for a in range(n_qb)]
        valid = [None if a == n_qb - 1 else blks[a] >= 0 for a in range(n_qb)]
        blks = [jnp.maximum(blk, 0) for blk in blks]
        carries = _sb_window(q2s,
                             [keys(blks[a], 1, lane_sl[p]) for a, p in chains],
                             [values(blks[a], 1, lane_sl[p]) for a, p in chains],
                             tri, None, [valid[a] for a, _ in chains], every,
                             carry_ref, acc_ref, False)
        return it + 1, _sb_min_carry(carries)

    lax.while_loop(cond, body,
                   (jnp.int32(0), _sb_min_carry([carry_ref[c] for c in every])))
    for c, (a, p) in enumerate(chains):
        rows = slice(a * r, (a + 1) * r)
        o_pair = _unstack_heads(acc_ref[c], in_head)
        out_ref[rows, lane_sl[p]] = (o_pair * _silu(g_ref[rows, lane_sl[p]])).astype(BF16)


def _sb_prompt(q16, g, kt16, v16):
    bsz, seq, db = q16.shape
    step_rows = SB_STEP_BLOCKS * SB_Q_ROWS
    n_chains = SB_STEP_BLOCKS * (db // LANES)
    assert SB_Q_ROWS == KEY_BLOCK and seq % step_rows == 0 and db % LANES == 0
    qblk = lambda b, i: (b, i, 0)
    whole = lambda b, i: (b, 0, 0)
    whole4 = lambda b, i: (b, 0, 0, 0)
    const = lambda b, i: (0, 0)
    return pl.pallas_call(
        _sb_prompt_kernel,
        out_shape=jax.ShapeDtypeStruct((bsz, seq, db), BF16),
        grid=(bsz, seq // step_rows),
        in_specs=[pl.BlockSpec((None, step_rows, db), qblk),
                  pl.BlockSpec((None, step_rows, db), qblk),
                  pl.BlockSpec((None, seq // KEY_BLOCK, db, KEY_BLOCK), whole4),
                  pl.BlockSpec((None, seq, db), whole),
                  pl.BlockSpec((2 * KEY_BLOCK, KEY_BLOCK), const)],
        out_specs=pl.BlockSpec((None, step_rows, db), qblk),
        scratch_shapes=[pltpu.VMEM((n_chains, HEADS_PER_TILE * SB_Q_ROWS, 1), F32),
                        pltpu.VMEM((n_chains, HEADS_PER_TILE * SB_Q_ROWS, LANES), F32)],
        compiler_params=_params(2),
        name="sb_prompt",
    )(q16, g, kt16, v16, _suffix_matrix(KEY_BLOCK))


def _sb_sample_kernel(q_ref, g_ref, nk_ref, nv_ref, ckt_ref, cvt_ref, trid_ref, trif_ref,
                      out_ref, carry_ref, acc_ref):
    td = trid_ref.shape[1]
    tk = trif_ref.shape[1]
    n_full = ckt_ref.shape[1] // tk
    trif = trif_ref[...]
    lane_sl, in_head, causal, q2s = _sb_setup(q_ref, td)

    carries = _sb_tiles(q2s, [([_pad_rows(nk_ref[:, sl], td) for sl in lane_sl],
                               [_pad_rows(nv_ref[:, sl], td) for sl in lane_sl],
                               trid_ref[...], causal)], carry_ref, acc_ref, True, False, False)
    smallest = _sb_min_carry(carries)

    for tile in reversed(range(n_full)):
        cols = slice(tile * tk, (tile + 1) * tk)

        def step(cols=cols):
            kts = [ckt_ref[sl, cols].astype(BF16) for sl in lane_sl]
            vts = [cvt_ref[sl, cols].astype(BF16) for sl in lane_sl]
            return _sb_min_carry(_sb_tiles(q2s, [(kts, vts, trif, None)], carry_ref, acc_ref,
                                           False, True, True))

        smallest = lax.cond(smallest < SB_UNDERFLOW, step, lambda s=smallest: s)
    _sb_store(acc_ref, lane_sl, in_head, g_ref, out_ref)


def _sb_sample(q16, g, nk16, nv16, cache_kt, cache_vt, layer):
    bsz, t, db = q16.shape
    past = cache_kt.shape[3]
    width, tk, td = SB_WIDTH, SB_TILE, KEY_BLOCK
    assert t <= td and t % 16 == 0 and past % tk == 0 and db % width == 0
    blk = lambda b, p: (b, 0, p)
    cblk = lambda b, p: (layer, b, p, 0)
    const = lambda b, p: (0, 0)
    return pl.pallas_call(
        _sb_sample_kernel,
        out_shape=jax.ShapeDtypeStruct((bsz, t, db), BF16),
        grid=(bsz, db // width),
        in_specs=[pl.BlockSpec((None, t, width), blk),
                  pl.BlockSpec((None, t, width), blk),
                  pl.BlockSpec((None, t, width), blk),
                  pl.BlockSpec((None, t, width), blk),
                  pl.BlockSpec((None, None, width, past), cblk),
                  pl.BlockSpec((None, None, width, past), cblk),
                  pl.BlockSpec((2 * td, td), const),
                  pl.BlockSpec((2 * tk, tk), const)],
        out_specs=pl.BlockSpec((None, t, width), blk),
        scratch_shapes=_sb_scratch(t, width),
        compiler_params=_params(2),
        name="sb_sample",
    )(q16, g, nk16, nv16, cache_kt, cache_vt, _suffix_matrix(td), _suffix_matrix(tk))


def _row_tile(t):
    return ROW_TILE if t % ROW_TILE == 0 else t


def _cache_transposed(cache):
    l, b, s, h, dh = cache.shape
    return jnp.transpose(cache, (0, 1, 3, 4, 2)).reshape(l, b, h * dh, s)


def _cache_layout(stacked, n_heads):
    l, b, hd, s = stacked.shape
    return jnp.transpose(stacked.reshape(l, b, n_heads, hd // n_heads, s), (0, 1, 4, 2, 3))


def kernel(x_prompt, x_sample, p_prompt, p_sample, cache_a_k, cache_a_v, cache_b_k, cache_b_v,
           g_pre, w_in, rel_bias, w_out, g_post, w_ple, w_ple_gate):
    depth = w_in.shape[0]
    bsz, seq, d = x_prompt.shape
    dbsz, dseq, _ = x_sample.shape
    past = cache_b_k.shape[2]
    keep = min(BAND_PAST, seq)
    da = w_in.shape[2] // 8
    n_heads = da // HEAD_DIM
    tp, ts = bsz * seq, dbsz * dseq

    hp = x_prompt
    hs = x_sample.reshape(ts, d)
    p_prompt = p_prompt.reshape(depth, tp, -1)
    p_sample = p_sample.reshape(depth, ts, -1)
    cache_a_kt, cache_a_vt, cache_b_kt, cache_b_vt = (
        _cache_transposed(c) for c in (cache_a_k, cache_a_v, cache_b_k, cache_b_v))
    stacked = ()
    sak, sav, sbk, sbv = [], [], [], []
    for i in range(depth):
        w16 = w_in[i].astype(BF16)
        grp = lambda n: w16[:, n * da:(n + 1) * da]
        wnn16 = jnp.concatenate([grp(0), grp(2), grp(3), grp(4), grp(7)], axis=1)
        wnt16 = jnp.concatenate([grp(1), grp(5), grp(6)], axis=1).T
        wo16 = w_out[i].astype(BF16)
        wg16 = w_ple_gate[i].astype(BF16)
        wp16 = w_ple[i].astype(BF16)

        (qa, va16, ga, qb, vb16, gb, kat16, kbt16, *stacked) = _inproj_prompt(
            hp, g_pre[i], wnn16, wnt16, tuple(stacked), keep)
        oa = _band_prompt(qa, ga, kat16, va16, rel_bias[i])
        ob = _sb_prompt(qb, gb, kbt16, vb16)
        hp = _finish(oa.reshape(tp, da), ob.reshape(tp, da), hp.reshape(tp, d),
                     p_prompt, i, wo16, g_post[i], wg16, wp16,
                     _row_tile(tp)).reshape(bsz, seq, d)

        (qa, ka, va, ga, qb, kb, vb, gb, ka16, va16, kb16, vb16) = _inproj(
            hs, g_pre[i], w16, _row_tile(ts))
        s3 = lambda a: a.reshape(dbsz, dseq, da)
        oa = _band_sample(s3(qa), s3(ga), cache_a_kt, cache_a_vt, i, s3(ka16), s3(va16),
                          rel_bias[i], past)
        ob = _sb_sample(s3(qb), s3(gb), s3(kb16), s3(vb16), cache_b_kt, cache_b_vt, i)
        hs = _finish(oa.reshape(ts, da), ob.reshape(ts, da), hs, p_sample, i,
                     wo16, g_post[i], wg16, wp16, _row_tile(ts))
        s4 = lambda a: a.reshape(dbsz, dseq, n_heads, HEAD_DIM)
        sak.append(s4(ka))
        sav.append(s4(va))
        sbk.append(s4(kb))
        sbv.append(s4(vb))

    kb_st, vb_st, ka_st, va_st = stacked
    return (hp, hs.reshape(dbsz, dseq, d),
            _cache_layout(ka_st, n_heads), _cache_layout(va_st, n_heads),
            _cache_layout(kb_st, n_heads), _cache_layout(vb_st, n_heads),
            jnp.stack(sak), jnp.stack(sav), jnp.stack(sbk), jnp.stack(sbv))
```

```python
import functools

import numpy as np
import jax
import jax.numpy as jnp
from jax import lax
from jax.experimental import pallas as pl
from jax.experimental.pallas import tpu as pltpu

HEAD_DIM = 64
CHUNK = 64
N_PREV_CHUNKS = 8
BAND_PAST = N_PREV_CHUNKS * CHUNK
REL_CLIP = 128
RMS_EPS = 1e-6
NEG_INF = -1e30
ATTN_SCALE = HEAD_DIM ** -0.5
NEG_LOG2E = -1.4426950408889634

LANES = 128
HEADS_PER_TILE = LANES // HEAD_DIM
VMEM_LIMIT_BYTES = 56 * 1024 * 1024

KEY_BLOCK = LANES
BAND_Q_ROWS = 2 * CHUNK
BAND_STEP_BLOCKS = 4
BAND_GROUP = 4
BAND_WINDOW = BAND_PAST + BAND_Q_ROWS
BIAS_TABLE_PAD = 384
BIAS_EXT = 768

SB_TILE = 256
SB_Q_ROWS = 128
SB_STEP_BLOCKS = 4
SB_WINDOW_BLOCKS = 3
SB_WIDTH = 512
SB_UNDERFLOW = 105.0

INPROJ_ROWS = 512
ROW_TILE = 1024

BF16 = jnp.bfloat16
F32 = jnp.float32


def _params(n_axes):
    return pltpu.CompilerParams(
        dimension_semantics=("arbitrary",) * n_axes,
        vmem_limit_bytes=VMEM_LIMIT_BYTES)


def _nt_dot(a, b):
    return lax.dot_general(a, b, (((1,), (1,)), ((), ())), preferred_element_type=F32)


def _nn_dot(a, b):
    return jnp.dot(a, b, preferred_element_type=F32)


def _silu(g):
    return g * (1.0 / (1.0 + jnp.exp(-g)))


def _rms_scale(x, gain):
    ms = jnp.mean(x * x, axis=-1, keepdims=True)
    return x * lax.rsqrt(ms + RMS_EPS) * gain


def _head_masks(m_rows):
    lane = lax.broadcasted_iota(jnp.int32, (m_rows, LANES), 1)
    return [(lane >= hh * HEAD_DIM) & (lane < (hh + 1) * HEAD_DIM)
            for hh in range(HEADS_PER_TILE)]


def _stack_heads(q_pair, in_head):
    zero = jnp.zeros_like(q_pair)
    return jnp.concatenate([jnp.where(m, q_pair, zero) for m in in_head], axis=0)


def _unstack_heads(o2, in_head):
    m_rows = o2.shape[0] // HEADS_PER_TILE
    o = o2[0:m_rows]
    for hh in range(1, HEADS_PER_TILE):
        o = jnp.where(in_head[hh], o2[hh * m_rows:(hh + 1) * m_rows], o)
    return o


def _pad_rows(a, rows):
    return jnp.concatenate([a, jnp.zeros((rows - a.shape[0], a.shape[1]), a.dtype)], axis=0)


def _inproj_kernel(x_ref, g_ref, w_ref,
                   qa_ref, ka_ref, va_ref, ga_ref, qb_ref, kb_ref, vb_ref, gb_ref,
                   ka16_ref, va16_ref, kb16_ref, vb16_ref):
    hn = _rms_scale(x_ref[...], g_ref[...]).astype(BF16)
    gw = qa_ref.shape[-1]

    def proj(n):
        return _nn_dot(hn, w_ref[:, n * gw:(n + 1) * gw])

    qa_ref[...] = (proj(0) * ATTN_SCALE).astype(BF16)
    u = proj(1)
    ka_ref[...] = u
    ka16_ref[...] = u.astype(BF16)
    u = proj(2)
    va_ref[...] = u
    va16_ref[...] = u.astype(BF16)
    ga_ref[...] = proj(3)
    qb_ref[...] = (proj(4) * ATTN_SCALE).astype(BF16)
    u = proj(5)
    kb_ref[...] = u
    kb16_ref[...] = u.astype(BF16)
    u = proj(6)
    vb_ref[...] = u
    vb16_ref[...] = u.astype(BF16)
    gb_ref[...] = proj(7)


def _inproj(x2d, g_pre, w16, tm):
    t, d = x2d.shape
    d_in = w16.shape[1]
    gw = d_in // 8
    row = lambda i: (i, 0)
    const = lambda i: (0, 0)
    f32_out = jax.ShapeDtypeStruct((t, gw), F32)
    b16_out = jax.ShapeDtypeStruct((t, gw), BF16)
    out_shape = (b16_out, f32_out, f32_out, f32_out, b16_out, f32_out, f32_out, f32_out,
                 b16_out, b16_out, b16_out, b16_out)
    return pl.pallas_call(
        _inproj_kernel,
        out_shape=out_shape,
        grid=(t // tm,),
        in_specs=[pl.BlockSpec((tm, d), row),
                  pl.BlockSpec((1, d), const),
                  pl.BlockSpec((d, d_in), const)],
        out_specs=tuple(pl.BlockSpec((tm, gw), row) for _ in out_shape),
        compiler_params=_params(1),
        name="inproj_sample",
    )(x2d, g_pre.reshape(1, d), w16)


def _inproj_prompt_kernel(*refs, n_prev, keep_first):
    x_ref, g_ref, wnn_ref, wnt_ref = refs[:4]
    prev = refs[4:4 + (4 if n_prev else 0)]
    (qa_ref, va16_ref, ga_ref, qb_ref, vb16_ref, gb_ref, kat16_ref, kbt16_ref,
     kb_st_ref, vb_st_ref, ka_st_ref, va_st_ref) = refs[4 + len(prev):]
    j = pl.program_id(1)
    gw = qa_ref.shape[-1]
    hn = _rms_scale(x_ref[...], g_ref[...]).astype(BF16)

    def proj(n):
        return _nn_dot(hn, wnn_ref[:, n * gw:(n + 1) * gw])

    nt = _nt_dot(wnt_ref[...], hn)
    kat, kbt, vbt = nt[0:gw], nt[gw:2 * gw], nt[2 * gw:3 * gw]
    vb16_ref[...] = vbt.T.astype(BF16)
    for u in range(kat16_ref.shape[0]):
        blk = slice(u * KEY_BLOCK, (u + 1) * KEY_BLOCK)
        kat16_ref[u] = kat[:, blk].astype(BF16)
        kbt16_ref[u] = kbt[:, blk].astype(BF16)
    for dpt in range(n_prev):
        kb_st_ref[dpt] = prev[0][dpt]
        vb_st_ref[dpt] = prev[1][dpt]
    kb_st_ref[n_prev] = kbt
    vb_st_ref[n_prev] = vbt

    qa_ref[...] = (proj(0) * ATTN_SCALE).astype(BF16)
    va = proj(1)
    va16_ref[...] = va.astype(BF16)
    ga_ref[...] = proj(2)
    qb_ref[...] = (proj(3) * ATTN_SCALE).astype(BF16)
    gb_ref[...] = proj(4)

    @pl.when(j >= keep_first)
    def _():
        for dpt in range(n_prev):
            ka_st_ref[dpt] = prev[2][dpt]
            va_st_ref[dpt] = prev[3][dpt]
        ka_st_ref[n_prev] = kat
        va_st_ref[n_prev] = va.T


def _inproj_prompt(x3d, g_pre, wnn16, wnt16, prev, keep):
    bsz, seq, d = x3d.shape
    tm = INPROJ_ROWS
    gw = wnt16.shape[0] // 3
    n_prev = prev[0].shape[0] if prev else 0
    assert seq % tm == 0 and keep % tm == 0 and tm % KEY_BLOCK == 0
    keep_first = (seq - keep) // tm
    tok = lambda b, j: (b, j, 0)
    const = lambda b, j: (0, 0)
    ktb = lambda b, j: (b, j, 0, 0)
    st_all = lambda b, j: (0, b, 0, j)
    st_keep = lambda b, j: (0, b, 0, jnp.maximum(j - keep_first, 0))
    tok_b16 = jax.ShapeDtypeStruct((bsz, seq, gw), BF16)
    tok_f32 = jax.ShapeDtypeStruct((bsz, seq, gw), F32)
    kt_b16 = jax.ShapeDtypeStruct((bsz, seq // KEY_BLOCK, gw, KEY_BLOCK), BF16)
    st_f32 = jax.ShapeDtypeStruct((n_prev + 1, bsz, gw, seq), F32)
    keep_f32 = jax.ShapeDtypeStruct((n_prev + 1, bsz, gw, keep), F32)
    out_shape = (tok_b16, tok_b16, tok_f32, tok_b16, tok_b16, tok_f32, kt_b16, kt_b16,
                 st_f32, st_f32, keep_f32, keep_f32)
    tok_spec = pl.BlockSpec((None, tm, gw), tok)
    kt_spec = pl.BlockSpec((None, tm // KEY_BLOCK, gw, KEY_BLOCK), ktb)
    out_specs = (tok_spec,) * 6 + (kt_spec,) * 2 + (
        pl.BlockSpec((n_prev + 1, None, gw, tm), st_all),
        pl.BlockSpec((n_prev + 1, None, gw, tm), st_all),
        pl.BlockSpec((n_prev + 1, None, gw, tm), st_keep),
        pl.BlockSpec((n_prev + 1, None, gw, tm), st_keep))
    in_specs = [pl.BlockSpec((None, tm, d), tok),
                pl.BlockSpec((1, d), const),
                pl.BlockSpec(wnn16.shape, const, pipeline_mode=pl.Buffered(1)),
                pl.BlockSpec(wnt16.shape, const, pipeline_mode=pl.Buffered(1))]
    if n_prev:
        in_specs += [pl.BlockSpec((n_prev, None, gw, tm), st_all),
                     pl.BlockSpec((n_prev, None, gw, tm), st_all),
                     pl.BlockSpec((n_prev, None, gw, tm), st_keep),
                     pl.BlockSpec((n_prev, None, gw, tm), st_keep)]
    return pl.pallas_call(
        functools.partial(_inproj_prompt_kernel, n_prev=n_prev, keep_first=keep_first),
        out_shape=out_shape,
        grid=(bsz, seq // tm),
        in_specs=in_specs,
        out_specs=out_specs,
        compiler_params=_params(2),
        name="inproj_prompt",
    )(x3d, g_pre.reshape(1, d), wnn16, wnt16, *prev)


def _finish_kernel(oa_ref, ob_ref, h_ref, p_ref, woa_ref, wob_ref, gpost_ref, wg_ref, wp_ref,
                   out_ref):
    y = _nn_dot(oa_ref[...], woa_ref[...]) + _nn_dot(ob_ref[...], wob_ref[...])
    h1 = h_ref[...] + _rms_scale(y, gpost_ref[...])
    gate_logit = _nn_dot(h1.astype(BF16), wg_ref[...])
    gate = 1.0 / (1.0 + jnp.exp(-gate_logit))
    emb = _nn_dot(p_ref[...].astype(BF16), wp_ref[...])
    out_ref[...] = h1 + gate * emb


def _finish(oa, ob, h2d, p_all, layer, wo16, g_post, wg16, wp16, tm):
    t, d = h2d.shape
    da = oa.shape[1]
    dp = p_all.shape[2]
    row = lambda i: (i, 0)
    const = lambda i: (0, 0)
    return pl.pallas_call(
        _finish_kernel,
        out_shape=jax.ShapeDtypeStruct((t, d), F32),
        grid=(t // tm,),
        in_specs=[pl.BlockSpec((tm, da), row),
                  pl.BlockSpec((tm, da), row),
                  pl.BlockSpec((tm, d), row),
                  pl.BlockSpec((None, tm, dp), lambda i: (layer, i, 0)),
                  pl.BlockSpec((da, d), const),
                  pl.BlockSpec((da, d), lambda i: (1, 0)),
                  pl.BlockSpec((1, d), const),
                  pl.BlockSpec((d, d), const),
                  pl.BlockSpec((dp, d), const)],
        out_specs=pl.BlockSpec((tm, d), row),
        compiler_params=_params(1),
        name="finish",
    )(oa, ob, h2d, p_all, wo16, wo16, g_post.reshape(1, d), wg16, wp16)


def _build_band_bias(tbl_ref, maskadd_ref, bias_ref):
    m_rows = maskadd_ref.shape[0]
    n_heads = bias_ref.shape[0] // m_rows
    tbl = tbl_ref[...]
    p1 = tbl.astype(BF16)
    r1 = tbl - p1.astype(F32)
    p2 = r1.astype(BF16)
    p3 = (r1 - p2.astype(F32)).astype(BF16)
    r_idx = lax.broadcasted_iota(jnp.int32, (BIAS_TABLE_PAD, BIAS_EXT), 0)
    m_idx = lax.broadcasted_iota(jnp.int32, (BIAS_TABLE_PAD, BIAS_EXT), 1)
    rel = jnp.clip(m_idx - (m_rows - 1) - BAND_PAST, -REL_CLIP, REL_CLIP) + REL_CLIP
    onehot = (r_idx == rel).astype(BF16)
    t_ext = _nn_dot(p1, onehot) + _nn_dot(p2, onehot) + _nn_dot(p3, onehot)
    ql = lax.broadcasted_iota(jnp.int32, (m_rows, BIAS_EXT), 0)
    n_bits = int(m_rows - 1).bit_length()
    for h in range(n_heads):
        x = jnp.broadcast_to(t_ext[h:h + 1, :], (m_rows, BIAS_EXT))
        x = pltpu.roll(x, BIAS_EXT - (m_rows - 1), 1)
        for b in range(n_bits):
            x = jnp.where(((ql >> b) & 1) == 1, pltpu.roll(x, 1 << b, 1), x)
        bias_ref[h * m_rows:(h + 1) * m_rows, :] = x[:, :BAND_WINDOW] + maskadd_ref[...]


def _band_attend(q_ref, g_ref, bias_ref, out_ref, m_rows, windows):
    n_tiles = q_ref.shape[1] // LANES
    in_head = _head_masks(m_rows)
    kl = lax.broadcasted_iota(jnp.int32, (1, BAND_WINDOW), 1)
    lane_sl = [slice(p * LANES, (p + 1) * LANES) for p in range(n_tiles)]
    row_sl = [slice(j * m_rows, (j + 1) * m_rows) for j in range(len(windows))]
    rows2 = HEADS_PER_TILE * m_rows
    items = [(j, p) for j in range(len(windows)) for p in range(n_tiles)]

    def item_scores(n):
        j, p = items[n]
        return windows[j][0](lane_sl[p], _stack_heads(q_ref[row_sl[j], lane_sl[p]], in_head))

    groups = [list(range(a, min(a + BAND_GROUP, len(items))))
              for a in range(0, len(items), BAND_GROUP)]
    s_next = [item_scores(n) for n in groups[0]]
    for gi, group in enumerate(groups):
        ss = [s_next[i] + bias_ref[items[n][1] * rows2:(items[n][1] + 1) * rows2, :]
              for i, n in enumerate(group)]
        if gi + 1 < len(groups):
            s_next = [item_scores(n) for n in groups[gi + 1]]
        es = []
        for i, n in enumerate(group):
            s, key_lo = ss[i], windows[items[n][0]][2]
            if key_lo is not None:
                s = jnp.where(kl >= key_lo, s, NEG_INF)
            es.append(jnp.exp(s - jnp.max(s, axis=-1, keepdims=True)).astype(BF16))
        for i, n in enumerate(group):
            j, p = items[n]
            o_ext = windows[j][1](lane_sl[p], es[i])
            o2 = o_ext[:, :LANES] * (1.0 / o_ext[:, LANES:])
            o_pair = _unstack_heads(o2, in_head)
            out_ref[row_sl[j], lane_sl[p]] = (
                o_pair * _silu(g_ref[row_sl[j], lane_sl[p]])).astype(BF16)


def _band_prompt_kernel(q_ref, g_ref, kt_ref, v_ref, tbl_ref, maskadd_ref, out_ref, bias_ref):
    b = pl.program_id(0)
    step = pl.program_id(1)
    n_prev = BAND_PAST // KEY_BLOCK
    n_blk = BAND_WINDOW // KEY_BLOCK

    @pl.when((b == 0) & (step == 0))
    def _():
        _build_band_bias(tbl_ref, maskadd_ref, bias_ref)

    def window(c, at_start):
        first = c - n_prev
        blocks = [max(first + t, 0) if at_start else first + t for t in range(n_blk)]

        def scores(sl, q2):
            kt = jnp.concatenate([kt_ref[blk, sl, :] for blk in blocks], axis=1)
            return _nn_dot(q2, kt)

        def weighted_values(sl, e16):
            if at_start:
                v = jnp.concatenate(
                    [v_ref[blk * KEY_BLOCK:(blk + 1) * KEY_BLOCK, sl] for blk in blocks], axis=0)
            else:
                v = v_ref[pl.ds(pl.multiple_of(first * KEY_BLOCK, KEY_BLOCK), BAND_WINDOW), sl]
            return _nn_dot(e16, jnp.concatenate([v, jnp.ones_like(v)], axis=1))

        return scores, weighted_values, (n_prev - c) * KEY_BLOCK if at_start else None

    @pl.when(step == 0)
    def _():
        _band_attend(q_ref, g_ref, bias_ref, out_ref, BAND_Q_ROWS,
                     [window(j, True) for j in range(BAND_STEP_BLOCKS)])

    @pl.when(step > 0)
    def _():
        _band_attend(q_ref, g_ref, bias_ref, out_ref, BAND_Q_ROWS,
                     [window(step * BAND_STEP_BLOCKS + j, False)
                      for j in range(BAND_STEP_BLOCKS)])


def _band_static_mask(q_pos, k_pos):
    qc = q_pos // CHUNK
    kc = k_pos // CHUNK
    ok = (kc[None, :] <= qc[:, None]) & (kc[None, :] >= qc[:, None] - N_PREV_CHUNKS)
    return np.where(ok, 0.0, NEG_INF).astype(np.float32)


def _pad_table(rel_bias):
    return jnp.pad(rel_bias, ((0, 0), (0, BIAS_TABLE_PAD - rel_bias.shape[1])))


def _band_prompt(q16, g, kt16, v16, rel_bias):
    bsz, seq, da = q16.shape
    n_heads = da // HEAD_DIM
    step_rows = BAND_STEP_BLOCKS * BAND_Q_ROWS
    assert BAND_Q_ROWS == KEY_BLOCK and seq % step_rows == 0
    assert BAND_STEP_BLOCKS == BAND_PAST // KEY_BLOCK
    q_pos = BAND_PAST + np.arange(BAND_Q_ROWS)
    k_pos = np.arange(BAND_WINDOW)
    maskadd = jnp.asarray(_band_static_mask(q_pos, k_pos))
    blk = lambda b, c: (b, c, 0)
    whole = lambda b, c: (b, 0, 0)
    whole4 = lambda b, c: (b, 0, 0, 0)
    const = lambda b, c: (0, 0)
    return pl.pallas_call(
        _band_prompt_kernel,
        out_shape=jax.ShapeDtypeStruct((bsz, seq, da), BF16),
        grid=(bsz, seq // step_rows),
        in_specs=[pl.BlockSpec((None, step_rows, da), blk),
                  pl.BlockSpec((None, step_rows, da), blk),
                  pl.BlockSpec((None, seq // KEY_BLOCK, da, KEY_BLOCK), whole4),
                  pl.BlockSpec((None, seq, da), whole),
                  pl.BlockSpec((n_heads, BIAS_TABLE_PAD), const),
                  pl.BlockSpec((BAND_Q_ROWS, BAND_WINDOW), const)],
        out_specs=pl.BlockSpec((None, step_rows, da), blk),
        scratch_shapes=[pltpu.VMEM((n_heads * BAND_Q_ROWS, BAND_WINDOW), F32)],
        compiler_params=_params(2),
        name="band_prompt",
    )(q16, g, kt16, v16, _pad_table(rel_bias), maskadd)


def _band_sample_kernel(q_ref, g_ref, ckt_ref, cvt_ref, nk_ref, nv_ref, tbl_ref, maskadd_ref,
                        out_ref, bias_ref):
    b = pl.program_id(0)
    la = ckt_ref.shape[1]

    @pl.when(b == 0)
    def _():
        _build_band_bias(tbl_ref, maskadd_ref, bias_ref)

    def scores(sl, q2):
        s_cache = _nn_dot(q2, ckt_ref[sl, :].astype(BF16))
        s_new = _nt_dot(q2, _pad_rows(nk_ref[:, sl], KEY_BLOCK))
        return jnp.concatenate([s_cache, s_new], axis=1)

    def weighted_values(sl, e16):
        cvt = cvt_ref[sl, :].astype(BF16)
        nv = _pad_rows(nv_ref[:, sl], KEY_BLOCK)
        return (_nt_dot(e16[:, :la], jnp.concatenate([cvt, jnp.ones_like(cvt)], axis=0))
                + _nn_dot(e16[:, la:], jnp.concatenate([nv, jnp.ones_like(nv)], axis=1)))

    _band_attend(q_ref, g_ref, bias_ref, out_ref, q_ref.shape[0],
                 [(scores, weighted_values, None)])


def _band_sample(q16, g, cache_kt, cache_vt, layer, nk16, nv16, rel_bias, past):
    bsz, t, da = q16.shape
    la = cache_kt.shape[3]
    n_heads = da // HEAD_DIM
    assert la == BAND_PAST and la + KEY_BLOCK == BAND_WINDOW and t <= KEY_BLOCK and t % 16 == 0
    q_pos = past + np.arange(t)
    k_pos = past - la + np.arange(BAND_WINDOW)
    maskadd = _band_static_mask(q_pos, k_pos)
    maskadd[:, la + t:] = NEG_INF
    maskadd = jnp.asarray(maskadd)
    blk = lambda b: (b, 0, 0)
    const = lambda b: (0, 0)
    return pl.pallas_call(
        _band_sample_kernel,
        out_shape=jax.ShapeDtypeStruct((bsz, t, da), BF16),
        grid=(bsz,),
        in_specs=[pl.BlockSpec((None, t, da), blk),
                  pl.BlockSpec((None, t, da), blk),
                  pl.BlockSpec((None, None, da, la), lambda b: (layer, b, 0, 0)),
                  pl.BlockSpec((None, None, da, la), lambda b: (layer, b, 0, 0)),
                  pl.BlockSpec((None, t, da), blk),
                  pl.BlockSpec((None, t, da), blk),
                  pl.BlockSpec((n_heads, BIAS_TABLE_PAD), const),
                  pl.BlockSpec((t, BAND_WINDOW), const)],
        out_specs=pl.BlockSpec((None, t, da), blk),
        scratch_shapes=[pltpu.VMEM((n_heads * t, BAND_WINDOW), F32)],
        compiler_params=_params(1),
        name="band_sample",
    )(q16, g, cache_kt, cache_vt, nk16, nv16, _pad_table(rel_bias), maskadd)


def _suffix_matrix(width):
    j = np.arange(width)[:, None]
    s = np.arange(width)[None, :]
    tri = (j >= s).astype(np.float32)
    return jnp.asarray(np.concatenate([tri, tri], axis=0), dtype=BF16)


def _sb_tiles(q2s, tiles, carry_ref, acc_ref, first, k_is_t, v_is_t):
    n = len(q2s)
    zs = [[(_nn_dot if k_is_t else _nt_dot)(q2s[i], ks[i]) for i in range(n)]
          for ks, _, _, _ in tiles]
    split = []
    for t, (_, _, _, causal) in enumerate(tiles):
        split.append([])
        for z in zs[t]:
            sp = _softplus(z)
            if causal is not None:
                sp = jnp.where(causal, sp, 0.0)
            hi = sp.astype(BF16)
            lo = (sp - hi.astype(F32)).astype(BF16)
            split[t].append(jnp.concatenate([hi, lo], axis=1))
    sums = [[_nn_dot(s, tri) for s in split[t]] for t, (_, _, tri, _) in enumerate(tiles)]
    carries = [None if first else carry_ref[i] for i in range(n)]
    ws = []
    for t, (_, _, _, causal) in enumerate(tiles):
        ws.append([])
        for i in range(n):
            log_w = zs[t][i] - sums[t][i]
            total = sums[t][i][:, 0:1]
            if carries[i] is not None:
                log_w = log_w - carries[i]
                total = total + carries[i]
            w = jnp.exp(log_w)
            if causal is not None:
                w = jnp.where(causal, w, 0.0)
            ws[t].append(w.astype(BF16))
            carries[i] = total
    for i in range(n):
        acc = None if first else acc_ref[i]
        for t, (_, vs, _, _) in enumerate(tiles):
            part = (_nt_dot if v_is_t else _nn_dot)(ws[t][i], vs[i])
            acc = part if acc is None else acc + part
        acc_ref[i] = acc
        carry_ref[i] = carries[i]
    return carries


def _sb_setup(q_ref, td):
    tq, width = q_ref.shape
    lane_sl = [slice(p * LANES, (p + 1) * LANES) for p in range(width // LANES)]
    in_head = _head_masks(tq)
    rows2 = HEADS_PER_TILE * tq
    row = lax.broadcasted_iota(jnp.int32, (rows2, td), 0) % tq
    col = lax.broadcasted_iota(jnp.int32, (rows2, td), 1)
    causal = col < row
    q2s = [_stack_heads(q_ref[:, sl], in_head) for sl in lane_sl]
    return lane_sl, in_head, causal, q2s


def _sb_min_carry(carries):
    m = carries[0]
    for c in carries[1:]:
        m = jnp.minimum(m, c)
    return jnp.min(m)


def _sb_store(acc_ref, lane_sl, in_head, g_ref, out_ref):
    for n, sl in enumerate(lane_sl):
        o_pair = _unstack_heads(acc_ref[n], in_head)
        out_ref[:, sl] = (o_pair * _silu(g_ref[:, sl])).astype(BF16)


def _sb_scratch(rows, width):
    n = width // LANES
    return [pltpu.VMEM((n, HEADS_PER_TILE * rows, 1), F32),
            pltpu.VMEM((n, HEADS_PER_TILE * rows, LANES), F32)]


def _softplus(z):
    return jnp.maximum(z, 0.0) + jnp.log(1.0 + jnp.exp2(jnp.abs(z) * NEG_LOG2E))


def _sb_window(q2s, kts, vs, tri, causal, valid, slots, carry_ref, acc_ref, first):
    n = len(q2s)
    n_blk = kts[0].shape[1] // KEY_BLOCK
    newest_first = list(reversed(range(n_blk)))
    zs = [_nn_dot(q2s[c], kts[c]) for c in range(n)]

    def block_mask(c, b):
        mask = causal if b == n_blk - 1 else None
        if valid[c] is not None:
            mask = valid[c] if mask is None else mask & valid[c]
        return mask

    split = []
    for c in range(n):
        split.append({})
        for b in newest_first:
            sp = _softplus(zs[c][:, b * KEY_BLOCK:(b + 1) * KEY_BLOCK])
            mask = block_mask(c, b)
            if mask is not None:
                sp = jnp.where(mask, sp, 0.0)
            hi = sp.astype(BF16)
            lo = (sp - hi.astype(F32)).astype(BF16)
            split[c][b] = jnp.concatenate([hi, lo], axis=1)
    sums = [{b: _nn_dot(split[c][b], tri) for b in newest_first} for c in range(n)]
    carries, ws = [], []
    for c in range(n):
        carry = None if first else carry_ref[slots[c]]
        pieces = [None] * n_blk
        for b in newest_first:
            log_w = zs[c][:, b * KEY_BLOCK:(b + 1) * KEY_BLOCK] - sums[c][b]
            total = sums[c][b][:, 0:1]
            if carry is not None:
                log_w = log_w - carry
                total = total + carry
            w = jnp.exp(log_w)
            mask = block_mask(c, b)
            if mask is not None:
                w = jnp.where(mask, w, 0.0)
            pieces[b] = w.astype(BF16)
            carry = total
        carries.append(carry)
        ws.append(pieces[0] if n_blk == 1 else jnp.concatenate(pieces, axis=1))
    for c in range(n):
        part = _nn_dot(ws[c], vs[c])
        acc_ref[slots[c]] = part if first else acc_ref[slots[c]] + part
        carry_ref[slots[c]] = carries[c]
    return carries


def _sb_prompt_kernel(q_ref, g_ref, kt_ref, v_ref, tri_ref, out_ref, carry_ref, acc_ref):
    step = pl.program_id(1)
    r, n_qb, n_win = SB_Q_ROWS, SB_STEP_BLOCKS, SB_WINDOW_BLOCKS
    lane_sl = [slice(p * LANES, (p + 1) * LANES) for p in range(q_ref.shape[1] // LANES)]
    n_lane = len(lane_sl)
    in_head = _head_masks(r)
    row = lax.broadcasted_iota(jnp.int32, (HEADS_PER_TILE * r, KEY_BLOCK), 0) % r
    col = lax.broadcasted_iota(jnp.int32, (HEADS_PER_TILE * r, KEY_BLOCK), 1)
    causal = col < row
    tri = tri_ref[...]
    chains = [(a, p) for a in range(n_qb) for p in range(n_lane)]
    q2s = [_stack_heads(q_ref[a * r:(a + 1) * r, lane_sl[p]], in_head) for a, p in chains]

    def keys(first_blk, n_blk, sl):
        return jnp.concatenate([kt_ref[first_blk + u, sl, :] for u in range(n_blk)], axis=1)

    def values(first_blk, n_blk, sl):
        if isinstance(first_blk, int):
            return v_ref[first_blk * KEY_BLOCK:(first_blk + n_blk) * KEY_BLOCK, sl]
        start = pl.multiple_of(first_blk * KEY_BLOCK, KEY_BLOCK)
        return v_ref[pl.ds(start, n_blk * KEY_BLOCK), sl]

    def run_windows(which, first_blk, n_blk):
        for a in sorted({chains[c][0] for c in which}):
            mine = [c for c in which if chains[c][0] == a]
            _sb_window([q2s[c] for c in mine],
                       [keys(first_blk(a), n_blk(a), lane_sl[chains[c][1]]) for c in mine],
                       [values(first_blk(a), n_blk(a), lane_sl[chains[c][1]]) for c in mine],
                       tri, causal, [None] * len(mine), mine, carry_ref, acc_ref, True)

    every = list(range(len(chains)))

    @pl.when(step == 0)
    def _():
        run_windows(every, lambda a: max(a + 1 - n_win, 0), lambda a: min(a + 1, n_win))

    @pl.when(step > 0)
    def _():
        _sb_window(q2s,
                   [keys(step * n_qb + a + 1 - n_win, n_win, lane_sl[p]) for a, p in chains],
                   [values(step * n_qb + a + 1 - n_win, n_win, lane_sl[p]) for a, p in chains],
                   tri, causal, [None] * len(chains), every, carry_ref, acc_ref, True)

    n_more = step * n_qb + n_qb - n_win

    def cond(loop_state):
        it, smallest = loop_state
        return (it < n_more) & (smallest < SB_UNDERFLOW)

    def body(loop_state):
        it, _ = loop_state
        blks = [step * n_qb + a - n_win - it for a in range(n_qb)]
        valid = [None if a == n_qb - 1 else blks[a] >= 0 for a in range(n_qb)]
        blks = [jnp.maximum(blk, 0) for blk in blks]
        carries = _sb_window(q2s,
                             [keys(blks[a], 1, lane_sl[p]) for a, p in chains],
                             [values(blks[a], 1, lane_sl[p]) for a, p in chains],
                             tri, None, [valid[a] for a, _ in chains], every,
                             carry_ref, acc_ref, False)
        return it + 1, _sb_min_carry(carries)

    lax.while_loop(cond, body,
                   (jnp.int32(0), _sb_min_carry([carry_ref[c] for c in every])))
    for c, (a, p) in enumerate(chains):
        rows = slice(a * r, (a + 1) * r)
        o_pair = _unstack_heads(acc_ref[c], in_head)
        out_ref[rows, lane_sl[p]] = (o_pair * _silu(g_ref[rows, lane_sl[p]])).astype(BF16)


def _sb_prompt(q16, g, kt16, v16):
    bsz, seq, db = q16.shape
    step_rows = SB_STEP_BLOCKS * SB_Q_ROWS
    n_chains = SB_STEP_BLOCKS * (db // LANES)
    assert SB_Q_ROWS == KEY_BLOCK and seq % step_rows == 0 and db % LANES == 0
    qblk = lambda b, i: (b, i, 0)
    whole = lambda b, i: (b, 0, 0)
    whole4 = lambda b, i: (b, 0, 0, 0)
    const = lambda b, i: (0, 0)
    return pl.pallas_call(
        _sb_prompt_kernel,
        out_shape=jax.ShapeDtypeStruct((bsz, seq, db), BF16),
        grid=(bsz, seq // step_rows),
        in_specs=[pl.BlockSpec((None, step_rows, db), qblk),
                  pl.BlockSpec((None, step_rows, db), qblk),
                  pl.BlockSpec((None, seq // KEY_BLOCK, db, KEY_BLOCK), whole4),
                  pl.BlockSpec((None, seq, db), whole),
                  pl.BlockSpec((2 * KEY_BLOCK, KEY_BLOCK), const)],
        out_specs=pl.BlockSpec((None, step_rows, db), qblk),
        scratch_shapes=[pltpu.VMEM((n_chains, HEADS_PER_TILE * SB_Q_ROWS, 1), F32),
                        pltpu.VMEM((n_chains, HEADS_PER_TILE * SB_Q_ROWS, LANES), F32)],
        compiler_params=_params(2),
        name="sb_prompt",
    )(q16, g, kt16, v16, _suffix_matrix(KEY_BLOCK))


def _sb_sample_kernel(q_ref, g_ref, nk_ref, nv_ref, ckt_ref, cvt_ref, trid_ref, trif_ref,
                      out_ref, carry_ref, acc_ref):
    td = trid_ref.shape[1]
    tk = trif_ref.shape[1]
    n_full = ckt_ref.shape[1] // tk
    trif = trif_ref[...]
    lane_sl, in_head, causal, q2s = _sb_setup(q_ref, td)

    carries = _sb_tiles(q2s, [([_pad_rows(nk_ref[:, sl], td) for sl in lane_sl],
                               [_pad_rows(nv_ref[:, sl], td) for sl in lane_sl],
                               trid_ref[...], causal)], carry_ref, acc_ref, True, False, False)
    smallest = _sb_min_carry(carries)

    for tile in reversed(range(n_full)):
        cols = slice(tile * tk, (tile + 1) * tk)

        def step(cols=cols):
            kts = [ckt_ref[sl, cols].astype(BF16) for sl in lane_sl]
            vts = [cvt_ref[sl, cols].astype(BF16) for sl in lane_sl]
            return _sb_min_carry(_sb_tiles(q2s, [(kts, vts, trif, None)], carry_ref, acc_ref,
                                           False, True, True))

        smallest = lax.cond(smallest < SB_UNDERFLOW, step, lambda s=smallest: s)
    _sb_store(acc_ref, lane_sl, in_head, g_ref, out_ref)


def _sb_sample(q16, g, nk16, nv16, cache_kt, cache_vt, layer):
    bsz, t, db = q16.shape
    past = cache_kt.shape[3]
    width, tk, td = SB_WIDTH, SB_TILE, KEY_BLOCK
    assert t <= td and t % 16 == 0 and past % tk == 0 and db % width == 0
    blk = lambda b, p: (b, 0, p)
    cblk = lambda b, p: (layer, b, p, 0)
    const = lambda b, p: (0, 0)
    return pl.pallas_call(
        _sb_sample_kernel,
        out_shape=jax.ShapeDtypeStruct((bsz, t, db), BF16),
        grid=(bsz, db // width),
        in_specs=[pl.BlockSpec((None, t, width), blk),
                  pl.BlockSpec((None, t, width), blk),
                  pl.BlockSpec((None, t, width), blk),
                  pl.BlockSpec((None, t, width), blk),
                  pl.BlockSpec((None, None, width, past), cblk),
                  pl.BlockSpec((None, None, width, past), cblk),
                  pl.BlockSpec((2 * td, td), const),
                  pl.BlockSpec((2 * tk, tk), const)],
        out_specs=pl.BlockSpec((None, t, width), blk),
        scratch_shapes=_sb_scratch(t, width),
        compiler_params=_params(2),
        name="sb_sample",
    )(q16, g, nk16, nv16, cache_kt, cache_vt, _suffix_matrix(td), _suffix_matrix(tk))


def _row_tile(t):
    return ROW_TILE if t % ROW_TILE == 0 else t


def _cache_transposed(cache):
    l, b, s, h, dh = cache.shape
    return jnp.transpose(cache, (0, 1, 3, 4, 2)).reshape(l, b, h * dh, s)


def _cache_layout(stacked, n_heads):
    l, b, hd, s = stacked.shape
    return jnp.transpose(stacked.reshape(l, b, n_heads, hd // n_heads, s), (0, 1, 4, 2, 3))


def kernel(x_prompt, x_sample, p_prompt, p_sample, cache_a_k, cache_a_v, cache_b_k, cache_b_v,
           g_pre, w_in, rel_bias, w_out, g_post, w_ple, w_ple_gate):
    depth = w_in.shape[0]
    bsz, seq, d = x_prompt.shape
    dbsz, dseq, _ = x_sample.shape
    past = cache_b_k.shape[2]
    keep = min(BAND_PAST, seq)
    da = w_in.shape[2] // 8
    n_heads = da // HEAD_DIM
    tp, ts = bsz * seq, dbsz * dseq

    hp = x_prompt
    hs = x_sample.reshape(ts, d)
    p_prompt = p_prompt.reshape(depth, tp, -1)
    p_sample = p_sample.reshape(depth, ts, -1)
    cache_a_kt, cache_a_vt, cache_b_kt, cache_b_vt = (
        _cache_transposed(c) for c in (cache_a_k, cache_a_v, cache_b_k, cache_b_v))
    stacked = ()
    sak, sav, sbk, sbv = [], [], [], []
    for i in range(depth):
        w16 = w_in[i].astype(BF16)
        grp = lambda n: w16[:, n * da:(n + 1) * da]
        wnn16 = jnp.concatenate([grp(0), grp(2), grp(3), grp(4), grp(7)], axis=1)
        wnt16 = jnp.concatenate([grp(1), grp(5), grp(6)], axis=1).T
        wo16 = w_out[i].astype(BF16)
        wg16 = w_ple_gate[i].astype(BF16)
        wp16 = w_ple[i].astype(BF16)

        (qa, va16, ga, qb, vb16, gb, kat16, kbt16, *stacked) = _inproj_prompt(
            hp, g_pre[i], wnn16, wnt16, tuple(stacked), keep)
        oa = _band_prompt(qa, ga, kat16, va16, rel_bias[i])
        ob = _sb_prompt(qb, gb, kbt16, vb16)
        hp = _finish(oa.reshape(tp, da), ob.reshape(tp, da), hp.reshape(tp, d),
                     p_prompt, i, wo16, g_post[i], wg16, wp16,
                     _row_tile(tp)).reshape(bsz, seq, d)

        (qa, ka, va, ga, qb, kb, vb, gb, ka16, va16, kb16, vb16) = _inproj(
            hs, g_pre[i], w16, _row_tile(ts))
        s3 = lambda a: a.reshape(dbsz, dseq, da)
        oa = _band_sample(s3(qa), s3(ga), cache_a_kt, cache_a_vt, i, s3(ka16), s3(va16),
                          rel_bias[i], past)
        ob = _sb_sample(s3(qb), s3(gb), s3(kb16), s3(vb16), cache_b_kt, cache_b_vt, i)
        hs = _finish(oa.reshape(ts, da), ob.reshape(ts, da), hs, p_sample, i,
                     wo16, g_post[i], wg16, wp16, _row_tile(ts))
        s4 = lambda a: a.reshape(dbsz, dseq, n_heads, HEAD_DIM)
        sak.append(s4(ka))
        sav.append(s4(va))
        sbk.append(s4(kb))
        sbv.append(s4(vb))

    kb_st, vb_st, ka_st, va_st = stacked
    return (hp, hs.reshape(dbsz, dseq, d),
            _cache_layout(ka_st, n_heads), _cache_layout(va_st, n_heads),
            _cache_layout(kb_st, n_heads), _cache_layout(vb_st, n_heads),
            jnp.stack(sak), jnp.stack(sav), jnp.stack(sbk), jnp.stack(sbv))
```

```python
import functools

import numpy as np
import jax
import jax.numpy as jnp
from jax import lax
from jax.experimental import pallas as pl
from jax.experimental.pallas import tpu as pltpu

HEAD_DIM = 64
CHUNK = 64
N_PREV_CHUNKS = 8
BAND_PAST = N_PREV_CHUNKS * CHUNK
REL_CLIP = 128
RMS_EPS = 1e-6
NEG_INF = -1e30
ATTN_SCALE = HEAD_DIM ** -0.5
NEG_LOG2E = -1.4426950408889634

LANES = 128
HEADS_PER_TILE = LANES // HEAD_DIM
VMEM_LIMIT_BYTES = 56 * 1024 * 1024

KEY_BLOCK = LANES
BAND_Q_ROWS = 2 * CHUNK
BAND_STEP_BLOCKS = 4
BAND_GROUP = 4
BAND_WINDOW = BAND_PAST + BAND_Q_ROWS
BIAS_TABLE_PAD = 384
BIAS_EXT = 768

SB_TILE = 256
SB_Q_ROWS = 128
SB_STEP_BLOCKS = 2
SB_WINDOW_BLOCKS = 3
SB_WIDTH = 512
SB_UNDERFLOW = 105.0

INPROJ_ROWS = 512
ROW_TILE = 1024

BF16 = jnp.bfloat16
F32 = jnp.float32


def _params(n_axes):
    return pltpu.CompilerParams(
        dimension_semantics=("arbitrary",) * n_axes,
        vmem_limit_bytes=VMEM_LIMIT_BYTES)


def _nt_dot(a, b):
    return lax.dot_general(a, b, (((1,), (1,)), ((), ())), preferred_element_type=F32)


def _nn_dot(a, b):
    return jnp.dot(a, b, preferred_element_type=F32)


def _silu(g):
    return g * (1.0 / (1.0 + jnp.exp(-g)))


def _rms_scale(x, gain):
    ms = jnp.mean(x * x, axis=-1, keepdims=True)
    return x * lax.rsqrt(ms + RMS_EPS) * gain


def _head_masks(m_rows):
    lane = lax.broadcasted_iota(jnp.int32, (m_rows, LANES), 1)
    return [(lane >= hh * HEAD_DIM) & (lane < (hh + 1) * HEAD_DIM)
            for hh in range(HEADS_PER_TILE)]


def _stack_heads(q_pair, in_head):
    zero = jnp.zeros_like(q_pair)
    return jnp.concatenate([jnp.where(m, q_pair, zero) for m in in_head], axis=0)


def _unstack_heads(o2, in_head):
    m_rows = o2.shape[0] // HEADS_PER_TILE
    o = o2[0:m_rows]
    for hh in range(1, HEADS_PER_TILE):
        o = jnp.where(in_head[hh], o2[hh * m_rows:(hh + 1) * m_rows], o)
    return o


def _pad_rows(a, rows):
    return jnp.concatenate([a, jnp.zeros((rows - a.shape[0], a.shape[1]), a.dtype)], axis=0)


def _inproj_kernel(*refs, n_prev):
    x_ref, g_ref, w_ref = refs[:3]
    prev = refs[3:3 + (4 if n_prev else 0)]
    (qa_ref, ga_ref, qb_ref, gb_ref, ka16_ref, va16_ref, kb16_ref, vb16_ref) = refs[
        3 + len(prev):11 + len(prev)]
    stacked = refs[11 + len(prev):]
    hn = _rms_scale(x_ref[...], g_ref[...]).astype(BF16)
    gw = qa_ref.shape[-1]

    def proj(n):
        return _nn_dot(hn, w_ref[:, n * gw:(n + 1) * gw])

    qa_ref[...] = (proj(0) * ATTN_SCALE).astype(BF16)
    ga_ref[...] = proj(3)
    qb_ref[...] = (proj(4) * ATTN_SCALE).astype(BF16)
    gb_ref[...] = proj(7)
    for n, st_ref, b16_ref in zip((1, 2, 5, 6), stacked, (ka16_ref, va16_ref, kb16_ref, vb16_ref)):
        u = proj(n)
        st_ref[n_prev] = u
        b16_ref[...] = u.astype(BF16)
    for k in range(len(prev)):
        for dpt in range(n_prev):
            stacked[k][dpt] = prev[k][dpt]


def _inproj(x2d, g_pre, w16, prev, tm):
    t, d = x2d.shape
    d_in = w16.shape[1]
    gw = d_in // 8
    n_prev = prev[0].shape[0] if prev else 0
    row = lambda i: (i, 0)
    row3 = lambda i: (0, i, 0)
    const = lambda i: (0, 0)
    f32_out = jax.ShapeDtypeStruct((t, gw), F32)
    b16_out = jax.ShapeDtypeStruct((t, gw), BF16)
    st_out = jax.ShapeDtypeStruct((n_prev + 1, t, gw), F32)
    out_shape = (b16_out, f32_out, b16_out, f32_out, b16_out, b16_out, b16_out, b16_out,
                 st_out, st_out, st_out, st_out)
    return pl.pallas_call(
        functools.partial(_inproj_kernel, n_prev=n_prev),
        out_shape=out_shape,
        grid=(t // tm,),
        in_specs=[pl.BlockSpec((tm, d), row),
                  pl.BlockSpec((1, d), const),
                  pl.BlockSpec((d, d_in), const)]
                 + [pl.BlockSpec((n_prev, tm, gw), row3)] * len(prev),
        out_specs=(pl.BlockSpec((tm, gw), row),) * 8
                  + (pl.BlockSpec((n_prev + 1, tm, gw), row3),) * 4,
        compiler_params=_params(1),
        name="inproj_sample",
    )(x2d, g_pre.reshape(1, d), w16, *prev)


def _inproj_prompt_kernel(*refs, n_prev, keep_first):
    x_ref, g_ref, wnn_ref, wnt_ref = refs[:4]
    prev = refs[4:4 + (4 if n_prev else 0)]
    (qa_ref, va16_ref, ga_ref, qb_ref, vb16_ref, gb_ref, kat16_ref, kbt16_ref,
     kb_st_ref, vb_st_ref, ka_st_ref, va_st_ref) = refs[4 + len(prev):]
    j = pl.program_id(1)
    gw = qa_ref.shape[-1]
    hn = _rms_scale(x_ref[...], g_ref[...]).astype(BF16)

    def proj(n):
        return _nn_dot(hn, wnn_ref[:, n * gw:(n + 1) * gw])

    nt = _nt_dot(wnt_ref[...], hn)
    kat, kbt, vbt = nt[0:gw], nt[gw:2 * gw], nt[2 * gw:3 * gw]
    vb16_ref[...] = vbt.T.astype(BF16)
    for u in range(kat16_ref.shape[0]):
        blk = slice(u * KEY_BLOCK, (u + 1) * KEY_BLOCK)
        kat16_ref[u] = kat[:, blk].astype(BF16)
        kbt16_ref[u] = kbt[:, blk].astype(BF16)
    for dpt in range(n_prev):
        kb_st_ref[dpt] = prev[0][dpt]
        vb_st_ref[dpt] = prev[1][dpt]
    kb_st_ref[n_prev] = kbt
    vb_st_ref[n_prev] = vbt

    qa_ref[...] = (proj(0) * ATTN_SCALE).astype(BF16)
    va = proj(1)
    va16_ref[...] = va.astype(BF16)
    ga_ref[...] = proj(2)
    qb_ref[...] = (proj(3) * ATTN_SCALE).astype(BF16)
    gb_ref[...] = proj(4)

    @pl.when(j >= keep_first)
    def _():
        for dpt in range(n_prev):
            ka_st_ref[dpt] = prev[2][dpt]
            va_st_ref[dpt] = prev[3][dpt]
        ka_st_ref[n_prev] = kat
        va_st_ref[n_prev] = va.T


def _inproj_prompt(x3d, g_pre, wnn16, wnt16, prev, keep):
    bsz, seq, d = x3d.shape
    tm = INPROJ_ROWS
    gw = wnt16.shape[0] // 3
    n_prev = prev[0].shape[0] if prev else 0
    assert seq % tm == 0 and keep % tm == 0 and tm % KEY_BLOCK == 0
    keep_first = (seq - keep) // tm
    tok = lambda b, j: (b, j, 0)
    const = lambda b, j: (0, 0)
    ktb = lambda b, j: (b, j, 0, 0)
    st_all = lambda b, j: (0, b, 0, j)
    st_keep = lambda b, j: (0, b, 0, jnp.maximum(j - keep_first, 0))
    tok_b16 = jax.ShapeDtypeStruct((bsz, seq, gw), BF16)
    tok_f32 = jax.ShapeDtypeStruct((bsz, seq, gw), F32)
    kt_b16 = jax.ShapeDtypeStruct((bsz, seq // KEY_BLOCK, gw, KEY_BLOCK), BF16)
    st_f32 = jax.ShapeDtypeStruct((n_prev + 1, bsz, gw, seq), F32)
    keep_f32 = jax.ShapeDtypeStruct((n_prev + 1, bsz, gw, keep), F32)
    out_shape = (tok_b16, tok_b16, tok_f32, tok_b16, tok_b16, tok_f32, kt_b16, kt_b16,
                 st_f32, st_f32, keep_f32, keep_f32)
    tok_spec = pl.BlockSpec((None, tm, gw), tok)
    kt_spec = pl.BlockSpec((None, tm // KEY_BLOCK, gw, KEY_BLOCK), ktb)
    out_specs = (tok_spec,) * 6 + (kt_spec,) * 2 + (
        pl.BlockSpec((n_prev + 1, None, gw, tm), st_all),
        pl.BlockSpec((n_prev + 1, None, gw, tm), st_all),
        pl.BlockSpec((n_prev + 1, None, gw, tm), st_keep),
        pl.BlockSpec((n_prev + 1, None, gw, tm), st_keep))
    in_specs = [pl.BlockSpec((None, tm, d), tok),
                pl.BlockSpec((1, d), const),
                pl.BlockSpec(wnn16.shape, const, pipeline_mode=pl.Buffered(1)),
                pl.BlockSpec(wnt16.shape, const, pipeline_mode=pl.Buffered(1))]
    if n_prev:
        in_specs += [pl.BlockSpec((n_prev, None, gw, tm), st_all),
                     pl.BlockSpec((n_prev, None, gw, tm), st_all),
                     pl.BlockSpec((n_prev, None, gw, tm), st_keep),
                     pl.BlockSpec((n_prev, None, gw, tm), st_keep)]
    return pl.pallas_call(
        functools.partial(_inproj_prompt_kernel, n_prev=n_prev, keep_first=keep_first),
        out_shape=out_shape,
        grid=(bsz, seq // tm),
        in_specs=in_specs,
        out_specs=out_specs,
        compiler_params=_params(2),
        name="inproj_prompt",
    )(x3d, g_pre.reshape(1, d), wnn16, wnt16, *prev)


def _finish_kernel(oa_ref, ob_ref, h_ref, p_ref, woa_ref, wob_ref, gpost_ref, wg_ref, wp_ref,
                   out_ref):
    y = _nn_dot(oa_ref[...], woa_ref[...]) + _nn_dot(ob_ref[...], wob_ref[...])
    h1 = h_ref[...] + _rms_scale(y, gpost_ref[...])
    gate_logit = _nn_dot(h1.astype(BF16), wg_ref[...])
    gate = 1.0 / (1.0 + jnp.exp(-gate_logit))
    emb = _nn_dot(p_ref[...].astype(BF16), wp_ref[...])
    out_ref[...] = h1 + gate * emb


def _finish(oa, ob, h2d, p_all, layer, wo16, g_post, wg16, wp16, tm):
    t, d = h2d.shape
    da = oa.shape[1]
    dp = p_all.shape[2]
    row = lambda i: (i, 0)
    const = lambda i: (0, 0)
    return pl.pallas_call(
        _finish_kernel,
        out_shape=jax.ShapeDtypeStruct((t, d), F32),
        grid=(t // tm,),
        in_specs=[pl.BlockSpec((tm, da), row),
                  pl.BlockSpec((tm, da), row),
                  pl.BlockSpec((tm, d), row),
                  pl.BlockSpec((None, tm, dp), lambda i: (layer, i, 0)),
                  pl.BlockSpec((da, d), const),
                  pl.BlockSpec((da, d), lambda i: (1, 0)),
                  pl.BlockSpec((1, d), const),
                  pl.BlockSpec((d, d), const),
                  pl.BlockSpec((dp, d), const)],
        out_specs=pl.BlockSpec((tm, d), row),
        compiler_params=_params(1),
        name="finish",
    )(oa, ob, h2d, p_all, wo16, wo16, g_post.reshape(1, d), wg16, wp16)


def _build_band_bias(tbl_ref, maskadd_ref, bias_ref):
    m_rows = maskadd_ref.shape[0]
    n_heads = bias_ref.shape[0] // m_rows
    tbl = tbl_ref[...]
    p1 = tbl.astype(BF16)
    r1 = tbl - p1.astype(F32)
    p2 = r1.astype(BF16)
    p3 = (r1 - p2.astype(F32)).astype(BF16)
    r_idx = lax.broadcasted_iota(jnp.int32, (BIAS_TABLE_PAD, BIAS_EXT), 0)
    m_idx = lax.broadcasted_iota(jnp.int32, (BIAS_TABLE_PAD, BIAS_EXT), 1)
    rel = jnp.clip(m_idx - (m_rows - 1) - BAND_PAST, -REL_CLIP, REL_CLIP) + REL_CLIP
    onehot = (r_idx == rel).astype(BF16)
    t_ext = _nn_dot(p1, onehot) + _nn_dot(p2, onehot) + _nn_dot(p3, onehot)
    ql = lax.broadcasted_iota(jnp.int32, (m_rows, BIAS_EXT), 0)
    n_bits = int(m_rows - 1).bit_length()
    for h in range(n_heads):
        x = jnp.broadcast_to(t_ext[h:h + 1, :], (m_rows, BIAS_EXT))
        x = pltpu.roll(x, BIAS_EXT - (m_rows - 1), 1)
        for b in range(n_bits):
            x = jnp.where(((ql >> b) & 1) == 1, pltpu.roll(x, 1 << b, 1), x)
        bias_ref[h * m_rows:(h + 1) * m_rows, :] = x[:, :BAND_WINDOW] + maskadd_ref[...]


def _band_attend(q_ref, g_ref, bias_ref, out_ref, m_rows, windows):
    n_tiles = q_ref.shape[1] // LANES
    in_head = _head_masks(m_rows)
    kl = lax.broadcasted_iota(jnp.int32, (1, BAND_WINDOW), 1)
    lane_sl = [slice(p * LANES, (p + 1) * LANES) for p in range(n_tiles)]
    row_sl = [slice(j * m_rows, (j + 1) * m_rows) for j in range(len(windows))]
    rows2 = HEADS_PER_TILE * m_rows
    items = [(j, p) for j in range(len(windows)) for p in range(n_tiles)]

    def item_scores(n):
        j, p = items[n]
        return windows[j][0](lane_sl[p], _stack_heads(q_ref[row_sl[j], lane_sl[p]], in_head))

    groups = [list(range(a, min(a + BAND_GROUP, len(items))))
              for a in range(0, len(items), BAND_GROUP)]
    s_next = [item_scores(n) for n in groups[0]]
    for gi, group in enumerate(groups):
        ss = [s_next[i] + bias_ref[items[n][1] * rows2:(items[n][1] + 1) * rows2, :]
              for i, n in enumerate(group)]
        if gi + 1 < len(groups):
            s_next = [item_scores(n) for n in groups[gi + 1]]
        es = []
        for i, n in enumerate(group):
            s, key_lo = ss[i], windows[items[n][0]][2]
            if key_lo is not None:
                s = jnp.where(kl >= key_lo, s, NEG_INF)
            es.append(jnp.exp(s - jnp.max(s, axis=-1, keepdims=True)).astype(BF16))
        for i, n in enumerate(group):
            j, p = items[n]
            o_ext = windows[j][1](lane_sl[p], es[i])
            o2 = o_ext[:, :LANES] * (1.0 / o_ext[:, LANES:])
            o_pair = _unstack_heads(o2, in_head)
            out_ref[row_sl[j], lane_sl[p]] = (
                o_pair * _silu(g_ref[row_sl[j], lane_sl[p]])).astype(BF16)


def _band_prompt_kernel(q_ref, g_ref, kt_ref, v_ref, tbl_ref, maskadd_ref, out_ref, bias_ref):
    b = pl.program_id(0)
    step = pl.program_id(1)
    n_prev = BAND_PAST // KEY_BLOCK
    n_blk = BAND_WINDOW // KEY_BLOCK

    @pl.when((b == 0) & (step == 0))
    def _():
        _build_band_bias(tbl_ref, maskadd_ref, bias_ref)

    def window(c, at_start):
        first = c - n_prev
        blocks = [max(first + t, 0) if at_start else first + t for t in range(n_blk)]

        def scores(sl, q2):
            kt = jnp.concatenate([kt_ref[blk, sl, :] for blk in blocks], axis=1)
            return _nn_dot(q2, kt)

        def weighted_values(sl, e16):
            if at_start:
                v = jnp.concatenate(
                    [v_ref[blk * KEY_BLOCK:(blk + 1) * KEY_BLOCK, sl] for blk in blocks], axis=0)
            else:
                v = v_ref[pl.ds(pl.multiple_of(first * KEY_BLOCK, KEY_BLOCK), BAND_WINDOW), sl]
            return _nn_dot(e16, jnp.concatenate([v, jnp.ones_like(v)], axis=1))

        return scores, weighted_values, (n_prev - c) * KEY_BLOCK if at_start else None

    @pl.when(step == 0)
    def _():
        _band_attend(q_ref, g_ref, bias_ref, out_ref, BAND_Q_ROWS,
                     [window(j, True) for j in range(BAND_STEP_BLOCKS)])

    @pl.when(step > 0)
    def _():
        _band_attend(q_ref, g_ref, bias_ref, out_ref, BAND_Q_ROWS,
                     [window(step * BAND_STEP_BLOCKS + j, False)
                      for j in range(BAND_STEP_BLOCKS)])


def _band_static_mask(q_pos, k_pos):
    qc = q_pos // CHUNK
    kc = k_pos // CHUNK
    ok = (kc[None, :] <= qc[:, None]) & (kc[None, :] >= qc[:, None] - N_PREV_CHUNKS)
    return np.where(ok, 0.0, NEG_INF).astype(np.float32)


def _pad_table(rel_bias):
    return jnp.pad(rel_bias, ((0, 0), (0, BIAS_TABLE_PAD - rel_bias.shape[1])))


def _band_prompt(q16, g, kt16, v16, rel_bias):
    bsz, seq, da = q16.shape
    n_heads = da // HEAD_DIM
    step_rows = BAND_STEP_BLOCKS * BAND_Q_ROWS
    assert BAND_Q_ROWS == KEY_BLOCK and seq % step_rows == 0
    assert BAND_STEP_BLOCKS == BAND_PAST // KEY_BLOCK
    q_pos = BAND_PAST + np.arange(BAND_Q_ROWS)
    k_pos = np.arange(BAND_WINDOW)
    maskadd = jnp.asarray(_band_static_mask(q_pos, k_pos))
    blk = lambda b, c: (b, c, 0)
    whole = lambda b, c: (b, 0, 0)
    whole4 = lambda b, c: (b, 0, 0, 0)
    const = lambda b, c: (0, 0)
    return pl.pallas_call(
        _band_prompt_kernel,
        out_shape=jax.ShapeDtypeStruct((bsz, seq, da), BF16),
        grid=(bsz, seq // step_rows),
        in_specs=[pl.BlockSpec((None, step_rows, da), blk),
                  pl.BlockSpec((None, step_rows, da), blk),
                  pl.BlockSpec((None, seq // KEY_BLOCK, da, KEY_BLOCK), whole4),
                  pl.BlockSpec((None, seq, da), whole),
                  pl.BlockSpec((n_heads, BIAS_TABLE_PAD), const),
                  pl.BlockSpec((BAND_Q_ROWS, BAND_WINDOW), const)],
        out_specs=pl.BlockSpec((None, step_rows, da), blk),
        scratch_shapes=[pltpu.VMEM((n_heads * BAND_Q_ROWS, BAND_WINDOW), F32)],
        compiler_params=_params(2),
        name="band_prompt",
    )(q16, g, kt16, v16, _pad_table(rel_bias), maskadd)


def _band_sample_kernel(q_ref, g_ref, ckt_ref, cvt_ref, nk_ref, nv_ref, tbl_ref, maskadd_ref,
                        out_ref, bias_ref):
    b = pl.program_id(0)
    la = ckt_ref.shape[1]

    @pl.when(b == 0)
    def _():
        _build_band_bias(tbl_ref, maskadd_ref, bias_ref)

    def scores(sl, q2):
        s_cache = _nn_dot(q2, ckt_ref[sl, :].astype(BF16))
        s_new = _nt_dot(q2, _pad_rows(nk_ref[:, sl], KEY_BLOCK))
        return jnp.concatenate([s_cache, s_new], axis=1)

    def weighted_values(sl, e16):
        cvt = cvt_ref[sl, :].astype(BF16)
        nv = _pad_rows(nv_ref[:, sl], KEY_BLOCK)
        return (_nt_dot(e16[:, :la], jnp.concatenate([cvt, jnp.ones_like(cvt)], axis=0))
                + _nn_dot(e16[:, la:], jnp.concatenate([nv, jnp.ones_like(nv)], axis=1)))

    _band_attend(q_ref, g_ref, bias_ref, out_ref, q_ref.shape[0],
                 [(scores, weighted_values, None)])


def _band_sample(q16, g, cache_kt, cache_vt, layer, nk16, nv16, rel_bias, past):
    bsz, t, da = q16.shape
    la = cache_kt.shape[3]
    n_heads = da // HEAD_DIM
    assert la == BAND_PAST and la + KEY_BLOCK == BAND_WINDOW and t <= KEY_BLOCK and t % 16 == 0
    q_pos = past + np.arange(t)
    k_pos = past - la + np.arange(BAND_WINDOW)
    maskadd = _band_static_mask(q_pos, k_pos)
    maskadd[:, la + t:] = NEG_INF
    maskadd = jnp.asarray(maskadd)
    blk = lambda b: (b, 0, 0)
    const = lambda b: (0, 0)
    return pl.pallas_call(
        _band_sample_kernel,
        out_shape=jax.ShapeDtypeStruct((bsz, t, da), BF16),
        grid=(bsz,),
        in_specs=[pl.BlockSpec((None, t, da), blk),
                  pl.BlockSpec((None, t, da), blk),
                  pl.BlockSpec((None, None, da, la), lambda b: (layer, b, 0, 0)),
                  pl.BlockSpec((None, None, da, la), lambda b: (layer, b, 0, 0)),
                  pl.BlockSpec((None, t, da), blk),
                  pl.BlockSpec((None, t, da), blk),
                  pl.BlockSpec((n_heads, BIAS_TABLE_PAD), const),
                  pl.BlockSpec((t, BAND_WINDOW), const)],
        out_specs=pl.BlockSpec((None, t, da), blk),
        scratch_shapes=[pltpu.VMEM((n_heads * t, BAND_WINDOW), F32)],
        compiler_params=_params(1),
        name="band_sample",
    )(q16, g, cache_kt, cache_vt, nk16, nv16, _pad_table(rel_bias), maskadd)


def _suffix_matrix(width):
    j = np.arange(width)[:, None]
    s = np.arange(width)[None, :]
    tri = (j >= s).astype(np.float32)
    return jnp.asarray(np.concatenate([tri, tri], axis=0), dtype=BF16)


def _sb_tiles(q2s, tiles, carry_ref, acc_ref, first, k_is_t, v_is_t):
    n = len(q2s)
    zs = [[(_nn_dot if k_is_t else _nt_dot)(q2s[i], ks[i]) for i in range(n)]
          for ks, _, _, _ in tiles]
    split = []
    for t, (_, _, _, causal) in enumerate(tiles):
        split.append([])
        for z in zs[t]:
            sp = _softplus(z)
            if causal is not None:
                sp = jnp.where(causal, sp, 0.0)
            hi = sp.astype(BF16)
            lo = (sp - hi.astype(F32)).astype(BF16)
            split[t].append(jnp.concatenate([hi, lo], axis=1))
    sums = [[_nn_dot(s, tri) for s in split[t]] for t, (_, _, tri, _) in enumerate(tiles)]
    carries = [None if first else carry_ref[i] for i in range(n)]
    ws = []
    for t, (_, _, _, causal) in enumerate(tiles):
        ws.append([])
        for i in range(n):
            log_w = zs[t][i] - sums[t][i]
            total = sums[t][i][:, 0:1]
            if carries[i] is not None:
                log_w = log_w - carries[i]
                total = total + carries[i]
            w = jnp.exp(log_w)
            if causal is not None:
                w = jnp.where(causal, w, 0.0)
            ws[t].append(w.astype(BF16))
            carries[i] = total
    for i in range(n):
        acc = None if first else acc_ref[i]
        for t, (_, vs, _, _) in enumerate(tiles):
            part = (_nt_dot if v_is_t else _nn_dot)(ws[t][i], vs[i])
            acc = part if acc is None else acc + part
        acc_ref[i] = acc
        carry_ref[i] = carries[i]
    return carries


def _sb_setup(q_ref, td):
    tq, width = q_ref.shape
    lane_sl = [slice(p * LANES, (p + 1) * LANES) for p in range(width // LANES)]
    in_head = _head_masks(tq)
    rows2 = HEADS_PER_TILE * tq
    row = lax.broadcasted_iota(jnp.int32, (rows2, td), 0) % tq
    col = lax.broadcasted_iota(jnp.int32, (rows2, td), 1)
    causal = col < row
    q2s = [_stack_heads(q_ref[:, sl], in_head) for sl in lane_sl]
    return lane_sl, in_head, causal, q2s


def _sb_min_carry(carries):
    m = carries[0]
    for c in carries[1:]:
        m = jnp.minimum(m, c)
    return jnp.min(m)


def _sb_store(acc_ref, lane_sl, in_head, g_ref, out_ref):
    for n, sl in enumerate(lane_sl):
        o_pair = _unstack_heads(acc_ref[n], in_head)
        out_ref[:, sl] = (o_pair * _silu(g_ref[:, sl])).astype(BF16)


def _sb_scratch(rows, width):
    n = width // LANES
    return [pltpu.VMEM((n, HEADS_PER_TILE * rows, 1), F32),
            pltpu.VMEM((n, HEADS_PER_TILE * rows, LANES), F32)]


def _softplus(z):
    return jnp.maximum(z, 0.0) + jnp.log(1.0 + jnp.exp2(jnp.abs(z) * NEG_LOG2E))


def _sb_window(q2s, kts, vs, tri, causal, valid, slots, carry_ref, acc_ref, first):
    n = len(q2s)
    n_blk = kts[0].shape[1] // KEY_BLOCK
    newest_first = list(reversed(range(n_blk)))
    zs = [_nn_dot(q2s[c], kts[c]) for c in range(n)]

    def block_mask(c, b):
        mask = causal if b == n_blk - 1 else None
        if valid[c] is not None:
            mask = valid[c] if mask is None else mask & valid[c]
        return mask

    split = []
    for c in range(n):
        split.append({})
        for b in newest_first:
            sp = _softplus(zs[c][:, b * KEY_BLOCK:(b + 1) * KEY_BLOCK])
            mask = block_mask(c, b)
            if mask is not None:
                sp = jnp.where(mask, sp, 0.0)
            hi = sp.astype(BF16)
            lo = (sp - hi.astype(F32)).astype(BF16)
            split[c][b] = jnp.concatenate([hi, lo], axis=1)
    sums = [{b: _nn_dot(split[c][b], tri) for b in newest_first} for c in range(n)]
    carries, ws = [], []
    for c in range(n):
        carry = None if first else carry_ref[slots[c]]
        pieces = [None] * n_blk
        for b in newest_first:
            log_w = zs[c][:, b * KEY_BLOCK:(b + 1) * KEY_BLOCK] - sums[c][b]
            total = sums[c][b][:, 0:1]
            if carry is not None:
                log_w = log_w - carry
                total = total + carry
            w = jnp.exp(log_w)
            mask = block_mask(c, b)
            if mask is not None:
                w = jnp.where(mask, w, 0.0)
            pieces[b] = w.astype(BF16)
            carry = total
        carries.append(carry)
        ws.append(pieces[0] if n_blk == 1 else jnp.concatenate(pieces, axis=1))
    for c in range(n):
        part = _nn_dot(ws[c], vs[c])
        acc_ref[slots[c]] = part if first else acc_ref[slots[c]] + part
        carry_ref[slots[c]] = carries[c]
    return carries


def _sb_prompt_kernel(q_ref, g_ref, kt_ref, v_ref, tri_ref, out_ref, carry_ref, acc_ref):
    step = pl.program_id(1)
    r, n_qb, n_win = SB_Q_ROWS, SB_STEP_BLOCKS, SB_WINDOW_BLOCKS
    lane_sl = [slice(p * LANES, (p + 1) * LANES) for p in range(q_ref.shape[1] // LANES)]
    n_lane = len(lane_sl)
    in_head = _head_masks(r)
    row = lax.broadcasted_iota(jnp.int32, (HEADS_PER_TILE * r, KEY_BLOCK), 0) % r
    col = lax.broadcasted_iota(jnp.int32, (HEADS_PER_TILE * r, KEY_BLOCK), 1)
    causal = col < row
    tri = tri_ref[...]
    chains = [(a, p) for a in range(n_qb) for p in range(n_lane)]
    q2s = [_stack_heads(q_ref[a * r:(a + 1) * r, lane_sl[p]], in_head) for a, p in chains]

    def keys(first_blk, n_blk, sl):
        return jnp.concatenate([kt_ref[first_blk + u, sl, :] for u in range(n_blk)], axis=1)

    def values(first_blk, n_blk, sl):
        if isinstance(first_blk, int):
            return v_ref[first_blk * KEY_BLOCK:(first_blk + n_blk) * KEY_BLOCK, sl]
        start = pl.multiple_of(first_blk * KEY_BLOCK, KEY_BLOCK)
        return v_ref[pl.ds(start, n_blk * KEY_BLOCK), sl]

    def run_windows(which, first_blk, n_blk):
        for a in sorted({chains[c][0] for c in which}):
            mine = [c for c in which if chains[c][0] == a]
            _sb_window([q2s[c] for c in mine],
                       [keys(first_blk(a), n_blk(a), lane_sl[chains[c][1]]) for c in mine],
                       [values(first_blk(a), n_blk(a), lane_sl[chains[c][1]]) for c in mine],
                       tri, causal, [None] * len(mine), mine, carry_ref, acc_ref, True)

    every = list(range(len(chains)))

    @pl.when(step == 0)
    def _():
        run_windows(every, lambda a: max(a + 1 - n_win, 0), lambda a: min(a + 1, n_win))

    @pl.when(step > 0)
    def _():
        _sb_window(q2s,
                   [keys(step * n_qb + a + 1 - n_win, n_win, lane_sl[p]) for a, p in chains],
                   [values(step * n_qb + a + 1 - n_win, n_win, lane_sl[p]) for a, p in chains],
                   tri, causal, [None] * len(chains), every, carry_ref, acc_ref, True)

    n_more = step * n_qb + n_qb - n_win

    def cond(loop_state):
        it, smallest = loop_state
        return (it < n_more) & (smallest < SB_UNDERFLOW)

    def body(loop_state):
        it, _ = loop_state
        blks = [step * n_qb + a - n_win - it for a in range(n_qb)]
        valid = [None if a == n_qb - 1 else blks[a] >= 0 for a in range(n_qb)]
        blks = [jnp.maximum(blk, 0) for blk in blks]
        carries = _sb_window(q2s,
                             [keys(blks[a], 1, lane_sl[p]) for a, p in chains],
                             [values(blks[a], 1, lane_sl[p]) for a, p in chains],
                             tri, None, [valid[a] for a, _ in chains], every,
                             carry_ref, acc_ref, False)
        return it + 1, _sb_min_carry(carries)

    lax.while_loop(cond, body,
                   (jnp.int32(0), _sb_min_carry([carry_ref[c] for c in every])))
    for c, (a, p) in enumerate(chains):
        rows = slice(a * r, (a + 1) * r)
        o_pair = _unstack_heads(acc_ref[c], in_head)
        out_ref[rows, lane_sl[p]] = (o_pair * _silu(g_ref[rows, lane_sl[p]])).astype(BF16)


def _sb_prompt(q16, g, kt16, v16):
    bsz, seq, db = q16.shape
    step_rows = SB_STEP_BLOCKS * SB_Q_ROWS
    n_chains = SB_STEP_BLOCKS * (db // LANES)
    assert SB_Q_ROWS == KEY_BLOCK and seq % step_rows == 0 and db % LANES == 0
    qblk = lambda b, i: (b, i, 0)
    whole = lambda b, i: (b, 0, 0)
    whole4 = lambda b, i: (b, 0, 0, 0)
    const = lambda b, i: (0, 0)
    return pl.pallas_call(
        _sb_prompt_kernel,
        out_shape=jax.ShapeDtypeStruct((bsz, seq, db), BF16),
        grid=(bsz, seq // step_rows),
        in_specs=[pl.BlockSpec((None, step_rows, db), qblk),
                  pl.BlockSpec((None, step_rows, db), qblk),
                  pl.BlockSpec((None, seq // KEY_BLOCK, db, KEY_BLOCK), whole4),
                  pl.BlockSpec((None, seq, db), whole),
                  pl.BlockSpec((2 * KEY_BLOCK, KEY_BLOCK), const)],
        out_specs=pl.BlockSpec((None, step_rows, db), qblk),
        scratch_shapes=[pltpu.VMEM((n_chains, HEADS_PER_TILE * SB_Q_ROWS, 1), F32),
                        pltpu.VMEM((n_chains, HEADS_PER_TILE * SB_Q_ROWS, LANES), F32)],
        compiler_params=_params(2),
        name="sb_prompt",
    )(q16, g, kt16, v16, _suffix_matrix(KEY_BLOCK))


def _sb_sample_kernel(q_ref, g_ref, nk_ref, nv_ref, ckt_ref, cvt_ref, trid_ref, trif_ref,
                      out_ref, carry_ref, acc_ref):
    td = trid_ref.shape[1]
    tk = trif_ref.shape[1]
    n_full = ckt_ref.shape[1] // tk
    trif = trif_ref[...]
    lane_sl, in_head, causal, q2s = _sb_setup(q_ref, td)

    carries = _sb_tiles(q2s, [([_pad_rows(nk_ref[:, sl], td) for sl in lane_sl],
                               [_pad_rows(nv_ref[:, sl], td) for sl in lane_sl],
                               trid_ref[...], causal)], carry_ref, acc_ref, True, False, False)
    smallest = _sb_min_carry(carries)

    for tile in reversed(range(n_full)):
        cols = slice(tile * tk, (tile + 1) * tk)

        def step(cols=cols):
            kts = [ckt_ref[sl, cols].astype(BF16) for sl in lane_sl]
            vts = [cvt_ref[sl, cols].astype(BF16) for sl in lane_sl]
            return _sb_min_carry(_sb_tiles(q2s, [(kts, vts, trif, None)], carry_ref, acc_ref,
                                           False, True, True))

        smallest = lax.cond(smallest < SB_UNDERFLOW, step, lambda s=smallest: s)
    _sb_store(acc_ref, lane_sl, in_head, g_ref, out_ref)


def _sb_sample(q16, g, nk16, nv16, cache_kt, cache_vt, layer):
    bsz, t, db = q16.shape
    past = cache_kt.shape[3]
    width, tk, td = SB_WIDTH, SB_TILE, KEY_BLOCK
    assert t <= td and t % 16 == 0 and past % tk == 0 and db % width == 0
    blk = lambda b, p: (b, 0, p)
    cblk = lambda b, p: (layer, b, p, 0)
    const = lambda b, p: (0, 0)
    return pl.pallas_call(
        _sb_sample_kernel,
        out_shape=jax.ShapeDtypeStruct((bsz, t, db), BF16),
        grid=(bsz, db // width),
        in_specs=[pl.BlockSpec((None, t, width), blk),
                  pl.BlockSpec((None, t, width), blk),
                  pl.BlockSpec((None, t, width), blk),
                  pl.BlockSpec((None, t, width), blk),
                  pl.BlockSpec((None, None, width, past), cblk),
                  pl.BlockSpec((None, None, width, past), cblk),
                  pl.BlockSpec((2 * td, td), const),
                  pl.BlockSpec((2 * tk, tk), const)],
        out_specs=pl.BlockSpec((None, t, width), blk),
        scratch_shapes=_sb_scratch(t, width),
        compiler_params=_params(2),
        name="sb_sample",
    )(q16, g, nk16, nv16, cache_kt, cache_vt, _suffix_matrix(td), _suffix_matrix(tk))


def _row_tile(t):
    return ROW_TILE if t % ROW_TILE == 0 else t


def _cache_transposed(cache):
    l, b, s, h, dh = cache.shape
    return jnp.transpose(cache, (0, 1, 3, 4, 2)).reshape(l, b, h * dh, s)


def _cache_layout(stacked, n_heads):
    l, b, hd, s = stacked.shape
    return jnp.transpose(stacked.reshape(l, b, n_heads, hd // n_heads, s), (0, 1, 4, 2, 3))


def kernel(x_prompt, x_sample, p_prompt, p_sample, cache_a_k, cache_a_v, cache_b_k, cache_b_v,
           g_pre, w_in, rel_bias, w_out, g_post, w_ple, w_ple_gate):
    depth = w_in.shape[0]
    bsz, seq, d = x_prompt.shape
    dbsz, dseq, _ = x_sample.shape
    past = cache_b_k.shape[2]
    keep = min(BAND_PAST, seq)
    da = w_in.shape[2] // 8
    n_heads = da // HEAD_DIM
    tp, ts = bsz * seq, dbsz * dseq

    hp = x_prompt
    hs = x_sample.reshape(ts, d)
    p_prompt = p_prompt.reshape(depth, tp, -1)
    p_sample = p_sample.reshape(depth, ts, -1)
    cache_a_kt, cache_a_vt, cache_b_kt, cache_b_vt = (
        _cache_transposed(c) for c in (cache_a_k, cache_a_v, cache_b_k, cache_b_v))
    stacked = ()
    sample_stacked = ()
    w_in16, w_out16, w_gate16, w_ple16 = (
        w.astype(BF16) for w in (w_in, w_out, w_ple_gate, w_ple))
    for i in range(depth):
        w16 = w_in16[i]
        grp = lambda n: w16[:, n * da:(n + 1) * da]
        wnn16 = jnp.concatenate([grp(0), grp(2), grp(3), grp(4), grp(7)], axis=1)
        wnt16 = jnp.concatenate([grp(1), grp(5), grp(6)], axis=1).T
        wo16, wg16, wp16 = w_out16[i], w_gate16[i], w_ple16[i]

        (qa, va16, ga, qb, vb16, gb, kat16, kbt16, *stacked) = _inproj_prompt(
            hp, g_pre[i], wnn16, wnt16, tuple(stacked), keep)
        oa = _band_prompt(qa, ga, kat16, va16, rel_bias[i])
        ob = _sb_prompt(qb, gb, kbt16, vb16)
        hp = _finish(oa.reshape(tp, da), ob.reshape(tp, da), hp.reshape(tp, d),
                     p_prompt, i, wo16, g_post[i], wg16, wp16,
                     _row_tile(tp)).reshape(bsz, seq, d)

        (qa, ga, qb, gb, ka16, va16, kb16, vb16, *sample_stacked) = _inproj(
            hs, g_pre[i], w16, tuple(sample_stacked), _row_tile(ts))
        s3 = lambda a: a.reshape(dbsz, dseq, da)
        oa = _band_sample(s3(qa), s3(ga), cache_a_kt, cache_a_vt, i, s3(ka16), s3(va16),
                          rel_bias[i], past)
        ob = _sb_sample(s3(qb), s3(gb), s3(kb16), s3(vb16), cache_b_kt, cache_b_vt, i)
        hs = _finish(oa.reshape(ts, da), ob.reshape(ts, da), hs, p_sample, i,
                     wo16, g_post[i], wg16, wp16, _row_tile(ts))

    kb_st, vb_st, ka_st, va_st = stacked
    s5 = lambda a: a.reshape(depth, dbsz, dseq, n_heads, HEAD_DIM)
    return (hp, hs.reshape(dbsz, dseq, d),
            _cache_layout(ka_st, n_heads), _cache_layout(va_st, n_heads),
            _cache_layout(kb_st, n_heads), _cache_layout(vb_st, n_heads),
            *(s5(a) for a in sample_stacked))
```

```python
import functools

import numpy as np
import jax
import jax.numpy as jnp
from jax import lax
from jax.experimental import pallas as pl
from jax.experimental.pallas import tpu as pltpu

HEAD_DIM = 64
CHUNK = 64
N_PREV_CHUNKS = 8
BAND_PAST = N_PREV_CHUNKS * CHUNK
REL_CLIP = 128
RMS_EPS = 1e-6
NEG_INF = -1e30
ATTN_SCALE = HEAD_DIM ** -0.5
NEG_LOG2E = -1.4426950408889634

LANES = 128
HEADS_PER_TILE = LANES // HEAD_DIM
VMEM_LIMIT_BYTES = 56 * 1024 * 1024

KEY_BLOCK = LANES
BAND_Q_ROWS = 2 * CHUNK
BAND_STEP_BLOCKS = 4
BAND_GROUP = 4
BAND_WINDOW = BAND_PAST + BAND_Q_ROWS
BIAS_TABLE_PAD = 384
BIAS_EXT = 768

SB_TILE = 256
SB_Q_ROWS = 128
SB_STEP_BLOCKS = 2
SB_WINDOW_BLOCKS = 3
SB_WIDTH = 512
SB_UNDERFLOW = 105.0

INPROJ_ROWS = 512
ROW_TILE = 1024

BF16 = jnp.bfloat16
F32 = jnp.float32


def _params(n_axes):
    return pltpu.CompilerParams(
        dimension_semantics=("arbitrary",) * n_axes,
        vmem_limit_bytes=VMEM_LIMIT_BYTES)


def _nt_dot(a, b):
    return lax.dot_general(a, b, (((1,), (1,)), ((), ())), preferred_element_type=F32)


def _nn_dot(a, b):
    return jnp.dot(a, b, preferred_element_type=F32)


def _silu(g):
    return g * (1.0 / (1.0 + jnp.exp(-g)))


def _rms_scale(x, gain):
    ms = jnp.mean(x * x, axis=-1, keepdims=True)
    return x * lax.rsqrt(ms + RMS_EPS) * gain


def _head_masks(m_rows):
    lane = lax.broadcasted_iota(jnp.int32, (m_rows, LANES), 1)
    return [(lane >= hh * HEAD_DIM) & (lane < (hh + 1) * HEAD_DIM)
            for hh in range(HEADS_PER_TILE)]


def _stack_heads(q_pair, in_head):
    zero = jnp.zeros_like(q_pair)
    return jnp.concatenate([jnp.where(m, q_pair, zero) for m in in_head], axis=0)


def _unstack_heads(o2, in_head):
    m_rows = o2.shape[0] // HEADS_PER_TILE
    o = o2[0:m_rows]
    for hh in range(1, HEADS_PER_TILE):
        o = jnp.where(in_head[hh], o2[hh * m_rows:(hh + 1) * m_rows], o)
    return o


def _pad_rows(a, rows):
    return jnp.concatenate([a, jnp.zeros((rows - a.shape[0], a.shape[1]), a.dtype)], axis=0)


def _inproj_kernel(*refs, n_prev):
    x_ref, g_ref, w_ref = refs[:3]
    prev = refs[3:3 + (4 if n_prev else 0)]
    (qa_ref, ga_ref, qb_ref, gb_ref, ka16_ref, va16_ref, kb16_ref, vb16_ref) = refs[
        3 + len(prev):11 + len(prev)]
    stacked = refs[11 + len(prev):]
    hn = _rms_scale(x_ref[...], g_ref[...]).astype(BF16)
    gw = qa_ref.shape[-1]

    def proj(n):
        return _nn_dot(hn, w_ref[:, n * gw:(n + 1) * gw])

    qa_ref[...] = (proj(0) * ATTN_SCALE).astype(BF16)
    ga_ref[...] = proj(3)
    qb_ref[...] = (proj(4) * ATTN_SCALE).astype(BF16)
    gb_ref[...] = proj(7)
    for n, st_ref, b16_ref in zip((1, 2, 5, 6), stacked, (ka16_ref, va16_ref, kb16_ref, vb16_ref)):
        u = proj(n)
        st_ref[n_prev] = u
        b16_ref[...] = u.astype(BF16)
    for k in range(len(prev)):
        for dpt in range(n_prev):
            stacked[k][dpt] = prev[k][dpt]


def _inproj(x2d, g_pre, w16, prev, tm):
    t, d = x2d.shape
    d_in = w16.shape[1]
    gw = d_in // 8
    n_prev = prev[0].shape[0] if prev else 0
    row = lambda i: (i, 0)
    row3 = lambda i: (0, i, 0)
    const = lambda i: (0, 0)
    f32_out = jax.ShapeDtypeStruct((t, gw), F32)
    b16_out = jax.ShapeDtypeStruct((t, gw), BF16)
    st_out = jax.ShapeDtypeStruct((n_prev + 1, t, gw), F32)
    out_shape = (b16_out, f32_out, b16_out, f32_out, b16_out, b16_out, b16_out, b16_out,
                 st_out, st_out, st_out, st_out)
    return pl.pallas_call(
        functools.partial(_inproj_kernel, n_prev=n_prev),
        out_shape=out_shape,
        grid=(t // tm,),
        in_specs=[pl.BlockSpec((tm, d), row),
                  pl.BlockSpec((1, d), const),
                  pl.BlockSpec((d, d_in), const)]
                 + [pl.BlockSpec((n_prev, tm, gw), row3)] * len(prev),
        out_specs=(pl.BlockSpec((tm, gw), row),) * 8
                  + (pl.BlockSpec((n_prev + 1, tm, gw), row3),) * 4,
        compiler_params=_params(1),
        name="inproj_sample",
    )(x2d, g_pre.reshape(1, d), w16, *prev)


def _inproj_prompt_kernel(*refs, n_prev, keep_first):
    x_ref, g_ref, w_ref = refs[:3]
    prev = refs[3:3 + (4 if n_prev else 0)]
    (qa_ref, va16_ref, ga_ref, qb_ref, vb16_ref, gb_ref, kat16_ref, kbt16_ref,
     kb_st_ref, vb_st_ref, ka_st_ref, va_st_ref) = refs[3 + len(prev):]
    j = pl.program_id(1)
    gw = qa_ref.shape[-1]
    hn = _rms_scale(x_ref[...], g_ref[...]).astype(BF16)

    def proj(n):
        return _nn_dot(hn, w_ref[:, n * gw:(n + 1) * gw])

    kat, kbt = proj(1).T, proj(5).T
    vb = proj(6)
    vbt = vb.T
    vb16_ref[...] = vb.astype(BF16)
    for u in range(kat16_ref.shape[0]):
        blk = slice(u * KEY_BLOCK, (u + 1) * KEY_BLOCK)
        kat16_ref[u] = kat[:, blk].astype(BF16)
        kbt16_ref[u] = kbt[:, blk].astype(BF16)
    for dpt in range(n_prev):
        kb_st_ref[dpt] = prev[0][dpt]
        vb_st_ref[dpt] = prev[1][dpt]
    kb_st_ref[n_prev] = kbt
    vb_st_ref[n_prev] = vbt

    qa_ref[...] = (proj(0) * ATTN_SCALE).astype(BF16)
    va = proj(2)
    va16_ref[...] = va.astype(BF16)
    ga_ref[...] = proj(3)
    qb_ref[...] = (proj(4) * ATTN_SCALE).astype(BF16)
    gb_ref[...] = proj(7)

    @pl.when(j >= keep_first)
    def _():
        for dpt in range(n_prev):
            ka_st_ref[dpt] = prev[2][dpt]
            va_st_ref[dpt] = prev[3][dpt]
        ka_st_ref[n_prev] = kat
        va_st_ref[n_prev] = va.T


def _inproj_prompt(x3d, g_pre, w_all16, layer, prev, keep):
    bsz, seq, d = x3d.shape
    tm = INPROJ_ROWS
    gw = w_all16.shape[2] // 8
    n_prev = prev[0].shape[0] if prev else 0
    assert seq % tm == 0 and keep % tm == 0 and tm % KEY_BLOCK == 0
    keep_first = (seq - keep) // tm
    tok = lambda b, j: (b, j, 0)
    const = lambda b, j: (0, 0)
    ktb = lambda b, j: (b, j, 0, 0)
    st_all = lambda b, j: (0, b, 0, j)
    st_keep = lambda b, j: (0, b, 0, jnp.maximum(j - keep_first, 0))
    tok_b16 = jax.ShapeDtypeStruct((bsz, seq, gw), BF16)
    tok_f32 = jax.ShapeDtypeStruct((bsz, seq, gw), F32)
    kt_b16 = jax.ShapeDtypeStruct((bsz, seq // KEY_BLOCK, gw, KEY_BLOCK), BF16)
    st_f32 = jax.ShapeDtypeStruct((n_prev + 1, bsz, gw, seq), F32)
    keep_f32 = jax.ShapeDtypeStruct((n_prev + 1, bsz, gw, keep), F32)
    out_shape = (tok_b16, tok_b16, tok_f32, tok_b16, tok_b16, tok_f32, kt_b16, kt_b16,
                 st_f32, st_f32, keep_f32, keep_f32)
    tok_spec = pl.BlockSpec((None, tm, gw), tok)
    kt_spec = pl.BlockSpec((None, tm // KEY_BLOCK, gw, KEY_BLOCK), ktb)
    out_specs = (tok_spec,) * 6 + (kt_spec,) * 2 + (
        pl.BlockSpec((n_prev + 1, None, gw, tm), st_all),
        pl.BlockSpec((n_prev + 1, None, gw, tm), st_all),
        pl.BlockSpec((n_prev + 1, None, gw, tm), st_keep),
        pl.BlockSpec((n_prev + 1, None, gw, tm), st_keep))
    in_specs = [pl.BlockSpec((None, tm, d), tok),
                pl.BlockSpec((1, d), const),
                pl.BlockSpec((None,) + w_all16.shape[1:], lambda b, j: (layer, 0, 0),
                             pipeline_mode=pl.Buffered(1))]
    if n_prev:
        in_specs += [pl.BlockSpec((n_prev, None, gw, tm), st_all),
                     pl.BlockSpec((n_prev, None, gw, tm), st_all),
                     pl.BlockSpec((n_prev, None, gw, tm), st_keep),
                     pl.BlockSpec((n_prev, None, gw, tm), st_keep)]
    return pl.pallas_call(
        functools.partial(_inproj_prompt_kernel, n_prev=n_prev, keep_first=keep_first),
        out_shape=out_shape,
        grid=(bsz, seq // tm),
        in_specs=in_specs,
        out_specs=out_specs,
        compiler_params=_params(2),
        name="inproj_prompt",
    )(x3d, g_pre.reshape(1, d), w_all16, *prev)


def _finish_kernel(oa_ref, ob_ref, h_ref, p_ref, woa_ref, wob_ref, gpost_ref, wg_ref, wp_ref,
                   out_ref):
    y = _nn_dot(oa_ref[...], woa_ref[...]) + _nn_dot(ob_ref[...], wob_ref[...])
    h1 = h_ref[...] + _rms_scale(y, gpost_ref[...])
    gate_logit = _nn_dot(h1.astype(BF16), wg_ref[...])
    gate = 1.0 / (1.0 + jnp.exp(-gate_logit))
    emb = _nn_dot(p_ref[...].astype(BF16), wp_ref[...])
    out_ref[...] = h1 + gate * emb


def _finish(oa, ob, h2d, p_all, layer, wo16, g_post, wg16, wp16, tm):
    t, d = h2d.shape
    da = oa.shape[1]
    dp = p_all.shape[2]
    row = lambda i: (i, 0)
    const = lambda i: (0, 0)
    return pl.pallas_call(
        _finish_kernel,
        out_shape=jax.ShapeDtypeStruct((t, d), F32),
        grid=(t // tm,),
        in_specs=[pl.BlockSpec((tm, da), row),
                  pl.BlockSpec((tm, da), row),
                  pl.BlockSpec((tm, d), row),
                  pl.BlockSpec((None, tm, dp), lambda i: (layer, i, 0)),
                  pl.BlockSpec((da, d), const),
                  pl.BlockSpec((da, d), lambda i: (1, 0)),
                  pl.BlockSpec((1, d), const),
                  pl.BlockSpec((d, d), const),
                  pl.BlockSpec((dp, d), const)],
        out_specs=pl.BlockSpec((tm, d), row),
        compiler_params=_params(1),
        name="finish",
    )(oa, ob, h2d, p_all, wo16, wo16, g_post.reshape(1, d), wg16, wp16)


def _build_band_bias(tbl_ref, maskadd_ref, bias_ref):
    m_rows = maskadd_ref.shape[0]
    n_heads = bias_ref.shape[0] // m_rows
    tbl = tbl_ref[...]
    p1 = tbl.astype(BF16)
    r1 = tbl - p1.astype(F32)
    p2 = r1.astype(BF16)
    p3 = (r1 - p2.astype(F32)).astype(BF16)
    r_idx = lax.broadcasted_iota(jnp.int32, (BIAS_TABLE_PAD, BIAS_EXT), 0)
    m_idx = lax.broadcasted_iota(jnp.int32, (BIAS_TABLE_PAD, BIAS_EXT), 1)
    rel = jnp.clip(m_idx - (m_rows - 1) - BAND_PAST, -REL_CLIP, REL_CLIP) + REL_CLIP
    onehot = (r_idx == rel).astype(BF16)
    t_ext = _nn_dot(p1, onehot) + _nn_dot(p2, onehot) + _nn_dot(p3, onehot)
    ql = lax.broadcasted_iota(jnp.int32, (m_rows, BIAS_EXT), 0)
    n_bits = int(m_rows - 1).bit_length()
    for h in range(n_heads):
        x = jnp.broadcast_to(t_ext[h:h + 1, :], (m_rows, BIAS_EXT))
        x = pltpu.roll(x, BIAS_EXT - (m_rows - 1), 1)
        for b in range(n_bits):
            x = jnp.where(((ql >> b) & 1) == 1, pltpu.roll(x, 1 << b, 1), x)
        bias_ref[h * m_rows:(h + 1) * m_rows, :] = x[:, :BAND_WINDOW] + maskadd_ref[...]


def _band_attend(q_ref, g_ref, bias_ref, out_ref, m_rows, windows):
    n_tiles = q_ref.shape[1] // LANES
    in_head = _head_masks(m_rows)
    kl = lax.broadcasted_iota(jnp.int32, (1, BAND_WINDOW), 1)
    lane_sl = [slice(p * LANES, (p + 1) * LANES) for p in range(n_tiles)]
    row_sl = [slice(j * m_rows, (j + 1) * m_rows) for j in range(len(windows))]
    rows2 = HEADS_PER_TILE * m_rows
    items = [(j, p) for j in range(len(windows)) for p in range(n_tiles)]

    def item_scores(n):
        j, p = items[n]
        return windows[j][0](lane_sl[p], _stack_heads(q_ref[row_sl[j], lane_sl[p]], in_head))

    groups = [list(range(a, min(a + BAND_GROUP, len(items))))
              for a in range(0, len(items), BAND_GROUP)]
    s_next = [item_scores(n) for n in groups[0]]
    for gi, group in enumerate(groups):
        ss = [s_next[i] + bias_ref[items[n][1] * rows2:(items[n][1] + 1) * rows2, :]
              for i, n in enumerate(group)]
        if gi + 1 < len(groups):
            s_next = [item_scores(n) for n in groups[gi + 1]]
        es = []
        for i, n in enumerate(group):
            s, key_lo = ss[i], windows[items[n][0]][2]
            if key_lo is not None:
                s = jnp.where(kl >= key_lo, s, NEG_INF)
            es.append(jnp.exp(s - jnp.max(s, axis=-1, keepdims=True)).astype(BF16))
        for i, n in enumerate(group):
            j, p = items[n]
            o_ext = windows[j][1](lane_sl[p], es[i])
            o2 = o_ext[:, :LANES] * (1.0 / o_ext[:, LANES:])
            o_pair = _unstack_heads(o2, in_head)
            out_ref[row_sl[j], lane_sl[p]] = (
                o_pair * _silu(g_ref[row_sl[j], lane_sl[p]])).astype(BF16)


def _band_prompt_kernel(q_ref, g_ref, kt_ref, v_ref, tbl_ref, maskadd_ref, out_ref, bias_ref):
    b = pl.program_id(0)
    step = pl.program_id(1)
    n_prev = BAND_PAST // KEY_BLOCK
    n_blk = BAND_WINDOW // KEY_BLOCK

    @pl.when((b == 0) & (step == 0))
    def _():
        _build_band_bias(tbl_ref, maskadd_ref, bias_ref)

    def window(c, at_start):
        first = c - n_prev
        blocks = [max(first + t, 0) if at_start else first + t for t in range(n_blk)]

        def scores(sl, q2):
            kt = jnp.concatenate([kt_ref[blk, sl, :] for blk in blocks], axis=1)
            return _nn_dot(q2, kt)

        def weighted_values(sl, e16):
            if at_start:
                v = jnp.concatenate(
                    [v_ref[blk * KEY_BLOCK:(blk + 1) * KEY_BLOCK, sl] for blk in blocks], axis=0)
            else:
                v = v_ref[pl.ds(pl.multiple_of(first * KEY_BLOCK, KEY_BLOCK), BAND_WINDOW), sl]
            return _nn_dot(e16, jnp.concatenate([v, jnp.ones_like(v)], axis=1))

        return scores, weighted_values, (n_prev - c) * KEY_BLOCK if at_start else None

    @pl.when(step == 0)
    def _():
        _band_attend(q_ref, g_ref, bias_ref, out_ref, BAND_Q_ROWS,
                     [window(j, True) for j in range(BAND_STEP_BLOCKS)])

    @pl.when(step > 0)
    def _():
        _band_attend(q_ref, g_ref, bias_ref, out_ref, BAND_Q_ROWS,
                     [window(step * BAND_STEP_BLOCKS + j, False)
                      for j in range(BAND_STEP_BLOCKS)])


def _band_static_mask(q_pos, k_pos):
    qc = q_pos // CHUNK
    kc = k_pos // CHUNK
    ok = (kc[None, :] <= qc[:, None]) & (kc[None, :] >= qc[:, None] - N_PREV_CHUNKS)
    return np.where(ok, 0.0, NEG_INF).astype(np.float32)


def _pad_table(rel_bias):
    return jnp.pad(rel_bias, ((0, 0), (0, BIAS_TABLE_PAD - rel_bias.shape[1])))


def _band_prompt(q16, g, kt16, v16, rel_bias):
    bsz, seq, da = q16.shape
    n_heads = da // HEAD_DIM
    step_rows = BAND_STEP_BLOCKS * BAND_Q_ROWS
    assert BAND_Q_ROWS == KEY_BLOCK and seq % step_rows == 0
    assert BAND_STEP_BLOCKS == BAND_PAST // KEY_BLOCK
    q_pos = BAND_PAST + np.arange(BAND_Q_ROWS)
    k_pos = np.arange(BAND_WINDOW)
    maskadd = jnp.asarray(_band_static_mask(q_pos, k_pos))
    blk = lambda b, c: (b, c, 0)
    whole = lambda b, c: (b, 0, 0)
    whole4 = lambda b, c: (b, 0, 0, 0)
    const = lambda b, c: (0, 0)
    return pl.pallas_call(
        _band_prompt_kernel,
        out_shape=jax.ShapeDtypeStruct((bsz, seq, da), BF16),
        grid=(bsz, seq // step_rows),
        in_specs=[pl.BlockSpec((None, step_rows, da), blk),
                  pl.BlockSpec((None, step_rows, da), blk),
                  pl.BlockSpec((None, seq // KEY_BLOCK, da, KEY_BLOCK), whole4),
                  pl.BlockSpec((None, seq, da), whole),
                  pl.BlockSpec((n_heads, BIAS_TABLE_PAD), const),
                  pl.BlockSpec((BAND_Q_ROWS, BAND_WINDOW), const)],
        out_specs=pl.BlockSpec((None, step_rows, da), blk),
        scratch_shapes=[pltpu.VMEM((n_heads * BAND_Q_ROWS, BAND_WINDOW), F32)],
        compiler_params=_params(2),
        name="band_prompt",
    )(q16, g, kt16, v16, _pad_table(rel_bias), maskadd)


def _band_sample_kernel(q_ref, g_ref, ckt_ref, cvt_ref, nk_ref, nv_ref, tbl_ref, maskadd_ref,
                        out_ref, bias_ref):
    b = pl.program_id(0)
    la = ckt_ref.shape[1]

    @pl.when(b == 0)
    def _():
        _build_band_bias(tbl_ref, maskadd_ref, bias_ref)

    def scores(sl, q2):
        s_cache = _nn_dot(q2, ckt_ref[sl, :].astype(BF16))
        s_new = _nt_dot(q2, _pad_rows(nk_ref[:, sl], KEY_BLOCK))
        return jnp.concatenate([s_cache, s_new], axis=1)

    def weighted_values(sl, e16):
        cvt = cvt_ref[sl, :].astype(BF16)
        nv = _pad_rows(nv_ref[:, sl], KEY_BLOCK)
        return (_nt_dot(e16[:, :la], jnp.concatenate([cvt, jnp.ones_like(cvt)], axis=0))
                + _nn_dot(e16[:, la:], jnp.concatenate([nv, jnp.ones_like(nv)], axis=1)))

    _band_attend(q_ref, g_ref, bias_ref, out_ref, q_ref.shape[0],
                 [(scores, weighted_values, None)])


def _band_sample(q16, g, cache_kt, cache_vt, layer, nk16, nv16, rel_bias, past):
    bsz, t, da = q16.shape
    la = cache_kt.shape[3]
    n_heads = da // HEAD_DIM
    assert la == BAND_PAST and la + KEY_BLOCK == BAND_WINDOW and t <= KEY_BLOCK and t % 16 == 0
    q_pos = past + np.arange(t)
    k_pos = past - la + np.arange(BAND_WINDOW)
    maskadd = _band_static_mask(q_pos, k_pos)
    maskadd[:, la + t:] = NEG_INF
    maskadd = jnp.asarray(maskadd)
    blk = lambda b: (b, 0, 0)
    const = lambda b: (0, 0)
    return pl.pallas_call(
        _band_sample_kernel,
        out_shape=jax.ShapeDtypeStruct((bsz, t, da), BF16),
        grid=(bsz,),
        in_specs=[pl.BlockSpec((None, t, da), blk),
                  pl.BlockSpec((None, t, da), blk),
                  pl.BlockSpec((None, None, da, la), lambda b: (layer, b, 0, 0)),
                  pl.BlockSpec((None, None, da, la), lambda b: (layer, b, 0, 0)),
                  pl.BlockSpec((None, t, da), blk),
                  pl.BlockSpec((None, t, da), blk),
                  pl.BlockSpec((n_heads, BIAS_TABLE_PAD), const),
                  pl.BlockSpec((t, BAND_WINDOW), const)],
        out_specs=pl.BlockSpec((None, t, da), blk),
        scratch_shapes=[pltpu.VMEM((n_heads * t, BAND_WINDOW), F32)],
        compiler_params=_params(1),
        name="band_sample",
    )(q16, g, cache_kt, cache_vt, nk16, nv16, _pad_table(rel_bias), maskadd)


def _suffix_matrix(width):
    j = np.arange(width)[:, None]
    s = np.arange(width)[None, :]
    tri = (j >= s).astype(np.float32)
    return jnp.asarray(np.concatenate([tri, tri], axis=0), dtype=BF16)


def _sb_tiles(q2s, tiles, carry_ref, acc_ref, first, k_is_t, v_is_t):
    n = len(q2s)
    zs = [[(_nn_dot if k_is_t else _nt_dot)(q2s[i], ks[i]) for i in range(n)]
          for ks, _, _, _ in tiles]
    split = []
    for t, (_, _, _, causal) in enumerate(tiles):
        split.append([])
        for z in zs[t]:
            sp = _softplus(z)
            if causal is not None:
                sp = jnp.where(causal, sp, 0.0)
            hi = sp.astype(BF16)
            lo = (sp - hi.astype(F32)).astype(BF16)
            split[t].append(jnp.concatenate([hi, lo], axis=1))
    sums = [[_nn_dot(s, tri) for s in split[t]] for t, (_, _, tri, _) in enumerate(tiles)]
    carries = [None if first else carry_ref[i] for i in range(n)]
    ws = []
    for t, (_, _, _, causal) in enumerate(tiles):
        ws.append([])
        for i in range(n):
            log_w = zs[t][i] - sums[t][i]
            total = sums[t][i][:, 0:1]
            if carries[i] is not None:
                log_w = log_w - carries[i]
                total = total + carries[i]
            w = jnp.exp(log_w)
            if causal is not None:
                w = jnp.where(causal, w, 0.0)
            ws[t].append(w.astype(BF16))
            carries[i] = total
    for i in range(n):
        acc = None if first else acc_ref[i]
        for t, (_, vs, _, _) in enumerate(tiles):
            part = (_nt_dot if v_is_t else _nn_dot)(ws[t][i], vs[i])
            acc = part if acc is None else acc + part
        acc_ref[i] = acc
        carry_ref[i] = carries[i]
    return carries


def _sb_setup(q_ref, td):
    tq, width = q_ref.shape
    lane_sl = [slice(p * LANES, (p + 1) * LANES) for p in range(width // LANES)]
    in_head = _head_masks(tq)
    rows2 = HEADS_PER_TILE * tq
    row = lax.broadcasted_iota(jnp.int32, (rows2, td), 0) % tq
    col = lax.broadcasted_iota(jnp.int32, (rows2, td), 1)
    causal = col < row
    q2s = [_stack_heads(q_ref[:, sl], in_head) for sl in lane_sl]
    return lane_sl, in_head, causal, q2s


def _sb_min_carry(carries):
    m = carries[0]
    for c in carries[1:]:
        m = jnp.minimum(m, c)
    return jnp.min(m)


def _sb_store(acc_ref, lane_sl, in_head, g_ref, out_ref):
    for n, sl in enumerate(lane_sl):
        o_pair = _unstack_heads(acc_ref[n], in_head)
        out_ref[:, sl] = (o_pair * _silu(g_ref[:, sl])).astype(BF16)


def _sb_scratch(rows, width):
    n = width // LANES
    return [pltpu.VMEM((n, HEADS_PER_TILE * rows, 1), F32),
            pltpu.VMEM((n, HEADS_PER_TILE * rows, LANES), F32)]


def _softplus(z):
    return jnp.maximum(z, 0.0) + jnp.log(1.0 + jnp.exp2(jnp.abs(z) * NEG_LOG2E))


def _sb_window(q2s, kts, vs, tri, causal, valid, slots, carry_ref, acc_ref, first):
    n = len(q2s)
    n_blk = kts[0].shape[1] // KEY_BLOCK
    newest_first = list(reversed(range(n_blk)))
    zs = [_nn_dot(q2s[c], kts[c]) for c in range(n)]

    def block_mask(c, b):
        mask = causal if b == n_blk - 1 else None
        if valid[c] is not None:
            mask = valid[c] if mask is None else mask & valid[c]
        return mask

    split = []
    for c in range(n):
        split.append({})
        for b in newest_first:
            sp = _softplus(zs[c][:, b * KEY_BLOCK:(b + 1) * KEY_BLOCK])
            mask = block_mask(c, b)
            if mask is not None:
                sp = jnp.where(mask, sp, 0.0)
            hi = sp.astype(BF16)
            lo = (sp - hi.astype(F32)).astype(BF16)
            split[c][b] = jnp.concatenate([hi, lo], axis=1)
    sums = [{b: _nn_dot(split[c][b], tri) for b in newest_first} for c in range(n)]
    carries, ws = [], []
    for c in range(n):
        carry = None if first else carry_ref[slots[c]]
        pieces = [None] * n_blk
        for b in newest_first:
            log_w = zs[c][:, b * KEY_BLOCK:(b + 1) * KEY_BLOCK] - sums[c][b]
            total = sums[c][b][:, 0:1]
            if carry is not None:
                log_w = log_w - carry
                total = total + carry
            w = jnp.exp(log_w)
            mask = block_mask(c, b)
            if mask is not None:
                w = jnp.where(mask, w, 0.0)
            pieces[b] = w.astype(BF16)
            carry = total
        carries.append(carry)
        ws.append(pieces[0] if n_blk == 1 else jnp.concatenate(pieces, axis=1))
    for c in range(n):
        part = _nn_dot(ws[c], vs[c])
        acc_ref[slots[c]] = part if first else acc_ref[slots[c]] + part
        carry_ref[slots[c]] = carries[c]
    return carries


def _sb_prompt_kernel(q_ref, g_ref, kt_ref, v_ref, tri_ref, out_ref, carry_ref, acc_ref):
    step = pl.program_id(1)
    r, n_qb, n_win = SB_Q_ROWS, SB_STEP_BLOCKS, SB_WINDOW_BLOCKS
    lane_sl = [slice(p * LANES, (p + 1) * LANES) for p in range(q_ref.shape[1] // LANES)]
    n_lane = len(lane_sl)
    in_head = _head_masks(r)
    row = lax.broadcasted_iota(jnp.int32, (HEADS_PER_TILE * r, KEY_BLOCK), 0) % r
    col = lax.broadcasted_iota(jnp.int32, (HEADS_PER_TILE * r, KEY_BLOCK), 1)
    causal = col < row
    tri = tri_ref[...]
    chains = [(a, p) for a in range(n_qb) for p in range(n_lane)]
    q2s = [_stack_heads(q_ref[a * r:(a + 1) * r, lane_sl[p]], in_head) for a, p in chains]

    def keys(first_blk, n_blk, sl):
        return jnp.concatenate([kt_ref[first_blk + u, sl, :] for u in range(n_blk)], axis=1)

    def values(first_blk, n_blk, sl):
        if isinstance(first_blk, int):
            return v_ref[first_blk * KEY_BLOCK:(first_blk + n_blk) * KEY_BLOCK, sl]
        start = pl.multiple_of(first_blk * KEY_BLOCK, KEY_BLOCK)
        return v_ref[pl.ds(start, n_blk * KEY_BLOCK), sl]

    def run_windows(which, first_blk, n_blk):
        for a in sorted({chains[c][0] for c in which}):
            mine = [c for c in which if chains[c][0] == a]
            _sb_window([q2s[c] for c in mine],
                       [keys(first_blk(a), n_blk(a), lane_sl[chains[c][1]]) for c in mine],
                       [values(first_blk(a), n_blk(a), lane_sl[chains[c][1]]) for c in mine],
                       tri, causal, [None] * len(mine), mine, carry_ref, acc_ref, True)

    every = list(range(len(chains)))

    @pl.when(step == 0)
    def _():
        run_windows(every, lambda a: max(a + 1 - n_win, 0), lambda a: min(a + 1, n_win))

    @pl.when(step > 0)
    def _():
        _sb_window(q2s,
                   [keys(step * n_qb + a + 1 - n_win, n_win, lane_sl[p]) for a, p in chains],
                   [values(step * n_qb + a + 1 - n_win, n_win, lane_sl[p]) for a, p in chains],
                   tri, causal, [None] * len(chains), every, carry_ref, acc_ref, True)

    n_more = step * n_qb + n_qb - n_win

    def cond(loop_state):
        it, smallest = loop_state
        return (it < n_more) & (smallest < SB_UNDERFLOW)

    def body(loop_state):
        it, _ = loop_state
        blks = [step * n_qb + a - n_win - it for a in range(n_qb)]
        valid = [None if a == n_qb - 1 else blks[a] >= 0 for a in range(n_qb)]
        blks = [jnp.maximum(blk, 0) for blk in blks]
        carries = _sb_window(q2s,
                             [keys(blks[a], 1, lane_sl[p]) for a, p in chains],
                             [values(blks[a], 1, lane_sl[p]) for a, p in chains],
                             tri, None, [valid[a] for a, _ in chains], every,
                             carry_ref, acc_ref, False)
        return it + 1, _sb_min_carry(carries)

    lax.while_loop(cond, body,
                   (jnp.int32(0), _sb_min_carry([carry_ref[c] for c in every])))
    for c, (a, p) in enumerate(chains):
        rows = slice(a * r, (a + 1) * r)
        o_pair = _unstack_heads(acc_ref[c], in_head)
        out_ref[rows, lane_sl[p]] = (o_pair * _silu(g_ref[rows, lane_sl[p]])).astype(BF16)


def _sb_prompt(q16, g, kt16, v16):
    bsz, seq, db = q16.shape
    step_rows = SB_STEP_BLOCKS * SB_Q_ROWS
    n_chains = SB_STEP_BLOCKS * (db // LANES)
    assert SB_Q_ROWS == KEY_BLOCK and seq % step_rows == 0 and db % LANES == 0
    qblk = lambda b, i: (b, i, 0)
    whole = lambda b, i: (b, 0, 0)
    whole4 = lambda b, i: (b, 0, 0, 0)
    const = lambda b, i: (0, 0)
    return pl.pallas_call(
        _sb_prompt_kernel,
        out_shape=jax.ShapeDtypeStruct((bsz, seq, db), BF16),
        grid=(bsz, seq // step_rows),
        in_specs=[pl.BlockSpec((None, step_rows, db), qblk),
                  pl.BlockSpec((None, step_rows, db), qblk),
                  pl.BlockSpec((None, seq // KEY_BLOCK, db, KEY_BLOCK), whole4),
                  pl.BlockSpec((None, seq, db), whole),
                  pl.BlockSpec((2 * KEY_BLOCK, KEY_BLOCK), const)],
        out_specs=pl.BlockSpec((None, step_rows, db), qblk),
        scratch_shapes=[pltpu.VMEM((n_chains, HEADS_PER_TILE * SB_Q_ROWS, 1), F32),
                        pltpu.VMEM((n_chains, HEADS_PER_TILE * SB_Q_ROWS, LANES), F32)],
        compiler_params=_params(2),
        name="sb_prompt",
    )(q16, g, kt16, v16, _suffix_matrix(KEY_BLOCK))


def _sb_sample_kernel(q_ref, g_ref, nk_ref, nv_ref, ckt_ref, cvt_ref, trid_ref, trif_ref,
                      out_ref, carry_ref, acc_ref):
    td = trid_ref.shape[1]
    tk = trif_ref.shape[1]
    n_full = ckt_ref.shape[1] // tk
    trif = trif_ref[...]
    lane_sl, in_head, causal, q2s = _sb_setup(q_ref, td)

    carries = _sb_tiles(q2s, [([_pad_rows(nk_ref[:, sl], td) for sl in lane_sl],
                               [_pad_rows(nv_ref[:, sl], td) for sl in lane_sl],
                               trid_ref[...], causal)], carry_ref, acc_ref, True, False, False)
    smallest = _sb_min_carry(carries)

    for tile in reversed(range(n_full)):
        cols = slice(tile * tk, (tile + 1) * tk)

        def step(cols=cols):
            kts = [ckt_ref[sl, cols].astype(BF16) for sl in lane_sl]
            vts = [cvt_ref[sl, cols].astype(BF16) for sl in lane_sl]
            return _sb_min_carry(_sb_tiles(q2s, [(kts, vts, trif, None)], carry_ref, acc_ref,
                                           False, True, True))

        smallest = lax.cond(smallest < SB_UNDERFLOW, step, lambda s=smallest: s)
    _sb_store(acc_ref, lane_sl, in_head, g_ref, out_ref)


def _sb_sample(q16, g, nk16, nv16, cache_kt, cache_vt, layer):
    bsz, t, db = q16.shape
    past = cache_kt.shape[3]
    width, tk, td = SB_WIDTH, SB_TILE, KEY_BLOCK
    assert t <= td and t % 16 == 0 and past % tk == 0 and db % width == 0
    blk = lambda b, p: (b, 0, p)
    cblk = lambda b, p: (layer, b, p, 0)
    const = lambda b, p: (0, 0)
    return pl.pallas_call(
        _sb_sample_kernel,
        out_shape=jax.ShapeDtypeStruct((bsz, t, db), BF16),
        grid=(bsz, db // width),
        in_specs=[pl.BlockSpec((None, t, width), blk),
                  pl.BlockSpec((None, t, width), blk),
                  pl.BlockSpec((None, t, width), blk),
                  pl.BlockSpec((None, t, width), blk),
                  pl.BlockSpec((None, None, width, past), cblk),
                  pl.BlockSpec((None, None, width, past), cblk),
                  pl.BlockSpec((2 * td, td), const),
                  pl.BlockSpec((2 * tk, tk), const)],
        out_specs=pl.BlockSpec((None, t, width), blk),
        scratch_shapes=_sb_scratch(t, width),
        compiler_params=_params(2),
        name="sb_sample",
    )(q16, g, nk16, nv16, cache_kt, cache_vt, _suffix_matrix(td), _suffix_matrix(tk))


def _row_tile(t):
    return ROW_TILE if t % ROW_TILE == 0 else t


def _cache_transposed(cache):
    l, b, s, h, dh = cache.shape
    return jnp.transpose(cache, (0, 1, 3, 4, 2)).reshape(l, b, h * dh, s)


def _cache_layout(stacked, n_heads):
    l, b, hd, s = stacked.shape
    return jnp.transpose(stacked.reshape(l, b, n_heads, hd // n_heads, s), (0, 1, 4, 2, 3))


def kernel(x_prompt, x_sample, p_prompt, p_sample, cache_a_k, cache_a_v, cache_b_k, cache_b_v,
           g_pre, w_in, rel_bias, w_out, g_post, w_ple, w_ple_gate):
    depth = w_in.shape[0]
    bsz, seq, d = x_prompt.shape
    dbsz, dseq, _ = x_sample.shape
    past = cache_b_k.shape[2]
    keep = min(BAND_PAST, seq)
    da = w_in.shape[2] // 8
    n_heads = da // HEAD_DIM
    tp, ts = bsz * seq, dbsz * dseq

    hp = x_prompt
    hs = x_sample.reshape(ts, d)
    p_prompt = p_prompt.reshape(depth, tp, -1)
    p_sample = p_sample.reshape(depth, ts, -1)
    cache_a_kt, cache_a_vt, cache_b_kt, cache_b_vt = (
        _cache_transposed(c) for c in (cache_a_k, cache_a_v, cache_b_k, cache_b_v))
    stacked = ()
    sample_stacked = ()
    w_in16, w_out16, w_gate16, w_ple16 = (
        w.astype(BF16) for w in (w_in, w_out, w_ple_gate, w_ple))
    for i in range(depth):
        w16 = w_in16[i]
        wo16, wg16, wp16 = w_out16[i], w_gate16[i], w_ple16[i]

        (qa, va16, ga, qb, vb16, gb, kat16, kbt16, *stacked) = _inproj_prompt(
            hp, g_pre[i], w_in16, i, tuple(stacked), keep)
        oa = _band_prompt(qa, ga, kat16, va16, rel_bias[i])
        ob = _sb_prompt(qb, gb, kbt16, vb16)
        hp = _finish(oa.reshape(tp, da), ob.reshape(tp, da), hp.reshape(tp, d),
                     p_prompt, i, wo16, g_post[i], wg16, wp16,
                     _row_tile(tp)).reshape(bsz, seq, d)

        (qa, ga, qb, gb, ka16, va16, kb16, vb16, *sample_stacked) = _inproj(
            hs, g_pre[i], w16, tuple(sample_stacked), _row_tile(ts))
        s3 = lambda a: a.reshape(dbsz, dseq, da)
        oa = _band_sample(s3(qa), s3(ga), cache_a_kt, cache_a_vt, i, s3(ka16), s3(va16),
                          rel_bias[i], past)
        ob = _sb_sample(s3(qb), s3(gb), s3(kb16), s3(vb16), cache_b_kt, cache_b_vt, i)
        hs = _finish(oa.reshape(ts, da), ob.reshape(ts, da), hs, p_sample, i,
                     wo16, g_post[i], wg16, wp16, _row_tile(ts))

    kb_st, vb_st, ka_st, va_st = stacked
    s5 = lambda a: a.reshape(depth, dbsz, dseq, n_heads, HEAD_DIM)
    return (hp, hs.reshape(dbsz, dseq, d),
            _cache_layout(ka_st, n_heads), _cache_layout(va_st, n_heads),
            _cache_layout(kb_st, n_heads), _cache_layout(vb_st, n_heads),
            *(s5(a) for a in sample_stacked))
```

```python
import functools

import numpy as np
import jax
import jax.numpy as jnp
from jax import lax
from jax.experimental import pallas as pl
from jax.experimental.pallas import tpu as pltpu

HEAD_DIM = 64
CHUNK = 64
N_PREV_CHUNKS = 8
BAND_PAST = N_PREV_CHUNKS * CHUNK
REL_CLIP = 128
RMS_EPS = 1e-6
NEG_INF = -1e30
ATTN_SCALE = HEAD_DIM ** -0.5
NEG_LOG2E = -1.4426950408889634

LANES = 128
HEADS_PER_TILE = LANES // HEAD_DIM
VMEM_LIMIT_BYTES = 56 * 1024 * 1024

KEY_BLOCK = LANES
BAND_Q_ROWS = 2 * CHUNK
BAND_STEP_BLOCKS = 4
BAND_GROUP = 4
BAND_WINDOW = BAND_PAST + BAND_Q_ROWS
BIAS_TABLE_PAD = 384
BIAS_EXT = 768

SB_TILE = 256
SB_Q_ROWS = 128
SB_STEP_BLOCKS = 2
SB_WINDOW_BLOCKS = 3
SB_WIDTH = 512
SB_UNDERFLOW = 105.0

INPROJ_ROWS = 512
ROW_TILE = 1024

BF16 = jnp.bfloat16
F32 = jnp.float32


def _params(n_axes):
    return pltpu.CompilerParams(
        dimension_semantics=("arbitrary",) * n_axes,
        vmem_limit_bytes=VMEM_LIMIT_BYTES)


def _nt_dot(a, b):
    return lax.dot_general(a, b, (((1,), (1,)), ((), ())), preferred_element_type=F32)


def _nn_dot(a, b):
    return jnp.dot(a, b, preferred_element_type=F32)


def _silu(g):
    return g * (1.0 / (1.0 + jnp.exp(-g)))


def _rms_scale(x, gain):
    ms = jnp.mean(x * x, axis=-1, keepdims=True)
    return x * lax.rsqrt(ms + RMS_EPS) * gain


def _head_masks(m_rows):
    lane = lax.broadcasted_iota(jnp.int32, (m_rows, LANES), 1)
    return [(lane >= hh * HEAD_DIM) & (lane < (hh + 1) * HEAD_DIM)
            for hh in range(HEADS_PER_TILE)]


def _stack_heads(q_pair, in_head):
    zero = jnp.zeros_like(q_pair)
    return jnp.concatenate([jnp.where(m, q_pair, zero) for m in in_head], axis=0)


def _unstack_heads(o2, in_head):
    m_rows = o2.shape[0] // HEADS_PER_TILE
    o = o2[0:m_rows]
    for hh in range(1, HEADS_PER_TILE):
        o = jnp.where(in_head[hh], o2[hh * m_rows:(hh + 1) * m_rows], o)
    return o


def _pad_rows(a, rows):
    return jnp.concatenate([a, jnp.zeros((rows - a.shape[0], a.shape[1]), a.dtype)], axis=0)


def _inproj_kernel(*refs, n_prev):
    x_ref, g_ref, w_ref = refs[:3]
    prev = refs[3:3 + (4 if n_prev else 0)]
    (qa_ref, ga_ref, qb_ref, gb_ref, ka16_ref, va16_ref, kb16_ref, vb16_ref) = refs[
        3 + len(prev):11 + len(prev)]
    stacked = refs[11 + len(prev):]
    hn = _rms_scale(x_ref[...], g_ref[...]).astype(BF16)
    gw = qa_ref.shape[-1]

    def proj(n):
        return _nn_dot(hn, w_ref[:, n * gw:(n + 1) * gw])

    qa_ref[...] = (proj(0) * ATTN_SCALE).astype(BF16)
    ga_ref[...] = proj(3)
    qb_ref[...] = (proj(4) * ATTN_SCALE).astype(BF16)
    gb_ref[...] = proj(7)
    for n, st_ref, b16_ref in zip((1, 2, 5, 6), stacked, (ka16_ref, va16_ref, kb16_ref, vb16_ref)):
        u = proj(n)
        st_ref[n_prev] = u
        b16_ref[...] = u.astype(BF16)
    for k in range(len(prev)):
        for dpt in range(n_prev):
            stacked[k][dpt] = prev[k][dpt]


def _inproj(x2d, g_pre, w_all16, layer, prev, tm):
    t, d = x2d.shape
    d_in = w_all16.shape[2]
    gw = d_in // 8
    n_prev = prev[0].shape[0] if prev else 0
    row = lambda i: (i, 0)
    row3 = lambda i: (0, i, 0)
    const = lambda i: (layer, 0, 0)
    f32_out = jax.ShapeDtypeStruct((t, gw), F32)
    b16_out = jax.ShapeDtypeStruct((t, gw), BF16)
    st_out = jax.ShapeDtypeStruct((n_prev + 1, t, gw), F32)
    out_shape = (b16_out, f32_out, b16_out, f32_out, b16_out, b16_out, b16_out, b16_out,
                 st_out, st_out, st_out, st_out)
    return pl.pallas_call(
        functools.partial(_inproj_kernel, n_prev=n_prev),
        out_shape=out_shape,
        grid=(t // tm,),
        in_specs=[pl.BlockSpec((tm, d), row),
                  pl.BlockSpec((None, 1, d), const),
                  pl.BlockSpec((None, d, d_in), const)]
                 + [pl.BlockSpec((n_prev, tm, gw), row3)] * len(prev),
        out_specs=(pl.BlockSpec((tm, gw), row),) * 8
                  + (pl.BlockSpec((n_prev + 1, tm, gw), row3),) * 4,
        compiler_params=_params(1),
        name="inproj_sample",
    )(x2d, g_pre.reshape(-1, 1, d), w_all16, *prev)


def _inproj_prompt_kernel(*refs, n_prev, keep_first):
    x_ref, g_ref, w_ref = refs[:3]
    prev = refs[3:3 + (4 if n_prev else 0)]
    (qa_ref, va16_ref, ga_ref, qb_ref, vb16_ref, gb_ref, kat16_ref, kbt16_ref,
     kb_st_ref, vb_st_ref, ka_st_ref, va_st_ref) = refs[3 + len(prev):]
    j = pl.program_id(1)
    gw = qa_ref.shape[-1]
    hn = _rms_scale(x_ref[...], g_ref[...]).astype(BF16)

    def proj(n):
        return _nn_dot(hn, w_ref[:, n * gw:(n + 1) * gw])

    kat, kbt = proj(1).T, proj(5).T
    vb = proj(6)
    vbt = vb.T
    vb16_ref[...] = vb.astype(BF16)
    for u in range(kat16_ref.shape[0]):
        blk = slice(u * KEY_BLOCK, (u + 1) * KEY_BLOCK)
        kat16_ref[u] = kat[:, blk].astype(BF16)
        kbt16_ref[u] = kbt[:, blk].astype(BF16)
    for dpt in range(n_prev):
        kb_st_ref[dpt] = prev[0][dpt]
        vb_st_ref[dpt] = prev[1][dpt]
    kb_st_ref[n_prev] = kbt
    vb_st_ref[n_prev] = vbt

    qa_ref[...] = (proj(0) * ATTN_SCALE).astype(BF16)
    va = proj(2)
    va16_ref[...] = va.astype(BF16)
    ga_ref[...] = proj(3)
    qb_ref[...] = (proj(4) * ATTN_SCALE).astype(BF16)
    gb_ref[...] = proj(7)

    @pl.when(j >= keep_first)
    def _():
        for dpt in range(n_prev):
            ka_st_ref[dpt] = prev[2][dpt]
            va_st_ref[dpt] = prev[3][dpt]
        ka_st_ref[n_prev] = kat
        va_st_ref[n_prev] = va.T


def _inproj_prompt(x3d, g_pre, w_all16, layer, prev, keep):
    bsz, seq, d = x3d.shape
    tm = INPROJ_ROWS
    gw = w_all16.shape[2] // 8
    n_prev = prev[0].shape[0] if prev else 0
    assert seq % tm == 0 and keep % tm == 0 and tm % KEY_BLOCK == 0
    keep_first = (seq - keep) // tm
    tok = lambda b, j: (b, j, 0)
    const = lambda b, j: (0, 0)
    ktb = lambda b, j: (b, j, 0, 0)
    st_all = lambda b, j: (0, b, 0, j)
    st_keep = lambda b, j: (0, b, 0, jnp.maximum(j - keep_first, 0))
    tok_b16 = jax.ShapeDtypeStruct((bsz, seq, gw), BF16)
    tok_f32 = jax.ShapeDtypeStruct((bsz, seq, gw), F32)
    kt_b16 = jax.ShapeDtypeStruct((bsz, seq // KEY_BLOCK, gw, KEY_BLOCK), BF16)
    st_f32 = jax.ShapeDtypeStruct((n_prev + 1, bsz, gw, seq), F32)
    keep_f32 = jax.ShapeDtypeStruct((n_prev + 1, bsz, gw, keep), F32)
    out_shape = (tok_b16, tok_b16, tok_f32, tok_b16, tok_b16, tok_f32, kt_b16, kt_b16,
                 st_f32, st_f32, keep_f32, keep_f32)
    tok_spec = pl.BlockSpec((None, tm, gw), tok)
    kt_spec = pl.BlockSpec((None, tm // KEY_BLOCK, gw, KEY_BLOCK), ktb)
    out_specs = (tok_spec,) * 6 + (kt_spec,) * 2 + (
        pl.BlockSpec((n_prev + 1, None, gw, tm), st_all),
        pl.BlockSpec((n_prev + 1, None, gw, tm), st_all),
        pl.BlockSpec((n_prev + 1, None, gw, tm), st_keep),
        pl.BlockSpec((n_prev + 1, None, gw, tm), st_keep))
    in_specs = [pl.BlockSpec((None, tm, d), tok),
                pl.BlockSpec((1, d), const),
                pl.BlockSpec((None,) + w_all16.shape[1:], lambda b, j: (layer, 0, 0),
                             pipeline_mode=pl.Buffered(1))]
    if n_prev:
        in_specs += [pl.BlockSpec((n_prev, None, gw, tm), st_all),
                     pl.BlockSpec((n_prev, None, gw, tm), st_all),
                     pl.BlockSpec((n_prev, None, gw, tm), st_keep),
                     pl.BlockSpec((n_prev, None, gw, tm), st_keep)]
    return pl.pallas_call(
        functools.partial(_inproj_prompt_kernel, n_prev=n_prev, keep_first=keep_first),
        out_shape=out_shape,
        grid=(bsz, seq // tm),
        in_specs=in_specs,
        out_specs=out_specs,
        compiler_params=_params(2),
        name="inproj_prompt",
    )(x3d, g_pre.reshape(1, d), w_all16, *prev)


def _finish_kernel(oa_ref, ob_ref, h_ref, p_ref, woa_ref, wob_ref, gpost_ref, wg_ref, wp_ref,
                   out_ref):
    y = _nn_dot(oa_ref[...], woa_ref[...]) + _nn_dot(ob_ref[...], wob_ref[...])
    h1 = h_ref[...] + _rms_scale(y, gpost_ref[...])
    gate_logit = _nn_dot(h1.astype(BF16), wg_ref[...])
    gate = 1.0 / (1.0 + jnp.exp(-gate_logit))
    emb = _nn_dot(p_ref[...].astype(BF16), wp_ref[...])
    out_ref[...] = h1 + gate * emb


def _finish(oa, ob, h2d, p_all, layer, wo16, g_post, wg16, wp16, tm):
    t, d = h2d.shape
    da = oa.shape[1]
    dp = p_all.shape[2]
    row = lambda i: (i, 0)
    const = lambda i: (layer, 0, 0)
    return pl.pallas_call(
        _finish_kernel,
        out_shape=jax.ShapeDtypeStruct((t, d), F32),
        grid=(t // tm,),
        in_specs=[pl.BlockSpec((tm, da), row),
                  pl.BlockSpec((tm, da), row),
                  pl.BlockSpec((tm, d), row),
                  pl.BlockSpec((None, tm, dp), lambda i: (layer, i, 0)),
                  pl.BlockSpec((None, da, d), const),
                  pl.BlockSpec((None, da, d), lambda i: (layer, 1, 0)),
                  pl.BlockSpec((None, 1, d), const),
                  pl.BlockSpec((None, d, d), const),
                  pl.BlockSpec((None, dp, d), const)],
        out_specs=pl.BlockSpec((tm, d), row),
        compiler_params=_params(1),
        name="finish",
    )(oa, ob, h2d, p_all, wo16, wo16, g_post.reshape(-1, 1, d), wg16, wp16)


def _build_band_bias(tbl_ref, maskadd_ref, bias_ref):
    m_rows = maskadd_ref.shape[0]
    n_heads = bias_ref.shape[0] // m_rows
    tbl = tbl_ref[...]
    p1 = tbl.astype(BF16)
    r1 = tbl - p1.astype(F32)
    p2 = r1.astype(BF16)
    p3 = (r1 - p2.astype(F32)).astype(BF16)
    r_idx = lax.broadcasted_iota(jnp.int32, (BIAS_TABLE_PAD, BIAS_EXT), 0)
    m_idx = lax.broadcasted_iota(jnp.int32, (BIAS_TABLE_PAD, BIAS_EXT), 1)
    rel = jnp.clip(m_idx - (m_rows - 1) - BAND_PAST, -REL_CLIP, REL_CLIP) + REL_CLIP
    onehot = (r_idx == rel).astype(BF16)
    t_ext = _nn_dot(p1, onehot) + _nn_dot(p2, onehot) + _nn_dot(p3, onehot)
    ql = lax.broadcasted_iota(jnp.int32, (m_rows, BIAS_EXT), 0)
    n_bits = int(m_rows - 1).bit_length()
    for h in range(n_heads):
        x = jnp.broadcast_to(t_ext[h:h + 1, :], (m_rows, BIAS_EXT))
        x = pltpu.roll(x, BIAS_EXT - (m_rows - 1), 1)
        for b in range(n_bits):
            x = jnp.where(((ql >> b) & 1) == 1, pltpu.roll(x, 1 << b, 1), x)
        bias_ref[h * m_rows:(h + 1) * m_rows, :] = x[:, :BAND_WINDOW] + maskadd_ref[...]


def _band_attend(q_ref, g_ref, bias_ref, out_ref, m_rows, windows):
    n_tiles = q_ref.shape[1] // LANES
    in_head = _head_masks(m_rows)
    kl = lax.broadcasted_iota(jnp.int32, (1, BAND_WINDOW), 1)
    lane_sl = [slice(p * LANES, (p + 1) * LANES) for p in range(n_tiles)]
    row_sl = [slice(j * m_rows, (j + 1) * m_rows) for j in range(len(windows))]
    rows2 = HEADS_PER_TILE * m_rows
    items = [(j, p) for j in range(len(windows)) for p in range(n_tiles)]

    def item_scores(n):
        j, p = items[n]
        return windows[j][0](lane_sl[p], _stack_heads(q_ref[row_sl[j], lane_sl[p]], in_head))

    groups = [list(range(a, min(a + BAND_GROUP, len(items))))
              for a in range(0, len(items), BAND_GROUP)]
    s_next = [item_scores(n) for n in groups[0]]
    for gi, group in enumerate(groups):
        ss = [s_next[i] + bias_ref[items[n][1] * rows2:(items[n][1] + 1) * rows2, :]
              for i, n in enumerate(group)]
        if gi + 1 < len(groups):
            s_next = [item_scores(n) for n in groups[gi + 1]]
        es = []
        for i, n in enumerate(group):
            s, key_lo = ss[i], windows[items[n][0]][2]
            if key_lo is not None:
                s = jnp.where(kl >= key_lo, s, NEG_INF)
            es.append(jnp.exp(s - jnp.max(s, axis=-1, keepdims=True)).astype(BF16))
        for i, n in enumerate(group):
            j, p = items[n]
            o_ext = windows[j][1](lane_sl[p], es[i])
            o2 = o_ext[:, :LANES] * (1.0 / o_ext[:, LANES:])
            o_pair = _unstack_heads(o2, in_head)
            out_ref[row_sl[j], lane_sl[p]] = (
                o_pair * _silu(g_ref[row_sl[j], lane_sl[p]])).astype(BF16)


def _band_prompt_kernel(q_ref, g_ref, kt_ref, v_ref, tbl_ref, maskadd_ref, out_ref, bias_ref):
    b = pl.program_id(0)
    step = pl.program_id(1)
    n_prev = BAND_PAST // KEY_BLOCK
    n_blk = BAND_WINDOW // KEY_BLOCK

    @pl.when((b == 0) & (step == 0))
    def _():
        _build_band_bias(tbl_ref, maskadd_ref, bias_ref)

    def window(c, at_start):
        first = c - n_prev
        blocks = [max(first + t, 0) if at_start else first + t for t in range(n_blk)]

        def scores(sl, q2):
            kt = jnp.concatenate([kt_ref[blk, sl, :] for blk in blocks], axis=1)
            return _nn_dot(q2, kt)

        def weighted_values(sl, e16):
            if at_start:
                v = jnp.concatenate(
                    [v_ref[blk * KEY_BLOCK:(blk + 1) * KEY_BLOCK, sl] for blk in blocks], axis=0)
            else:
                v = v_ref[pl.ds(pl.multiple_of(first * KEY_BLOCK, KEY_BLOCK), BAND_WINDOW), sl]
            return _nn_dot(e16, jnp.concatenate([v, jnp.ones_like(v)], axis=1))

        return scores, weighted_values, (n_prev - c) * KEY_BLOCK if at_start else None

    @pl.when(step == 0)
    def _():
        _band_attend(q_ref, g_ref, bias_ref, out_ref, BAND_Q_ROWS,
                     [window(j, True) for j in range(BAND_STEP_BLOCKS)])

    @pl.when(step > 0)
    def _():
        _band_attend(q_ref, g_ref, bias_ref, out_ref, BAND_Q_ROWS,
                     [window(step * BAND_STEP_BLOCKS + j, False)
                      for j in range(BAND_STEP_BLOCKS)])


def _band_static_mask(q_pos, k_pos):
    qc = q_pos // CHUNK
    kc = k_pos // CHUNK
    ok = (kc[None, :] <= qc[:, None]) & (kc[None, :] >= qc[:, None] - N_PREV_CHUNKS)
    return np.where(ok, 0.0, NEG_INF).astype(np.float32)


def _pad_table(rel_bias):
    return jnp.pad(rel_bias, ((0, 0), (0, BIAS_TABLE_PAD - rel_bias.shape[1])))


def _band_prompt(q16, g, kt16, v16, rel_bias):
    bsz, seq, da = q16.shape
    n_heads = da // HEAD_DIM
    step_rows = BAND_STEP_BLOCKS * BAND_Q_ROWS
    assert BAND_Q_ROWS == KEY_BLOCK and seq % step_rows == 0
    assert BAND_STEP_BLOCKS == BAND_PAST // KEY_BLOCK
    q_pos = BAND_PAST + np.arange(BAND_Q_ROWS)
    k_pos = np.arange(BAND_WINDOW)
    maskadd = jnp.asarray(_band_static_mask(q_pos, k_pos))
    blk = lambda b, c: (b, c, 0)
    whole = lambda b, c: (b, 0, 0)
    whole4 = lambda b, c: (b, 0, 0, 0)
    const = lambda b, c: (0, 0)
    return pl.pallas_call(
        _band_prompt_kernel,
        out_shape=jax.ShapeDtypeStruct((bsz, seq, da), BF16),
        grid=(bsz, seq // step_rows),
        in_specs=[pl.BlockSpec((None, step_rows, da), blk),
                  pl.BlockSpec((None, step_rows, da), blk),
                  pl.BlockSpec((None, seq // KEY_BLOCK, da, KEY_BLOCK), whole4),
                  pl.BlockSpec((None, seq, da), whole),
                  pl.BlockSpec((n_heads, BIAS_TABLE_PAD), const),
                  pl.BlockSpec((BAND_Q_ROWS, BAND_WINDOW), const)],
        out_specs=pl.BlockSpec((None, step_rows, da), blk),
        scratch_shapes=[pltpu.VMEM((n_heads * BAND_Q_ROWS, BAND_WINDOW), F32)],
        compiler_params=_params(2),
        name="band_prompt",
    )(q16, g, kt16, v16, _pad_table(rel_bias), maskadd)


def _band_sample_kernel(q_ref, g_ref, ckt_ref, cvt_ref, nk_ref, nv_ref, tbl_ref, maskadd_ref,
                        out_ref, bias_ref):
    b = pl.program_id(0)
    la = ckt_ref.shape[1]

    @pl.when(b == 0)
    def _():
        _build_band_bias(tbl_ref, maskadd_ref, bias_ref)

    def scores(sl, q2):
        s_cache = _nn_dot(q2, ckt_ref[sl, :].astype(BF16))
        s_new = _nt_dot(q2, _pad_rows(nk_ref[:, sl], KEY_BLOCK))
        return jnp.concatenate([s_cache, s_new], axis=1)

    def weighted_values(sl, e16):
        cvt = cvt_ref[sl, :].astype(BF16)
        nv = _pad_rows(nv_ref[:, sl], KEY_BLOCK)
        return (_nt_dot(e16[:, :la], jnp.concatenate([cvt, jnp.ones_like(cvt)], axis=0))
                + _nn_dot(e16[:, la:], jnp.concatenate([nv, jnp.ones_like(nv)], axis=1)))

    _band_attend(q_ref, g_ref, bias_ref, out_ref, q_ref.shape[0],
                 [(scores, weighted_values, None)])


def _band_sample(q16, g, cache_kt, cache_vt, layer, nk16, nv16, rel_bias, past):
    bsz, t, da = q16.shape
    la = cache_kt.shape[3]
    n_heads = da // HEAD_DIM
    assert la == BAND_PAST and la + KEY_BLOCK == BAND_WINDOW and t <= KEY_BLOCK and t % 16 == 0
    q_pos = past + np.arange(t)
    k_pos = past - la + np.arange(BAND_WINDOW)
    maskadd = _band_static_mask(q_pos, k_pos)
    maskadd[:, la + t:] = NEG_INF
    maskadd = jnp.asarray(maskadd)
    blk = lambda b: (b, 0, 0)
    const = lambda b: (0, 0)
    return pl.pallas_call(
        _band_sample_kernel,
        out_shape=jax.ShapeDtypeStruct((bsz, t, da), BF16),
        grid=(bsz,),
        in_specs=[pl.BlockSpec((None, t, da), blk),
                  pl.BlockSpec((None, t, da), blk),
                  pl.BlockSpec((None, None, da, la), lambda b: (layer, b, 0, 0)),
                  pl.BlockSpec((None, None, da, la), lambda b: (layer, b, 0, 0)),
                  pl.BlockSpec((None, t, da), blk),
                  pl.BlockSpec((None, t, da), blk),
                  pl.BlockSpec((n_heads, BIAS_TABLE_PAD), const),
                  pl.BlockSpec((t, BAND_WINDOW), const)],
        out_specs=pl.BlockSpec((None, t, da), blk),
        scratch_shapes=[pltpu.VMEM((n_heads * t, BAND_WINDOW), F32)],
        compiler_params=_params(1),
        name="band_sample",
    )(q16, g, cache_kt, cache_vt, nk16, nv16, _pad_table(rel_bias), maskadd)


def _suffix_matrix(width):
    j = np.arange(width)[:, None]
    s = np.arange(width)[None, :]
    tri = (j >= s).astype(np.float32)
    return jnp.asarray(np.concatenate([tri, tri], axis=0), dtype=BF16)


def _sb_tiles(q2s, tiles, carry_ref, acc_ref, first, k_is_t, v_is_t):
    n = len(q2s)
    zs = [[(_nn_dot if k_is_t else _nt_dot)(q2s[i], ks[i]) for i in range(n)]
          for ks, _, _, _ in tiles]
    split = []
    for t, (_, _, _, causal) in enumerate(tiles):
        split.append([])
        for z in zs[t]:
            sp = _softplus(z)
            if causal is not None:
                sp = jnp.where(causal, sp, 0.0)
            hi = sp.astype(BF16)
            lo = (sp - hi.astype(F32)).astype(BF16)
            split[t].append(jnp.concatenate([hi, lo], axis=1))
    sums = [[_nn_dot(s, tri) for s in split[t]] for t, (_, _, tri, _) in enumerate(tiles)]
    carries = [None if first else carry_ref[i] for i in range(n)]
    ws = []
    for t, (_, _, _, causal) in enumerate(tiles):
        ws.append([])
        for i in range(n):
            log_w = zs[t][i] - sums[t][i]
            total = sums[t][i][:, 0:1]
            if carries[i] is not None:
                log_w = log_w - carries[i]
                total = total + carries[i]
            w = jnp.exp(log_w)
            if causal is not None:
                w = jnp.where(causal, w, 0.0)
            ws[t].append(w.astype(BF16))
            carries[i] = total
    for i in range(n):
        acc = None if first else acc_ref[i]
        for t, (_, vs, _, _) in enumerate(tiles):
            part = (_nt_dot if v_is_t else _nn_dot)(ws[t][i], vs[i])
            acc = part if acc is None else acc + part
        acc_ref[i] = acc
        carry_ref[i] = carries[i]
    return carries


def _sb_setup(q_ref, td):
    tq, width = q_ref.shape
    lane_sl = [slice(p * LANES, (p + 1) * LANES) for p in range(width // LANES)]
    in_head = _head_masks(tq)
    rows2 = HEADS_PER_TILE * tq
    row = lax.broadcasted_iota(jnp.int32, (rows2, td), 0) % tq
    col = lax.broadcasted_iota(jnp.int32, (rows2, td), 1)
    causal = col < row
    q2s = [_stack_heads(q_ref[:, sl], in_head) for sl in lane_sl]
    return lane_sl, in_head, causal, q2s


def _sb_min_carry(carries):
    m = carries[0]
    for c in carries[1:]:
        m = jnp.minimum(m, c)
    return jnp.min(m)


def _sb_store(acc_ref, lane_sl, in_head, g_ref, out_ref):
    for n, sl in enumerate(lane_sl):
        o_pair = _unstack_heads(acc_ref[n], in_head)
        out_ref[:, sl] = (o_pair * _silu(g_ref[:, sl])).astype(BF16)


def _sb_scratch(rows, width):
    n = width // LANES
    return [pltpu.VMEM((n, HEADS_PER_TILE * rows, 1), F32),
            pltpu.VMEM((n, HEADS_PER_TILE * rows, LANES), F32)]


def _softplus(z):
    return jnp.maximum(z, 0.0) + jnp.log(1.0 + jnp.exp2(jnp.abs(z) * NEG_LOG2E))


def _sb_window(q2s, kts, vs, tri, causal, valid, slots, carry_ref, acc_ref, first):
    n = len(q2s)
    n_blk = kts[0].shape[1] // KEY_BLOCK
    newest_first = list(reversed(range(n_blk)))
    zs = [_nn_dot(q2s[c], kts[c]) for c in range(n)]

    def block_mask(c, b):
        mask = causal if b == n_blk - 1 else None
        if valid[c] is not None:
            mask = valid[c] if mask is None else mask & valid[c]
        return mask

    split = []
    for c in range(n):
        split.append({})
        for b in newest_first:
            sp = _softplus(zs[c][:, b * KEY_BLOCK:(b + 1) * KEY_BLOCK])
            mask = block_mask(c, b)
            if mask is not None:
                sp = jnp.where(mask, sp, 0.0)
            hi = sp.astype(BF16)
            lo = (sp - hi.astype(F32)).astype(BF16)
            split[c][b] = jnp.concatenate([hi, lo], axis=1)
    sums = [{b: _nn_dot(split[c][b], tri) for b in newest_first} for c in range(n)]
    carries, ws = [], []
    for c in range(n):
        carry = None if first else carry_ref[slots[c]]
        pieces = [None] * n_blk
        for b in newest_first:
            log_w = zs[c][:, b * KEY_BLOCK:(b + 1) * KEY_BLOCK] - sums[c][b]
            total = sums[c][b][:, 0:1]
            if carry is not None:
                log_w = log_w - carry
                total = total + carry
            w = jnp.exp(log_w)
            mask = block_mask(c, b)
            if mask is not None:
                w = jnp.where(mask, w, 0.0)
            pieces[b] = w.astype(BF16)
            carry = total
        carries.append(carry)
        ws.append(pieces[0] if n_blk == 1 else jnp.concatenate(pieces, axis=1))
    for c in range(n):
        part = _nn_dot(ws[c], vs[c])
        acc_ref[slots[c]] = part if first else acc_ref[slots[c]] + part
        carry_ref[slots[c]] = carries[c]
    return carries


def _sb_prompt_kernel(q_ref, g_ref, kt_ref, v_ref, tri_ref, out_ref, carry_ref, acc_ref):
    step = pl.program_id(1)
    r, n_qb, n_win = SB_Q_ROWS, SB_STEP_BLOCKS, SB_WINDOW_BLOCKS
    lane_sl = [slice(p * LANES, (p + 1) * LANES) for p in range(q_ref.shape[1] // LANES)]
    n_lane = len(lane_sl)
    in_head = _head_masks(r)
    row = lax.broadcasted_iota(jnp.int32, (HEADS_PER_TILE * r, KEY_BLOCK), 0) % r
    col = lax.broadcasted_iota(jnp.int32, (HEADS_PER_TILE * r, KEY_BLOCK), 1)
    causal = col < row
    tri = tri_ref[...]
    chains = [(a, p) for a in range(n_qb) for p in range(n_lane)]
    q2s = [_stack_heads(q_ref[a * r:(a + 1) * r, lane_sl[p]], in_head) for a, p in chains]

    def keys(first_blk, n_blk, sl):
        return jnp.concatenate([kt_ref[first_blk + u, sl, :] for u in range(n_blk)], axis=1)

    def values(first_blk, n_blk, sl):
        if isinstance(first_blk, int):
            return v_ref[first_blk * KEY_BLOCK:(first_blk + n_blk) * KEY_BLOCK, sl]
        start = pl.multiple_of(first_blk * KEY_BLOCK, KEY_BLOCK)
        return v_ref[pl.ds(start, n_blk * KEY_BLOCK), sl]

    def run_windows(which, first_blk, n_blk):
        for a in sorted({chains[c][0] for c in which}):
            mine = [c for c in which if chains[c][0] == a]
            _sb_window([q2s[c] for c in mine],
                       [keys(first_blk(a), n_blk(a), lane_sl[chains[c][1]]) for c in mine],
                       [values(first_blk(a), n_blk(a), lane_sl[chains[c][1]]) for c in mine],
                       tri, causal, [None] * len(mine), mine, carry_ref, acc_ref, True)

    every = list(range(len(chains)))

    @pl.when(step == 0)
    def _():
        run_windows(every, lambda a: max(a + 1 - n_win, 0), lambda a: min(a + 1, n_win))

    @pl.when(step > 0)
    def _():
        _sb_window(q2s,
                   [keys(step * n_qb + a + 1 - n_win, n_win, lane_sl[p]) for a, p in chains],
                   [values(step * n_qb + a + 1 - n_win, n_win, lane_sl[p]) for a, p in chains],
                   tri, causal, [None] * len(chains), every, carry_ref, acc_ref, True)

    n_more = step * n_qb + n_qb - n_win

    def cond(loop_state):
        it, smallest = loop_state
        return (it < n_more) & (smallest < SB_UNDERFLOW)

    def body(loop_state):
        it, _ = loop_state
        blks = [step * n_qb + a - n_win - it for a in range(n_qb)]
        valid = [None if a == n_qb - 1 else blks[a] >= 0 for a in range(n_qb)]
        blks = [jnp.maximum(blk, 0) for blk in blks]
        carries = _sb_window(q2s,
                             [keys(blks[a], 1, lane_sl[p]) for a, p in chains],
                             [values(blks[a], 1, lane_sl[p]) for a, p in chains],
                             tri, None, [valid[a] for a, _ in chains], every,
                             carry_ref, acc_ref, False)
        return it + 1, _sb_min_carry(carries)

    lax.while_loop(cond, body,
                   (jnp.int32(0), _sb_min_carry([carry_ref[c] for c in every])))
    for c, (a, p) in enumerate(chains):
        rows = slice(a * r, (a + 1) * r)
        o_pair = _unstack_heads(acc_ref[c], in_head)
        out_ref[rows, lane_sl[p]] = (o_pair * _silu(g_ref[rows, lane_sl[p]])).astype(BF16)


def _sb_prompt(q16, g, kt16, v16):
    bsz, seq, db = q16.shape
    step_rows = SB_STEP_BLOCKS * SB_Q_ROWS
    n_chains = SB_STEP_BLOCKS * (db // LANES)
    assert SB_Q_ROWS == KEY_BLOCK and seq % step_rows == 0 and db % LANES == 0
    qblk = lambda b, i: (b, i, 0)
    whole = lambda b, i: (b, 0, 0)
    whole4 = lambda b, i: (b, 0, 0, 0)
    const = lambda b, i: (0, 0)
    return pl.pallas_call(
        _sb_prompt_kernel,
        out_shape=jax.ShapeDtypeStruct((bsz, seq, db), BF16),
        grid=(bsz, seq // step_rows),
        in_specs=[pl.BlockSpec((None, step_rows, db), qblk),
                  pl.BlockSpec((None, step_rows, db), qblk),
                  pl.BlockSpec((None, seq // KEY_BLOCK, db, KEY_BLOCK), whole4),
                  pl.BlockSpec((None, seq, db), whole),
                  pl.BlockSpec((2 * KEY_BLOCK, KEY_BLOCK), const)],
        out_specs=pl.BlockSpec((None, step_rows, db), qblk),
        scratch_shapes=[pltpu.VMEM((n_chains, HEADS_PER_TILE * SB_Q_ROWS, 1), F32),
                        pltpu.VMEM((n_chains, HEADS_PER_TILE * SB_Q_ROWS, LANES), F32)],
        compiler_params=_params(2),
        name="sb_prompt",
    )(q16, g, kt16, v16, _suffix_matrix(KEY_BLOCK))


def _sb_sample_kernel(q_ref, g_ref, nk_ref, nv_ref, ckt_ref, cvt_ref, trid_ref, trif_ref,
                      out_ref, carry_ref, acc_ref):
    td = trid_ref.shape[1]
    tk = trif_ref.shape[1]
    n_full = ckt_ref.shape[1] // tk
    trif = trif_ref[...]
    lane_sl, in_head, causal, q2s = _sb_setup(q_ref, td)

    carries = _sb_tiles(q2s, [([_pad_rows(nk_ref[:, sl], td) for sl in lane_sl],
                               [_pad_rows(nv_ref[:, sl], td) for sl in lane_sl],
                               trid_ref[...], causal)], carry_ref, acc_ref, True, False, False)
    smallest = _sb_min_carry(carries)

    for tile in reversed(range(n_full)):
        cols = slice(tile * tk, (tile + 1) * tk)

        def step(cols=cols):
            kts = [ckt_ref[sl, cols].astype(BF16) for sl in lane_sl]
            vts = [cvt_ref[sl, cols].astype(BF16) for sl in lane_sl]
            return _sb_min_carry(_sb_tiles(q2s, [(kts, vts, trif, None)], carry_ref, acc_ref,
                                           False, True, True))

        smallest = lax.cond(smallest < SB_UNDERFLOW, step, lambda s=smallest: s)
    _sb_store(acc_ref, lane_sl, in_head, g_ref, out_ref)


def _sb_sample(q16, g, nk16, nv16, cache_kt, cache_vt, layer):
    bsz, t, db = q16.shape
    past = cache_kt.shape[3]
    width, tk, td = SB_WIDTH, SB_TILE, KEY_BLOCK
    assert t <= td and t % 16 == 0 and past % tk == 0 and db % width == 0
    blk = lambda b, p: (b, 0, p)
    cblk = lambda b, p: (layer, b, p, 0)
    const = lambda b, p: (0, 0)
    return pl.pallas_call(
        _sb_sample_kernel,
        out_shape=jax.ShapeDtypeStruct((bsz, t, db), BF16),
        grid=(bsz, db // width),
        in_specs=[pl.BlockSpec((None, t, width), blk),
                  pl.BlockSpec((None, t, width), blk),
                  pl.BlockSpec((None, t, width), blk),
                  pl.BlockSpec((None, t, width), blk),
                  pl.BlockSpec((None, None, width, past), cblk),
                  pl.BlockSpec((None, None, width, past), cblk),
                  pl.BlockSpec((2 * td, td), const),
                  pl.BlockSpec((2 * tk, tk), const)],
        out_specs=pl.BlockSpec((None, t, width), blk),
        scratch_shapes=_sb_scratch(t, width),
        compiler_params=_params(2),
        name="sb_sample",
    )(q16, g, nk16, nv16, cache_kt, cache_vt, _suffix_matrix(td), _suffix_matrix(tk))


def _row_tile(t):
    return ROW_TILE if t % ROW_TILE == 0 else t


def _cache_transposed(cache):
    l, b, s, h, dh = cache.shape
    return jnp.transpose(cache, (0, 1, 3, 4, 2)).reshape(l, b, h * dh, s)


def _cache_layout(stacked, n_heads):
    l, b, hd, s = stacked.shape
    return jnp.transpose(stacked.reshape(l, b, n_heads, hd // n_heads, s), (0, 1, 4, 2, 3))


def kernel(x_prompt, x_sample, p_prompt, p_sample, cache_a_k, cache_a_v, cache_b_k, cache_b_v,
           g_pre, w_in, rel_bias, w_out, g_post, w_ple, w_ple_gate):
    depth = w_in.shape[0]
    bsz, seq, d = x_prompt.shape
    dbsz, dseq, _ = x_sample.shape
    past = cache_b_k.shape[2]
    keep = min(BAND_PAST, seq)
    da = w_in.shape[2] // 8
    n_heads = da // HEAD_DIM
    tp, ts = bsz * seq, dbsz * dseq

    hp = x_prompt
    hs = x_sample.reshape(ts, d)
    p_prompt = p_prompt.reshape(depth, tp, -1)
    p_sample = p_sample.reshape(depth, ts, -1)
    cache_a_kt, cache_a_vt, cache_b_kt, cache_b_vt = (
        _cache_transposed(c) for c in (cache_a_k, cache_a_v, cache_b_k, cache_b_v))
    stacked = ()
    sample_stacked = ()
    w_in16, w_out16, w_gate16, w_ple16 = (
        w.astype(BF16) for w in (w_in, w_out, w_ple_gate, w_ple))
    for i in range(depth):
        (qa, va16, ga, qb, vb16, gb, kat16, kbt16, *stacked) = _inproj_prompt(
            hp, g_pre[i], w_in16, i, tuple(stacked), keep)
        oa = _band_prompt(qa, ga, kat16, va16, rel_bias[i])
        ob = _sb_prompt(qb, gb, kbt16, vb16)
        hp = _finish(oa.reshape(tp, da), ob.reshape(tp, da), hp.reshape(tp, d),
                     p_prompt, i, w_out16, g_post, w_gate16, w_ple16,
                     _row_tile(tp)).reshape(bsz, seq, d)

        (qa, ga, qb, gb, ka16, va16, kb16, vb16, *sample_stacked) = _inproj(
            hs, g_pre, w_in16, i, tuple(sample_stacked), _row_tile(ts))
        s3 = lambda a: a.reshape(dbsz, dseq, da)
        oa = _band_sample(s3(qa), s3(ga), cache_a_kt, cache_a_vt, i, s3(ka16), s3(va16),
                          rel_bias[i], past)
        ob = _sb_sample(s3(qb), s3(gb), s3(kb16), s3(vb16), cache_b_kt, cache_b_vt, i)
        hs = _finish(oa.reshape(ts, da), ob.reshape(ts, da), hs, p_sample, i,
                     w_out16, g_post, w_gate16, w_ple16, _row_tile(ts))

    kb_st, vb_st, ka_st, va_st = stacked
    s5 = lambda a: a.reshape(depth, dbsz, dseq, n_heads, HEAD_DIM)
    return (hp, hs.reshape(dbsz, dseq, d),
            _cache_layout(ka_st, n_heads), _cache_layout(va_st, n_heads),
            _cache_layout(kb_st, n_heads), _cache_layout(vb_st, n_heads),
            *(s5(a) for a in sample_stacked))
```

```python
import functools

import numpy as np
import jax
import jax.numpy as jnp
from jax import lax
from jax.experimental import pallas as pl
from jax.experimental.pallas import tpu as pltpu

HEAD_DIM = 64
CHUNK = 64
N_PREV_CHUNKS = 8
BAND_PAST = N_PREV_CHUNKS * CHUNK
REL_CLIP = 128
RMS_EPS = 1e-6
NEG_INF = -1e30
ATTN_SCALE = HEAD_DIM ** -0.5
NEG_LOG2E = -1.4426950408889634

LANES = 128
HEADS_PER_TILE = LANES // HEAD_DIM
VMEM_LIMIT_BYTES = 56 * 1024 * 1024

KEY_BLOCK = LANES
BAND_Q_ROWS = 2 * CHUNK
BAND_STEP_BLOCKS = 4
BAND_GROUP = 4
BAND_WINDOW = BAND_PAST + BAND_Q_ROWS
BIAS_TABLE_PAD = 384
BIAS_EXT = 768

SB_TILE = 256
SB_Q_ROWS = 128
SB_STEP_BLOCKS = 2
SB_WINDOW_BLOCKS = 3
SB_WIDTH = 512
SB_UNDERFLOW = 105.0

INPROJ_ROWS = 512
ROW_TILE = 1024

BF16 = jnp.bfloat16
F32 = jnp.float32


def _params(n_axes):
    return pltpu.CompilerParams(
        dimension_semantics=("arbitrary",) * n_axes,
        vmem_limit_bytes=VMEM_LIMIT_BYTES)


def _nt_dot(a, b):
    return lax.dot_general(a, b, (((1,), (1,)), ((), ())), preferred_element_type=F32)


def _nn_dot(a, b):
    return jnp.dot(a, b, preferred_element_type=F32)


def _silu(g):
    return g * (1.0 / (1.0 + jnp.exp(-g)))


def _rms_scale(x, gain):
    ms = jnp.mean(x * x, axis=-1, keepdims=True)
    return x * lax.rsqrt(ms + RMS_EPS) * gain


def _head_masks(m_rows):
    lane = lax.broadcasted_iota(jnp.int32, (m_rows, LANES), 1)
    return [(lane >= hh * HEAD_DIM) & (lane < (hh + 1) * HEAD_DIM)
            for hh in range(HEADS_PER_TILE)]


def _stack_heads(q_pair, in_head):
    zero = jnp.zeros_like(q_pair)
    return jnp.concatenate([jnp.where(m, q_pair, zero) for m in in_head], axis=0)


def _unstack_heads(o2, in_head):
    m_rows = o2.shape[0] // HEADS_PER_TILE
    o = o2[0:m_rows]
    for hh in range(1, HEADS_PER_TILE):
        o = jnp.where(in_head[hh], o2[hh * m_rows:(hh + 1) * m_rows], o)
    return o


def _pad_rows(a, rows):
    return jnp.concatenate([a, jnp.zeros((rows - a.shape[0], a.shape[1]), a.dtype)], axis=0)


def _inproj_kernel(*refs, n_prev):
    x_ref, g_ref, w_ref = refs[:3]
    prev = refs[3:3 + (4 if n_prev else 0)]
    (qa_ref, ga_ref, qb_ref, gb_ref, ka16_ref, va16_ref, kb16_ref, vb16_ref) = refs[
        3 + len(prev):11 + len(prev)]
    stacked = refs[11 + len(prev):]
    hn = _rms_scale(x_ref[...], g_ref[...]).astype(BF16)
    gw = qa_ref.shape[-1]

    def proj(n):
        return _nn_dot(hn, w_ref[:, n * gw:(n + 1) * gw])

    qa_ref[...] = (proj(0) * ATTN_SCALE).astype(BF16)
    ga_ref[...] = proj(3)
    qb_ref[...] = (proj(4) * ATTN_SCALE).astype(BF16)
    gb_ref[...] = proj(7)
    for n, st_ref, b16_ref in zip((1, 2, 5, 6), stacked, (ka16_ref, va16_ref, kb16_ref, vb16_ref)):
        u = proj(n)
        st_ref[n_prev] = u
        b16_ref[...] = u.astype(BF16)
    for k in range(len(prev)):
        for dpt in range(n_prev):
            stacked[k][dpt] = prev[k][dpt]


def _inproj(x2d, g_pre, w_all16, layer, prev, tm):
    t, d = x2d.shape
    d_in = w_all16.shape[2]
    gw = d_in // 8
    n_prev = prev[0].shape[0] if prev else 0
    row = lambda i: (i, 0)
    row3 = lambda i: (0, i, 0)
    const = lambda i: (layer, 0, 0)
    f32_out = jax.ShapeDtypeStruct((t, gw), F32)
    b16_out = jax.ShapeDtypeStruct((t, gw), BF16)
    st_out = jax.ShapeDtypeStruct((n_prev + 1, t, gw), F32)
    out_shape = (b16_out, f32_out, b16_out, f32_out, b16_out, b16_out, b16_out, b16_out,
                 st_out, st_out, st_out, st_out)
    return pl.pallas_call(
        functools.partial(_inproj_kernel, n_prev=n_prev),
        out_shape=out_shape,
        grid=(t // tm,),
        in_specs=[pl.BlockSpec((tm, d), row),
                  pl.BlockSpec((None, 1, d), const),
                  pl.BlockSpec((None, d, d_in), const)]
                 + [pl.BlockSpec((n_prev, tm, gw), row3)] * len(prev),
        out_specs=(pl.BlockSpec((tm, gw), row),) * 8
                  + (pl.BlockSpec((n_prev + 1, tm, gw), row3),) * 4,
        compiler_params=_params(1),
        name="inproj_sample",
    )(x2d, g_pre.reshape(-1, 1, d), w_all16, *prev)


def _inproj_prompt_kernel(*refs, n_prev, keep_first):
    x_ref, g_ref, w_ref = refs[:3]
    prev = refs[3:3 + (4 if n_prev else 0)]
    (qa_ref, va16_ref, ga_ref, qb_ref, vb16_ref, gb_ref, kat16_ref, kbt16_ref,
     kb_st_ref, vb_st_ref, ka_st_ref, va_st_ref) = refs[3 + len(prev):]
    j = pl.program_id(1)
    gw = qa_ref.shape[-1]
    hn = _rms_scale(x_ref[...], g_ref[...]).astype(BF16)

    def proj(n):
        return _nn_dot(hn, w_ref[:, n * gw:(n + 1) * gw])

    kat, kbt = proj(1).T, proj(5).T
    vb = proj(6)
    vbt = vb.T
    vb16_ref[...] = vb.astype(BF16)
    for u in range(kat16_ref.shape[0]):
        blk = slice(u * KEY_BLOCK, (u + 1) * KEY_BLOCK)
        kat16_ref[u] = kat[:, blk].astype(BF16)
        kbt16_ref[u] = kbt[:, blk].astype(BF16)
    for dpt in range(n_prev):
        kb_st_ref[dpt] = prev[0][dpt]
        vb_st_ref[dpt] = prev[1][dpt]
    kb_st_ref[n_prev] = kbt
    vb_st_ref[n_prev] = vbt

    qa_ref[...] = (proj(0) * ATTN_SCALE).astype(BF16)
    va = proj(2)
    va16_ref[...] = va.astype(BF16)
    ga_ref[...] = proj(3)
    qb_ref[...] = (proj(4) * ATTN_SCALE).astype(BF16)
    gb_ref[...] = proj(7)

    @pl.when(j >= keep_first)
    def _():
        for dpt in range(n_prev):
            ka_st_ref[dpt] = prev[2][dpt]
            va_st_ref[dpt] = prev[3][dpt]
        ka_st_ref[n_prev] = kat
        va_st_ref[n_prev] = va.T


def _inproj_prompt(x3d, g_pre, w_all16, layer, prev, keep):
    bsz, seq, d = x3d.shape
    tm = INPROJ_ROWS
    gw = w_all16.shape[2] // 8
    n_prev = prev[0].shape[0] if prev else 0
    assert seq % tm == 0 and keep % tm == 0 and tm % KEY_BLOCK == 0
    keep_first = (seq - keep) // tm
    tok = lambda b, j: (b, j, 0)
    const = lambda b, j: (0, 0)
    ktb = lambda b, j: (b, j, 0, 0)
    st_all = lambda b, j: (0, b, 0, j)
    st_keep = lambda b, j: (0, b, 0, jnp.maximum(j - keep_first, 0))
    tok_b16 = jax.ShapeDtypeStruct((bsz, seq, gw), BF16)
    tok_f32 = jax.ShapeDtypeStruct((bsz, seq, gw), F32)
    kt_b16 = jax.ShapeDtypeStruct((bsz, seq // KEY_BLOCK, gw, KEY_BLOCK), BF16)
    st_f32 = jax.ShapeDtypeStruct((n_prev + 1, bsz, gw, seq), F32)
    keep_f32 = jax.ShapeDtypeStruct((n_prev + 1, bsz, gw, keep), F32)
    out_shape = (tok_b16, tok_b16, tok_f32, tok_b16, tok_b16, tok_f32, kt_b16, kt_b16,
                 st_f32, st_f32, keep_f32, keep_f32)
    tok_spec = pl.BlockSpec((None, tm, gw), tok)
    kt_spec = pl.BlockSpec((None, tm // KEY_BLOCK, gw, KEY_BLOCK), ktb)
    out_specs = (tok_spec,) * 6 + (kt_spec,) * 2 + (
        pl.BlockSpec((n_prev + 1, None, gw, tm), st_all),
        pl.BlockSpec((n_prev + 1, None, gw, tm), st_all),
        pl.BlockSpec((n_prev + 1, None, gw, tm), st_keep),
        pl.BlockSpec((n_prev + 1, None, gw, tm), st_keep))
    in_specs = [pl.BlockSpec((None, tm, d), tok),
                pl.BlockSpec((1, d), const),
                pl.BlockSpec((None,) + w_all16.shape[1:], lambda b, j: (layer, 0, 0),
                             pipeline_mode=pl.Buffered(1))]
    if n_prev:
        in_specs += [pl.BlockSpec((n_prev, None, gw, tm), st_all),
                     pl.BlockSpec((n_prev, None, gw, tm), st_all),
                     pl.BlockSpec((n_prev, None, gw, tm), st_keep),
                     pl.BlockSpec((n_prev, None, gw, tm), st_keep)]
    return pl.pallas_call(
        functools.partial(_inproj_prompt_kernel, n_prev=n_prev, keep_first=keep_first),
        out_shape=out_shape,
        grid=(bsz, seq // tm),
        in_specs=in_specs,
        out_specs=out_specs,
        compiler_params=_params(2),
        name="inproj_prompt",
    )(x3d, g_pre.reshape(1, d), w_all16, *prev)


def _finish_kernel(oa_ref, ob_ref, h_ref, p_ref, woa_ref, wob_ref, gpost_ref, wg_ref, wp_ref,
                   out_ref):
    y = _nn_dot(oa_ref[...], woa_ref[...]) + _nn_dot(ob_ref[...], wob_ref[...])
    h1 = h_ref[...] + _rms_scale(y, gpost_ref[...])
    gate_logit = _nn_dot(h1.astype(BF16), wg_ref[...])
    gate = 1.0 / (1.0 + jnp.exp(-gate_logit))
    emb = _nn_dot(p_ref[...].astype(BF16), wp_ref[...])
    out_ref[...] = h1 + gate * emb


def _finish(oa, ob, h2d, p_all, layer, wo16, g_post, wg16, wp16, tm):
    t, d = h2d.shape
    da = oa.shape[1]
    dp = p_all.shape[2]
    row = lambda i: (i, 0)
    const = lambda i: (layer, 0, 0)
    return pl.pallas_call(
        _finish_kernel,
        out_shape=jax.ShapeDtypeStruct((t, d), F32),
        grid=(t // tm,),
        in_specs=[pl.BlockSpec((tm, da), row),
                  pl.BlockSpec((tm, da), row),
                  pl.BlockSpec((tm, d), row),
                  pl.BlockSpec((None, tm, dp), lambda i: (layer, i, 0)),
                  pl.BlockSpec((None, da, d), const),
                  pl.BlockSpec((None, da, d), lambda i: (layer, 1, 0)),
                  pl.BlockSpec((None, 1, d), const),
                  pl.BlockSpec((None, d, d), const),
                  pl.BlockSpec((None, dp, d), const)],
        out_specs=pl.BlockSpec((tm, d), row),
        compiler_params=_params(1),
        name="finish",
    )(oa, ob, h2d, p_all, wo16, wo16, g_post.reshape(-1, 1, d), wg16, wp16)


def _build_band_bias(tbl_ref, maskadd_ref, bias_ref):
    m_rows = maskadd_ref.shape[0]
    n_heads = bias_ref.shape[0] // m_rows
    tbl = tbl_ref[...]
    p1 = tbl.astype(BF16)
    r1 = tbl - p1.astype(F32)
    p2 = r1.astype(BF16)
    p3 = (r1 - p2.astype(F32)).astype(BF16)
    r_idx = lax.broadcasted_iota(jnp.int32, (BIAS_TABLE_PAD, BIAS_EXT), 0)
    m_idx = lax.broadcasted_iota(jnp.int32, (BIAS_TABLE_PAD, BIAS_EXT), 1)
    rel = jnp.clip(m_idx - (m_rows - 1) - BAND_PAST, -REL_CLIP, REL_CLIP) + REL_CLIP
    onehot = (r_idx == rel).astype(BF16)
    t_ext = _nn_dot(p1, onehot) + _nn_dot(p2, onehot) + _nn_dot(p3, onehot)
    ql = lax.broadcasted_iota(jnp.int32, (m_rows, BIAS_EXT), 0)
    n_bits = int(m_rows - 1).bit_length()
    for h in range(n_heads):
        x = jnp.broadcast_to(t_ext[h:h + 1, :], (m_rows, BIAS_EXT))
        x = pltpu.roll(x, BIAS_EXT - (m_rows - 1), 1)
        for b in range(n_bits):
            x = jnp.where(((ql >> b) & 1) == 1, pltpu.roll(x, 1 << b, 1), x)
        bias_ref[h * m_rows:(h + 1) * m_rows, :] = x[:, :BAND_WINDOW] + maskadd_ref[...]


def _band_attend(q_ref, g_ref, bias_ref, out_ref, m_rows, windows):
    n_tiles = q_ref.shape[1] // LANES
    in_head = _head_masks(m_rows)
    kl = lax.broadcasted_iota(jnp.int32, (1, BAND_WINDOW), 1)
    lane_sl = [slice(p * LANES, (p + 1) * LANES) for p in range(n_tiles)]
    row_sl = [slice(j * m_rows, (j + 1) * m_rows) for j in range(len(windows))]
    rows2 = HEADS_PER_TILE * m_rows
    items = [(j, p) for j in range(len(windows)) for p in range(n_tiles)]

    def item_scores(n):
        j, p = items[n]
        return windows[j][0](lane_sl[p], _stack_heads(q_ref[row_sl[j], lane_sl[p]], in_head))

    groups = [list(range(a, min(a + BAND_GROUP, len(items))))
              for a in range(0, len(items), BAND_GROUP)]
    s_next = [item_scores(n) for n in groups[0]]
    for gi, group in enumerate(groups):
        ss = [s_next[i] + bias_ref[items[n][1] * rows2:(items[n][1] + 1) * rows2, :]
              for i, n in enumerate(group)]
        if gi + 1 < len(groups):
            s_next = [item_scores(n) for n in groups[gi + 1]]
        es = []
        for i, n in enumerate(group):
            s, key_lo = ss[i], windows[items[n][0]][2]
            if key_lo is not None:
                s = jnp.where(kl >= key_lo, s, NEG_INF)
            es.append(jnp.exp(s - jnp.max(s, axis=-1, keepdims=True)).astype(BF16))
        for i, n in enumerate(group):
            j, p = items[n]
            o_ext = windows[j][1](lane_sl[p], es[i])
            o2 = o_ext[:, :LANES] * (1.0 / o_ext[:, LANES:])
            o_pair = _unstack_heads(o2, in_head)
            out_ref[row_sl[j], lane_sl[p]] = (
                o_pair * _silu(g_ref[row_sl[j], lane_sl[p]])).astype(BF16)


def _band_prompt_kernel(q_ref, g_ref, kt_ref, v_ref, tbl_ref, maskadd_ref, out_ref, bias_ref):
    b = pl.program_id(0)
    step = pl.program_id(1)
    n_prev = BAND_PAST // KEY_BLOCK
    n_blk = BAND_WINDOW // KEY_BLOCK

    @pl.when((b == 0) & (step == 0))
    def _():
        _build_band_bias(tbl_ref, maskadd_ref, bias_ref)

    def window(c, at_start):
        first = c - n_prev
        blocks = [max(first + t, 0) if at_start else first + t for t in range(n_blk)]

        def scores(sl, q2):
            kt = jnp.concatenate([kt_ref[blk, sl, :] for blk in blocks], axis=1)
            return _nn_dot(q2, kt)

        def weighted_values(sl, e16):
            if at_start:
                v = jnp.concatenate(
                    [v_ref[blk * KEY_BLOCK:(blk + 1) * KEY_BLOCK, sl] for blk in blocks], axis=0)
            else:
                v = v_ref[pl.ds(pl.multiple_of(first * KEY_BLOCK, KEY_BLOCK), BAND_WINDOW), sl]
            return _nn_dot(e16, jnp.concatenate([v, jnp.ones_like(v)], axis=1))

        return scores, weighted_values, (n_prev - c) * KEY_BLOCK if at_start else None

    @pl.when(step == 0)
    def _():
        _band_attend(q_ref, g_ref, bias_ref, out_ref, BAND_Q_ROWS,
                     [window(j, True) for j in range(BAND_STEP_BLOCKS)])

    @pl.when(step > 0)
    def _():
        _band_attend(q_ref, g_ref, bias_ref, out_ref, BAND_Q_ROWS,
                     [window(step * BAND_STEP_BLOCKS + j, False)
                      for j in range(BAND_STEP_BLOCKS)])


def _band_static_mask(q_pos, k_pos):
    qc = q_pos // CHUNK
    kc = k_pos // CHUNK
    ok = (kc[None, :] <= qc[:, None]) & (kc[None, :] >= qc[:, None] - N_PREV_CHUNKS)
    return np.where(ok, 0.0, NEG_INF).astype(np.float32)


def _pad_table(rel_bias):
    return jnp.pad(rel_bias, ((0, 0), (0, BIAS_TABLE_PAD - rel_bias.shape[1])))


def _band_prompt(q16, g, kt16, v16, rel_bias):
    bsz, seq, da = q16.shape
    n_heads = da // HEAD_DIM
    step_rows = BAND_STEP_BLOCKS * BAND_Q_ROWS
    assert BAND_Q_ROWS == KEY_BLOCK and seq % step_rows == 0
    assert BAND_STEP_BLOCKS == BAND_PAST // KEY_BLOCK
    q_pos = BAND_PAST + np.arange(BAND_Q_ROWS)
    k_pos = np.arange(BAND_WINDOW)
    maskadd = jnp.asarray(_band_static_mask(q_pos, k_pos))
    blk = lambda b, c: (b, c, 0)
    whole = lambda b, c: (b, 0, 0)
    whole4 = lambda b, c: (b, 0, 0, 0)
    const = lambda b, c: (0, 0)
    return pl.pallas_call(
        _band_prompt_kernel,
        out_shape=jax.ShapeDtypeStruct((bsz, seq, da), BF16),
        grid=(bsz, seq // step_rows),
        in_specs=[pl.BlockSpec((None, step_rows, da), blk),
                  pl.BlockSpec((None, step_rows, da), blk),
                  pl.BlockSpec((None, seq // KEY_BLOCK, da, KEY_BLOCK), whole4),
                  pl.BlockSpec((None, seq, da), whole),
                  pl.BlockSpec((n_heads, BIAS_TABLE_PAD), const),
                  pl.BlockSpec((BAND_Q_ROWS, BAND_WINDOW), const)],
        out_specs=pl.BlockSpec((None, step_rows, da), blk),
        scratch_shapes=[pltpu.VMEM((n_heads * BAND_Q_ROWS, BAND_WINDOW), F32)],
        compiler_params=_params(2),
        name="band_prompt",
    )(q16, g, kt16, v16, _pad_table(rel_bias), maskadd)


def _band_sample_kernel(q_ref, g_ref, ckt_ref, cvt_ref, nk_ref, nv_ref, tbl_ref, maskadd_ref,
                        out_ref, bias_ref):
    b = pl.program_id(0)
    la = ckt_ref.shape[1]

    @pl.when(b == 0)
    def _():
        _build_band_bias(tbl_ref, maskadd_ref, bias_ref)

    def scores(sl, q2):
        s_cache = _nn_dot(q2, ckt_ref[sl, :].astype(BF16))
        s_new = _nt_dot(q2, _pad_rows(nk_ref[:, sl], KEY_BLOCK))
        return jnp.concatenate([s_cache, s_new], axis=1)

    def weighted_values(sl, e16):
        cvt = cvt_ref[sl, :].astype(BF16)
        nv = _pad_rows(nv_ref[:, sl], KEY_BLOCK)
        return (_nt_dot(e16[:, :la], jnp.concatenate([cvt, jnp.ones_like(cvt)], axis=0))
                + _nn_dot(e16[:, la:], jnp.concatenate([nv, jnp.ones_like(nv)], axis=1)))

    _band_attend(q_ref, g_ref, bias_ref, out_ref, q_ref.shape[0],
                 [(scores, weighted_values, None)])


def _band_sample(q16, g, cache_kt, cache_vt, layer, nk16, nv16, rel_bias, past):
    bsz, t, da = q16.shape
    la = cache_kt.shape[3]
    n_heads = da // HEAD_DIM
    assert la == BAND_PAST and la + KEY_BLOCK == BAND_WINDOW and t <= KEY_BLOCK and t % 16 == 0
    q_pos = past + np.arange(t)
    k_pos = past - la + np.arange(BAND_WINDOW)
    maskadd = _band_static_mask(q_pos, k_pos)
    maskadd[:, la + t:] = NEG_INF
    maskadd = jnp.asarray(maskadd)
    blk = lambda b: (b, 0, 0)
    const = lambda b: (0, 0)
    return pl.pallas_call(
        _band_sample_kernel,
        out_shape=jax.ShapeDtypeStruct((bsz, t, da), BF16),
        grid=(bsz,),
        in_specs=[pl.BlockSpec((None, t, da), blk),
                  pl.BlockSpec((None, t, da), blk),
                  pl.BlockSpec((None, None, da, la), lambda b: (layer, b, 0, 0)),
                  pl.BlockSpec((None, None, da, la), lambda b: (layer, b, 0, 0)),
                  pl.BlockSpec((None, t, da), blk),
                  pl.BlockSpec((None, t, da), blk),
                  pl.BlockSpec((n_heads, BIAS_TABLE_PAD), const),
                  pl.BlockSpec((t, BAND_WINDOW), const)],
        out_specs=pl.BlockSpec((None, t, da), blk),
        scratch_shapes=[pltpu.VMEM((n_heads * t, BAND_WINDOW), F32)],
        compiler_params=_params(1),
        name="band_sample",
    )(q16, g, cache_kt, cache_vt, nk16, nv16, _pad_table(rel_bias), maskadd)


def _suffix_matrix(width):
    j = np.arange(width)[:, None]
    s = np.arange(width)[None, :]
    tri = (j >= s).astype(np.float32)
    return jnp.asarray(np.concatenate([tri, tri], axis=0), dtype=BF16)


def _sb_tiles(q2s, tiles, carry_ref, acc_ref, first, k_is_t, v_is_t):
    n = len(q2s)
    zs = [[(_nn_dot if k_is_t else _nt_dot)(q2s[i], ks[i]) for i in range(n)]
          for ks, _, _, _ in tiles]
    split = []
    for t, (_, _, _, causal) in enumerate(tiles):
        split.append([])
        for z in zs[t]:
            sp = _softplus(z)
            if causal is not None:
                sp = jnp.where(causal, sp, 0.0)
            hi = sp.astype(BF16)
            lo = (sp - hi.astype(F32)).astype(BF16)
            split[t].append(jnp.concatenate([hi, lo], axis=1))
    sums = [[_nn_dot(s, tri) for s in split[t]] for t, (_, _, tri, _) in enumerate(tiles)]
    carries = [None if first else carry_ref[i] for i in range(n)]
    ws = []
    for t, (_, _, _, causal) in enumerate(tiles):
        ws.append([])
        for i in range(n):
            log_w = zs[t][i] - sums[t][i]
            total = sums[t][i][:, 0:1]
            if carries[i] is not None:
                log_w = log_w - carries[i]
                total = total + carries[i]
            w = jnp.exp(log_w)
            if causal is not None:
                w = jnp.where(causal, w, 0.0)
            ws[t].append(w.astype(BF16))
            carries[i] = total
    for i in range(n):
        acc = None if first else acc_ref[i]
        for t, (_, vs, _, _) in enumerate(tiles):
            part = (_nt_dot if v_is_t else _nn_dot)(ws[t][i], vs[i])
            acc = part if acc is None else acc + part
        acc_ref[i] = acc
        carry_ref[i] = carries[i]
    return carries


def _sb_setup(q_ref, td):
    tq, width = q_ref.shape
    lane_sl = [slice(p * LANES, (p + 1) * LANES) for p in range(width // LANES)]
    in_head = _head_masks(tq)
    rows2 = HEADS_PER_TILE * tq
    row = lax.broadcasted_iota(jnp.int32, (rows2, td), 0) % tq
    col = lax.broadcasted_iota(jnp.int32, (rows2, td), 1)
    causal = col < row
    q2s = [_stack_heads(q_ref[:, sl], in_head) for sl in lane_sl]
    return lane_sl, in_head, causal, q2s


def _sb_min_carry(carries):
    m = carries[0]
    for c in carries[1:]:
        m = jnp.minimum(m, c)
    return jnp.min(m)


def _sb_store(acc_ref, lane_sl, in_head, g_ref, out_ref):
    for n, sl in enumerate(lane_sl):
        o_pair = _unstack_heads(acc_ref[n], in_head)
        out_ref[:, sl] = (o_pair * _silu(g_ref[:, sl])).astype(BF16)


def _sb_scratch(rows, width):
    n = width // LANES
    return [pltpu.VMEM((n, HEADS_PER_TILE * rows, 1), F32),
            pltpu.VMEM((n, HEADS_PER_TILE * rows, LANES), F32)]


def _softplus(z):
    return jnp.maximum(z, 0.0) + jnp.log(1.0 + jnp.exp2(jnp.abs(z) * NEG_LOG2E))


def _suffix_pair_matrix(width):
    j = np.arange(width)[:, None]
    s = np.arange(width)[None, :]
    tri = np.tile((j >= s).astype(np.float32), (2, 1))
    zero = np.zeros_like(tri)
    return jnp.asarray(np.block([[tri, zero], [zero, tri]]), dtype=BF16)


def _sb_window(q2s, kts, vs, tri, tri_pair, causal, valid, slots, carry_ref, acc_ref, first):
    n = len(q2s)
    n_blk = kts[0].shape[1] // KEY_BLOCK
    newest_first = list(reversed(range(n_blk)))
    zs = [_nn_dot(q2s[c], kts[c]) for c in range(n)]

    def block_mask(c, b):
        mask = causal if b == n_blk - 1 else None
        if valid[c] is not None:
            mask = valid[c] if mask is None else mask & valid[c]
        return mask

    split = []
    for c in range(n):
        split.append({})
        for b in newest_first:
            sp = _softplus(zs[c][:, b * KEY_BLOCK:(b + 1) * KEY_BLOCK])
            mask = block_mask(c, b)
            if mask is not None:
                sp = jnp.where(mask, sp, 0.0)
            hi = sp.astype(BF16)
            lo = (sp - hi.astype(F32)).astype(BF16)
            split[c][b] = jnp.concatenate([hi, lo], axis=1)
    sums = []
    for c in range(n):
        sums.append({})
        todo = list(newest_first)
        while todo:
            if len(todo) >= 2:
                newer, older = todo[0], todo[1]
                both = _nn_dot(jnp.concatenate([split[c][older], split[c][newer]], axis=1),
                               tri_pair)
                sums[c][older], sums[c][newer] = both[:, :KEY_BLOCK], both[:, KEY_BLOCK:]
                todo = todo[2:]
            else:
                sums[c][todo[0]] = _nn_dot(split[c][todo[0]], tri)
                todo = todo[1:]
    carries, ws = [], []
    for c in range(n):
        carry = None if first else carry_ref[slots[c]]
        pieces = [None] * n_blk
        for b in newest_first:
            log_w = zs[c][:, b * KEY_BLOCK:(b + 1) * KEY_BLOCK] - sums[c][b]
            total = sums[c][b][:, 0:1]
            if carry is not None:
                log_w = log_w - carry
                total = total + carry
            w = jnp.exp(log_w)
            mask = block_mask(c, b)
            if mask is not None:
                w = jnp.where(mask, w, 0.0)
            pieces[b] = w.astype(BF16)
            carry = total
        carries.append(carry)
        ws.append(pieces[0] if n_blk == 1 else jnp.concatenate(pieces, axis=1))
    for c in range(n):
        part = _nn_dot(ws[c], vs[c])
        acc_ref[slots[c]] = part if first else acc_ref[slots[c]] + part
        carry_ref[slots[c]] = carries[c]
    return carries


def _sb_prompt_kernel(q_ref, g_ref, kt_ref, v_ref, tri_ref, tri_pair_ref, out_ref, carry_ref,
                      acc_ref):
    step = pl.program_id(1)
    r, n_qb, n_win = SB_Q_ROWS, SB_STEP_BLOCKS, SB_WINDOW_BLOCKS
    lane_sl = [slice(p * LANES, (p + 1) * LANES) for p in range(q_ref.shape[1] // LANES)]
    n_lane = len(lane_sl)
    in_head = _head_masks(r)
    row = lax.broadcasted_iota(jnp.int32, (HEADS_PER_TILE * r, KEY_BLOCK), 0) % r
    col = lax.broadcasted_iota(jnp.int32, (HEADS_PER_TILE * r, KEY_BLOCK), 1)
    causal = col < row
    tri = tri_ref[...]
    tri_pair = tri_pair_ref[...]
    chains = [(a, p) for a in range(n_qb) for p in range(n_lane)]
    q2s = [_stack_heads(q_ref[a * r:(a + 1) * r, lane_sl[p]], in_head) for a, p in chains]

    def keys(first_blk, n_blk, sl):
        return jnp.concatenate([kt_ref[first_blk + u, sl, :] for u in range(n_blk)], axis=1)

    def values(first_blk, n_blk, sl):
        if isinstance(first_blk, int):
            return v_ref[first_blk * KEY_BLOCK:(first_blk + n_blk) * KEY_BLOCK, sl]
        start = pl.multiple_of(first_blk * KEY_BLOCK, KEY_BLOCK)
        return v_ref[pl.ds(start, n_blk * KEY_BLOCK), sl]

    def run_windows(which, first_blk, n_blk):
        for a in sorted({chains[c][0] for c in which}):
            mine = [c for c in which if chains[c][0] == a]
            _sb_window([q2s[c] for c in mine],
                       [keys(first_blk(a), n_blk(a), lane_sl[chains[c][1]]) for c in mine],
                       [values(first_blk(a), n_blk(a), lane_sl[chains[c][1]]) for c in mine],
                       tri, tri_pair, causal, [None] * len(mine), mine, carry_ref, acc_ref,
                       True)

    every = list(range(len(chains)))

    @pl.when(step == 0)
    def _():
        run_windows(every, lambda a: max(a + 1 - n_win, 0), lambda a: min(a + 1, n_win))

    @pl.when(step > 0)
    def _():
        _sb_window(q2s,
                   [keys(step * n_qb + a + 1 - n_win, n_win, lane_sl[p]) for a, p in chains],
                   [values(step * n_qb + a + 1 - n_win, n_win, lane_sl[p]) for a, p in chains],
                   tri, tri_pair, causal, [None] * len(chains), every, carry_ref, acc_ref,
                   True)

    n_more = step * n_qb + n_qb - n_win

    def cond(loop_state):
        it, smallest = loop_state
        return (it < n_more) & (smallest < SB_UNDERFLOW)

    def body(loop_state):
        it, _ = loop_state
        blks = [step * n_qb + a - n_win - it for a in range(n_qb)]
        valid = [None if a == n_qb - 1 else blks[a] >= 0 for a in range(n_qb)]
        blks = [jnp.maximum(blk, 0) for blk in blks]
        carries = _sb_window(q2s,
                             [keys(blks[a], 1, lane_sl[p]) for a, p in chains],
                             [values(blks[a], 1, lane_sl[p]) for a, p in chains],
                             tri, tri_pair, None, [valid[a] for a, _ in chains], every,
                             carry_ref, acc_ref, False)
        return it + 1, _sb_min_carry(carries)

    lax.while_loop(cond, body,
                   (jnp.int32(0), _sb_min_carry([carry_ref[c] for c in every])))
    for c, (a, p) in enumerate(chains):
        rows = slice(a * r, (a + 1) * r)
        o_pair = _unstack_heads(acc_ref[c], in_head)
        out_ref[rows, lane_sl[p]] = (o_pair * _silu(g_ref[rows, lane_sl[p]])).astype(BF16)


def _sb_prompt(q16, g, kt16, v16):
    bsz, seq, db = q16.shape
    step_rows = SB_STEP_BLOCKS * SB_Q_ROWS
    n_chains = SB_STEP_BLOCKS * (db // LANES)
    assert SB_Q_ROWS == KEY_BLOCK and seq % step_rows == 0 and db % LANES == 0
    qblk = lambda b, i: (b, i, 0)
    whole = lambda b, i: (b, 0, 0)
    whole4 = lambda b, i: (b, 0, 0, 0)
    const = lambda b, i: (0, 0)
    return pl.pallas_call(
        _sb_prompt_kernel,
        out_shape=jax.ShapeDtypeStruct((bsz, seq, db), BF16),
        grid=(bsz, seq // step_rows),
        in_specs=[pl.BlockSpec((None, step_rows, db), qblk),
                  pl.BlockSpec((None, step_rows, db), qblk),
                  pl.BlockSpec((None, seq // KEY_BLOCK, db, KEY_BLOCK), whole4),
                  pl.BlockSpec((None, seq, db), whole),
                  pl.BlockSpec((2 * KEY_BLOCK, KEY_BLOCK), const),
                  pl.BlockSpec((4 * KEY_BLOCK, 2 * KEY_BLOCK), const)],
        out_specs=pl.BlockSpec((None, step_rows, db), qblk),
        scratch_shapes=[pltpu.VMEM((n_chains, HEADS_PER_TILE * SB_Q_ROWS, 1), F32),
                        pltpu.VMEM((n_chains, HEADS_PER_TILE * SB_Q_ROWS, LANES), F32)],
        compiler_params=_params(2),
        name="sb_prompt",
    )(q16, g, kt16, v16, _suffix_matrix(KEY_BLOCK), _suffix_pair_matrix(KEY_BLOCK))


def _sb_sample_kernel(q_ref, g_ref, nk_ref, nv_ref, ckt_ref, cvt_ref, trid_ref, trif_ref,
                      out_ref, carry_ref, acc_ref):
    td = trid_ref.shape[1]
    tk = trif_ref.shape[1]
    n_full = ckt_ref.shape[1] // tk
    trif = trif_ref[...]
    lane_sl, in_head, causal, q2s = _sb_setup(q_ref, td)

    carries = _sb_tiles(q2s, [([_pad_rows(nk_ref[:, sl], td) for sl in lane_sl],
                               [_pad_rows(nv_ref[:, sl], td) for sl in lane_sl],
                               trid_ref[...], causal)], carry_ref, acc_ref, True, False, False)
    smallest = _sb_min_carry(carries)

    for tile in reversed(range(n_full)):
        cols = slice(tile * tk, (tile + 1) * tk)

        def step(cols=cols):
            kts = [ckt_ref[sl, cols].astype(BF16) for sl in lane_sl]
            vts = [cvt_ref[sl, cols].astype(BF16) for sl in lane_sl]
            return _sb_min_carry(_sb_tiles(q2s, [(kts, vts, trif, None)], carry_ref, acc_ref,
                                           False, True, True))

        smallest = lax.cond(smallest < SB_UNDERFLOW, step, lambda s=smallest: s)
    _sb_store(acc_ref, lane_sl, in_head, g_ref, out_ref)


def _sb_sample(q16, g, nk16, nv16, cache_kt, cache_vt, layer):
    bsz, t, db = q16.shape
    past = cache_kt.shape[3]
    width, tk, td = SB_WIDTH, SB_TILE, KEY_BLOCK
    assert t <= td and t % 16 == 0 and past % tk == 0 and db % width == 0
    blk = lambda b, p: (b, 0, p)
    cblk = lambda b, p: (layer, b, p, 0)
    const = lambda b, p: (0, 0)
    return pl.pallas_call(
        _sb_sample_kernel,
        out_shape=jax.ShapeDtypeStruct((bsz, t, db), BF16),
        grid=(bsz, db // width),
        in_specs=[pl.BlockSpec((None, t, width), blk),
                  pl.BlockSpec((None, t, width), blk),
                  pl.BlockSpec((None, t, width), blk),
                  pl.BlockSpec((None, t, width), blk),
                  pl.BlockSpec((None, None, width, past), cblk),
                  pl.BlockSpec((None, None, width, past), cblk),
                  pl.BlockSpec((2 * td, td), const),
                  pl.BlockSpec((2 * tk, tk), const)],
        out_specs=pl.BlockSpec((None, t, width), blk),
        scratch_shapes=_sb_scratch(t, width),
        compiler_params=_params(2),
        name="sb_sample",
    )(q16, g, nk16, nv16, cache_kt, cache_vt, _suffix_matrix(td), _suffix_matrix(tk))


def _row_tile(t):
    return ROW_TILE if t % ROW_TILE == 0 else t


def _cache_transposed(cache):
    l, b, s, h, dh = cache.shape
    return jnp.transpose(cache, (0, 1, 3, 4, 2)).reshape(l, b, h * dh, s)


def _cache_layout(stacked, n_heads):
    l, b, hd, s = stacked.shape
    return jnp.transpose(stacked.reshape(l, b, n_heads, hd // n_heads, s), (0, 1, 4, 2, 3))


def kernel(x_prompt, x_sample, p_prompt, p_sample, cache_a_k, cache_a_v, cache_b_k, cache_b_v,
           g_pre, w_in, rel_bias, w_out, g_post, w_ple, w_ple_gate):
    depth = w_in.shape[0]
    bsz, seq, d = x_prompt.shape
    dbsz, dseq, _ = x_sample.shape
    past = cache_b_k.shape[2]
    keep = min(BAND_PAST, seq)
    da = w_in.shape[2] // 8
    n_heads = da // HEAD_DIM
    tp, ts = bsz * seq, dbsz * dseq

    hp = x_prompt
    hs = x_sample.reshape(ts, d)
    p_prompt = p_prompt.reshape(depth, tp, -1)
    p_sample = p_sample.reshape(depth, ts, -1)
    cache_a_kt, cache_a_vt, cache_b_kt, cache_b_vt = (
        _cache_transposed(c) for c in (cache_a_k, cache_a_v, cache_b_k, cache_b_v))
    stacked = ()
    sample_stacked = ()
    w_in16, w_out16, w_gate16, w_ple16 = (
        w.astype(BF16) for w in (w_in, w_out, w_ple_gate, w_ple))
    for i in range(depth):
        (qa, va16, ga, qb, vb16, gb, kat16, kbt16, *stacked) = _inproj_prompt(
            hp, g_pre[i], w_in16, i, tuple(stacked), keep)
        oa = _band_prompt(qa, ga, kat16, va16, rel_bias[i])
        ob = _sb_prompt(qb, gb, kbt16, vb16)
        hp = _finish(oa.reshape(tp, da), ob.reshape(tp, da), hp.reshape(tp, d),
                     p_prompt, i, w_out16, g_post, w_gate16, w_ple16,
                     _row_tile(tp)).reshape(bsz, seq, d)

        (qa, ga, qb, gb, ka16, va16, kb16, vb16, *sample_stacked) = _inproj(
            hs, g_pre, w_in16, i, tuple(sample_stacked), _row_tile(ts))
        s3 = lambda a: a.reshape(dbsz, dseq, da)
        oa = _band_sample(s3(qa), s3(ga), cache_a_kt, cache_a_vt, i, s3(ka16), s3(va16),
                          rel_bias[i], past)
        ob = _sb_sample(s3(qb), s3(gb), s3(kb16), s3(vb16), cache_b_kt, cache_b_vt, i)
        hs = _finish(oa.reshape(ts, da), ob.reshape(ts, da), hs, p_sample, i,
                     w_out16, g_post, w_gate16, w_ple16, _row_tile(ts))

    kb_st, vb_st, ka_st, va_st = stacked
    s5 = lambda a: a.reshape(depth, dbsz, dseq, n_heads, HEAD_DIM)
    return (hp, hs.reshape(dbsz, dseq, d),
            _cache_layout(ka_st, n_heads), _cache_layout(va_st, n_heads),
            _cache_layout(kb_st, n_heads), _cache_layout(vb_st, n_heads),
            *(s5(a) for a in sample_stacked))
```
